```python
import jax
import jax.numpy as jnp
from jax import lax
import numpy as np

D_MODEL = 1024
BATCH = 8
SEQ = 4096
DEPTH = 2
DEC_BATCH = 16
DEC_SEQ = 4096
PAST_LEN = 128

D_MIX = D_MODEL
D_LRU = D_MIX // 2
D_RWKV = D_MIX - D_LRU
LRU_HEADS = 8
LRU_HD = D_LRU // LRU_HEADS
CONV_W = 4
LRU_C = 8.0
RWKV_HEAD = 64
RWKV_HEADS = D_RWKV // RWKV_HEAD
W_LORA = 64
A_LORA = 64
G_LORA = 128
N_EXPERTS = 16
N_GROUPS = 4
EXP_PER_GROUP = N_EXPERTS // N_GROUPS
TOP_K = 2
D_EXPERT = 512
MOE_BLOCK = 512
NORM_EPS = 1e-6
GN_EPS = 64e-5

D_RW_IN = 3 * D_RWKV + 2 * W_LORA + 2 * A_LORA + G_LORA
D_IN = 2 * D_LRU + D_RW_IN
RW_SPLITS = tuple(int(s) for s in np.cumsum([D_RWKV, D_RWKV, D_RWKV, W_LORA, W_LORA, A_LORA, A_LORA]))

kernel_name = 'hybrid_lru_rwkv7_moe_encoder'

F32 = jnp.float32


def rms_norm(x, g):
    xf = x.astype(F32)
    y = xf * lax.rsqrt(jnp.mean(xf * xf, axis=-1, keepdims=True) + NORM_EPS)
    return y * g.astype(F32)


def modulate(h, shift, scale):
    return h * (1.0 + scale[:, None, :]) + shift[:, None, :]


def centred_conv(u, w, b):
    T = u.shape[1]
    left = CONV_W // 2
    up = jnp.pad(u, ((0, 0), (left, CONV_W - 1 - left), (0, 0)))
    return sum(up[:, k:k + T] * w[k] for k in range(CONV_W)) + b


def token_shift(u, mu_prev, mu_next):
    prev = jnp.pad(u, ((0, 0), (1, 0), (0, 0)))[:, :-1]
    nxt = jnp.pad(u, ((0, 0), (0, 1), (0, 0)))[:, 1:]
    return u + mu_prev * (prev - u) + mu_next * (nxt - u)


def linear_scan(a, b, reverse):
    def comb(l, r):
        return (l[0] * r[0], r[0] * l[1] + r[1])
    return lax.associative_scan(comb, (a, b), reverse=reverse, axis=1)[1]


def rg_lru(u, wa, ba, wx, bx, lam, reverse):
    Bn, T, _ = u.shape
    ub = u.reshape(Bn, T, LRU_HEADS, LRU_HD)
    r = jax.nn.sigmoid(jnp.einsum('bthi,hij->bthj', ub, wa).reshape(Bn, T, D_LRU) + ba)
    i = jax.nn.sigmoid(jnp.einsum('bthi,hij->bthj', ub, wx).reshape(Bn, T, D_LRU) + bx)
    log_a = -LRU_C * r * jax.nn.softplus(-lam)
    a = jnp.exp(log_a)
    b = jnp.sqrt(-jnp.expm1(2.0 * log_a)) * (i * u)
    return linear_scan(a, b, reverse)


def wkv_scan(r, decay, k, v, kk, b, reverse):
    xs = tuple(jnp.swapaxes(t, 0, 1) for t in (r, decay, k, v, kk, b))
    Bn, H, N = r.shape[0], r.shape[2], r.shape[3]
    s0 = jnp.zeros((Bn, H, N, N), F32)

    def step(S, inp):
        rt, wt, kt, vt, kkt, bt = inp
        sa = jnp.einsum('bhij,bhj->bhi', S, -kkt)
        S = S * wt[:, :, None, :] + sa[..., None] * bt[:, :, None, :] + vt[..., None] * kt[:, :, None, :]
        return S, jnp.einsum('bhij,bhj->bhi', S, rt)

    _, ys = lax.scan(step, s0, xs, reverse=reverse)
    return jnp.swapaxes(ys, 0, 1)


def hybrid_mixer(h, w_in, w_out, conv_w, conv_b, lru_wa, lru_ba, lru_wx, lru_bx, lru_lam,
                 mu_prev, mu_next, w0, wup, a0, aup, gup, k_k, k_a, r_k, lnw, lnb):
    Bn, T, _ = h.shape
    proj = h @ w_in
    lru_x = proj[..., :D_LRU].astype(F32)
    lru_y = proj[..., D_LRU:2 * D_LRU].astype(F32)
    z = token_shift(proj[..., 2 * D_LRU:].astype(F32), mu_prev, mu_next)

    u = centred_conv(lru_x, conv_w, conv_b)
    hl = rg_lru(u, lru_wa[0], lru_ba[0], lru_wx[0], lru_bx[0], lru_lam[0], False) + \
        rg_lru(u, lru_wa[1], lru_ba[1], lru_wx[1], lru_bx[1], lru_lam[1], True)
    lru_out = hl * jax.nn.gelu(lru_y)

    r, k, v, wd_f, wd_b, ad_f, ad_b, gd = jnp.split(z, RW_SPLITS, axis=-1)

    def heads(t):
        return t.reshape(Bn, T, RWKV_HEADS, RWKV_HEAD)

    kk = heads(k * k_k)
    kk = kk / jnp.maximum(jnp.sqrt(jnp.sum(kk * kk, axis=-1, keepdims=True)), 1e-12)
    rh, vh = heads(r), heads(v)
    y = 0.0
    k_sum = 0.0
    for d, (wd, ad) in enumerate(((wd_f, ad_f), (wd_b, ad_b))):
        w_raw = -jax.nn.softplus(-(w0[d] + jnp.tanh(wd) @ wup[d])) - 0.5
        decay = jnp.exp(-jnp.exp(w_raw))
        a = jax.nn.sigmoid(a0[d] + ad @ aup[d])
        kd = heads(k * (1.0 + (a - 1.0) * k_a))
        y = y + wkv_scan(rh, heads(decay), kd, vh, kk, kk * heads(a), d == 1)
        k_sum = k_sum + kd
    mu = jnp.mean(y, axis=-1, keepdims=True)
    var = jnp.mean(jnp.square(y - mu), axis=-1, keepdims=True)
    gn = ((y - mu) * lax.rsqrt(var + GN_EPS)).reshape(Bn, T, D_RWKV) * lnw + lnb
    bonus = (jnp.sum(rh * k_sum * heads(jnp.broadcast_to(r_k, r.shape)), axis=-1, keepdims=True) * vh).reshape(Bn, T, D_RWKV)
    g = jax.nn.sigmoid(gd) @ gup
    rw_out = (gn + bonus) * g

    mix = jnp.concatenate([lru_out, rw_out], axis=-1).astype(h.dtype)
    return mix @ w_out


def moe(h, w_router, b_router, w_gate, w_up, w_down):
    Bn, T, D = h.shape
    xt = h.reshape(-1, D)
    n_tok = xt.shape[0]
    probs = jax.nn.softmax((xt @ w_router).astype(F32), axis=-1)
    sel = (probs + b_router.astype(F32)).reshape(n_tok, N_GROUPS, EXP_PER_GROUP)
    group_score = jnp.sum(lax.top_k(sel, TOP_K)[0], axis=-1)
    best_group = jnp.argmax(group_score, axis=-1)
    in_group = jnp.take_along_axis(sel, best_group[:, None, None], axis=1)[:, 0]
    _, local_idx = lax.top_k(in_group, TOP_K)
    e_idx = best_group[:, None] * EXP_PER_GROUP + local_idx
    gate = jnp.take_along_axis(probs, e_idx, axis=1)
    gate = gate / jnp.sum(gate, axis=-1, keepdims=True)

    n_asg = n_tok * TOP_K
    e_flat = e_idx.reshape(-1).astype(jnp.int32)
    tok_flat = jnp.repeat(jnp.arange(n_tok, dtype=jnp.int32), TOP_K)
    g_flat = gate.reshape(-1)
    order = jnp.argsort(e_flat)
    e_s, tok_s, g_s = e_flat[order], tok_flat[order], g_flat[order]
    counts = jnp.bincount(e_flat, length=N_EXPERTS)
    start = jnp.cumsum(counts) - counts
    padded = (counts + MOE_BLOCK - 1) // MOE_BLOCK * MOE_BLOCK
    pad_end = jnp.cumsum(padded)
    pad_start = pad_end - padded
    dest = pad_start[e_s] + jnp.arange(n_asg, dtype=jnp.int32) - start[e_s]
    n_blk = (n_asg + N_EXPERTS * (MOE_BLOCK - 1) + MOE_BLOCK - 1) // MOE_BLOCK
    n_rows = n_blk * MOE_BLOCK
    row_tok = jnp.zeros((n_rows,), jnp.int32).at[dest].set(tok_s)
    row_gate = jnp.zeros((n_rows,), F32).at[dest].set(g_s)
    blk_start = jnp.arange(n_blk, dtype=jnp.int32) * MOE_BLOCK
    blk_e = jnp.minimum(jnp.sum(pad_end[None, :] <= blk_start[:, None], axis=-1), N_EXPERTS - 1)
    xb = xt[row_tok].reshape(n_blk, MOE_BLOCK, D)

    def expert_block(args):
        xblk, e = args
        hid = jax.nn.silu(xblk @ w_gate[e]) * (xblk @ w_up[e])
        return hid @ w_down[e]

    yb = lax.map(expert_block, (xb, blk_e)).reshape(n_rows, D)
    y = jnp.zeros((n_tok, D), xt.dtype).at[row_tok].add((yb * row_gate[:, None]).astype(xt.dtype))
    return y.reshape(Bn, T, D)


def setup_inputs(seed: int = 0) -> dict:
    key = jax.random.key(seed)
    ks = iter(jax.random.split(key, 48))

    def nrm(shape, scale):
        return jax.random.normal(next(ks), shape, F32) * scale

    def uni(shape, lo, hi):
        return jax.random.uniform(next(ks), shape, F32, lo, hi)

    L = DEPTH
    a_init = uni((L, 2, D_LRU), 0.9, 0.999)
    return {
        'x_prompt': nrm((BATCH, SEQ, D_MODEL), 1.0),
        'x_sample': nrm((DEC_BATCH, DEC_SEQ, D_MODEL), 1.0),
        'c_prompt': nrm((BATCH, D_MODEL), 1.0),
        'c_sample': nrm((DEC_BATCH, D_MODEL), 1.0),
        'w_mod': nrm((L, D_MODEL, 6 * D_MODEL), 0.5 * D_MODEL ** -0.5),
        'b_mod': nrm((L, 6 * D_MODEL), 0.02),
        'norm1': 1.0 + nrm((L, D_MODEL), 0.02),
        'norm2': 1.0 + nrm((L, D_MODEL), 0.02),
        'w_in': nrm((L, D_MODEL, D_IN), D_MODEL ** -0.5),
        'w_out': nrm((L, D_MIX, D_MODEL), D_MIX ** -0.5),
        'conv_w': nrm((L, CONV_W, D_LRU), CONV_W ** -0.5),
        'conv_b': nrm((L, D_LRU), 0.02),
        'lru_wa': nrm((L, 2, LRU_HEADS, LRU_HD, LRU_HD), LRU_HD ** -0.5),
        'lru_ba': nrm((L, 2, D_LRU), 0.02),
        'lru_wx': nrm((L, 2, LRU_HEADS, LRU_HD, LRU_HD), LRU_HD ** -0.5),
        'lru_bx': nrm((L, 2, D_LRU), 0.02),
        'lru_lam': jnp.log(a_init) - jnp.log1p(-a_init),
        'mu_prev': uni((L, D_RW_IN), 0.0, 0.5),
        'mu_next': uni((L, D_RW_IN), 0.0, 0.5),
        'rw_w0': uni((L, 2, D_RWKV), -3.0, 1.0),
        'rw_wup': nrm((L, 2, W_LORA, D_RWKV), W_LORA ** -0.5),
        'rw_a0': nrm((L, 2, D_RWKV), 0.5),
        'rw_aup': nrm((L, 2, A_LORA, D_RWKV), A_LORA ** -0.5),
        'rw_gup': nrm((L, G_LORA, D_RWKV), G_LORA ** -0.5),
        'rw_kk': 0.85 + nrm((L, D_RWKV), 0.05),
        'rw_ka': 1.0 + nrm((L, D_RWKV), 0.05),
        'rw_rk': nrm((L, D_RWKV), 0.1),
        'ln_x_w': 1.0 + nrm((L, D_RWKV), 0.02),
        'ln_x_b': nrm((L, D_RWKV), 0.02),
        'w_router': nrm((D_MODEL, N_EXPERTS), D_MODEL ** -0.5),
        'b_router': nrm((N_EXPERTS,), 0.01),
        'exp_gate': nrm((L, N_EXPERTS, D_MODEL, D_EXPERT), D_MODEL ** -0.5),
        'exp_up': nrm((L, N_EXPERTS, D_MODEL, D_EXPERT), D_MODEL ** -0.5),
        'exp_down': nrm((L, N_EXPERTS, D_EXPERT, D_MODEL), D_EXPERT ** -0.5),
        'norm_f': 1.0 + nrm((D_MODEL,), 0.02),
    }


def reference(x_prompt, x_sample, c_prompt, c_sample, w_mod, b_mod, norm1, norm2, w_in, w_out,
              conv_w, conv_b, lru_wa, lru_ba, lru_wx, lru_bx, lru_lam, mu_prev, mu_next,
              rw_w0, rw_wup, rw_a0, rw_aup, rw_gup, rw_kk, rw_ka, rw_rk, ln_x_w, ln_x_b,
              w_router, b_router, exp_gate, exp_up, exp_down, norm_f):
    def run(x, c):
        cs = jax.nn.silu(c.astype(F32))
        for l in range(DEPTH):
            mod = cs @ w_mod[l] + b_mod[l]
            sh1, sc1, g1, sh2, sc2, g2 = jnp.split(mod, 6, axis=-1)
            h = modulate(rms_norm(x, norm1[l]), sh1, sc1).astype(x.dtype)
            mix = hybrid_mixer(h, w_in[l], w_out[l], conv_w[l], conv_b[l], lru_wa[l], lru_ba[l],
                               lru_wx[l], lru_bx[l], lru_lam[l], mu_prev[l], mu_next[l],
                               rw_w0[l], rw_wup[l], rw_a0[l], rw_aup[l], rw_gup[l],
                               rw_kk[l], rw_ka[l], rw_rk[l], ln_x_w[l], ln_x_b[l])
            x = x + (g1[:, None, :] * mix).astype(x.dtype)
            h = modulate(rms_norm(x, norm2[l]), sh2, sc2).astype(x.dtype)
            ff = moe(h, w_router, b_router, exp_gate[l], exp_up[l], exp_down[l])
            x = x + (g2[:, None, :] * ff).astype(x.dtype)
        return rms_norm(x, norm_f).astype(x.dtype)

    y_prompt = run(x_prompt, c_prompt)
    y_sample = run(x_sample, c_sample)
    return (y_prompt, y_sample)
```

```python
import functools

import jax
import jax.numpy as jnp
import numpy as np
from jax import lax
from jax.experimental import pallas as pl
from jax.experimental.pallas import tpu as pltpu

F32 = jnp.float32
BF16 = jnp.bfloat16

D_MODEL = 1024
D_LRU = 512
D_RWKV = 512
LRU_HEADS = 8
LRU_C = 8.0
RWKV_HEAD = 64
N_EXPERTS = 16
N_GROUPS = 4
EXP_PER_GROUP = 4
D_EXPERT = 512
MOE_BLOCK = 512
NORM_EPS = 1e-6
GN_EPS = 64e-5
D_RW_IN = 1920
D_IN = 2944

LANES = 128
SUBLANES = 8
CHUNK = 64
PAIR = 2 * RWKV_HEAD
N_PAIRS = D_RWKV // PAIR


def _cparams(sem, vmem_mb=48):
    return pltpu.CompilerParams(dimension_semantics=sem, vmem_limit_bytes=vmem_mb * 1024 * 1024)


def _dot(a, b):
    return jnp.dot(a, b, preferred_element_type=F32)


def _dot_nt(a, b):
    return lax.dot_general(a, b, (((1,), (1,)), ((), ())), preferred_element_type=F32)


def _split(x):
    hi = x.astype(BF16)
    lo = (x - hi.astype(F32)).astype(BF16)
    return hi, lo


def _dot_hilo(x, w_bf16):
    hi, lo = _split(x)
    return _dot(hi, w_bf16) + _dot(lo, w_bf16)


def _dot3(a, b):
    ah, al = _split(a)
    bh, bl = _split(b)
    return _dot(ah, bh) + (_dot(ah, bl) + _dot(al, bh))


def _dot3_nt(a, b):
    ah, al = _split(a)
    bh, bl = _split(b)
    return _dot_nt(ah, bh) + (_dot_nt(ah, bl) + _dot_nt(al, bh))


def _sigmoid(x):
    return 1.0 / (1.0 + jnp.exp(-x))


def _softplus(x):
    return jnp.maximum(x, 0.0) + jnp.log(1.0 + jnp.exp(-jnp.abs(x)))


def _rows(shape):
    return lax.broadcasted_iota(jnp.int32, shape, 0)


def _cols(shape):
    return lax.broadcasted_iota(jnp.int32, shape, 1)


def _mod_kernel(c_ref, w_ref, b_ref, o_ref):
    c = c_ref[...]
    cs = c * _sigmoid(c)
    o_ref[0] = _dot3(cs, w_ref[0]) + b_ref[0]


def _mod_call(c, w_mod, b_mod):
    nl, d, n6 = w_mod.shape
    bn = c.shape[0]
    tn = 1536
    return pl.pallas_call(
        _mod_kernel,
        grid=(nl, n6 // tn),
        in_specs=[
            pl.BlockSpec((bn, d), lambda l, j: (0, 0)),
            pl.BlockSpec((1, d, tn), lambda l, j: (l, 0, j)),
            pl.BlockSpec((1, 1, tn), lambda l, j: (l, 0, j)),
        ],
        out_specs=pl.BlockSpec((1, bn, tn), lambda l, j: (l, 0, j)),
        out_shape=jax.ShapeDtypeStruct((nl, bn, n6), F32),
        compiler_params=_cparams(("arbitrary", "arbitrary")),
        name="adaln_mod",
    )(c, w_mod, b_mod.reshape(nl, 1, n6))


def _norm_mod(x, nw, sc, sh):
    ms = jnp.mean(x * x, axis=-1, keepdims=True)
    return (x * lax.rsqrt(ms + NORM_EPS) * nw) * (1.0 + sc) + sh


def _inproj_kernel(x_ref, sh_ref, sc_ref, nw_ref, w_ref, lru_ref, rw_ref):
    h = _norm_mod(x_ref[0], nw_ref[...], sc_ref[0], sh_ref[0]).astype(BF16)
    lru_ref[0] = _dot(h, w_ref[:, : 2 * D_LRU])
    rw_ref[0] = _dot(h, w_ref[:, 2 * D_LRU:])


def _inproj_call(x, sh, sc, nw, w_in_bf16, tm):
    bn, t, d = x.shape
    return pl.pallas_call(
        _inproj_kernel,
        grid=(bn, t // tm),
        in_specs=[
            pl.BlockSpec((1, tm, d), lambda b, i: (b, i, 0)),
            pl.BlockSpec((1, 1, d), lambda b, i: (b, 0, 0)),
            pl.BlockSpec((1, 1, d), lambda b, i: (b, 0, 0)),
            pl.BlockSpec((1, d), lambda b, i: (0, 0)),
            pl.BlockSpec((d, D_IN), lambda b, i: (0, 0)),
        ],
        out_specs=[
            pl.BlockSpec((1, tm, 2 * D_LRU), lambda b, i: (b, i, 0)),
            pl.BlockSpec((1, tm, D_RW_IN), lambda b, i: (b, i, 0)),
        ],
        out_shape=[
            jax.ShapeDtypeStruct((bn, t, 2 * D_LRU), F32),
            jax.ShapeDtypeStruct((bn, t, D_RW_IN), F32),
        ],
        compiler_params=_cparams(("parallel", "parallel")),
        name="norm_inproj",
    )(x, sh, sc, nw, w_in_bf16)


def _shift_rows(x, s, fill, reverse):
    n = x.shape[0]
    r = _rows(x.shape)
    if reverse:
        return jnp.where(r >= n - s, fill, pltpu.roll(x, n - s, 0))
    return jnp.where(r < s, fill, pltpu.roll(x, s, 0))


def _lru_kernel(x_ref, prev_ref, next_ref, cw_ref, cb_ref, wg_ref, bg_ref, sp_ref, o_ref, carry_ref, *, n_tiles):
    d = pl.program_id(1)
    i = pl.program_id(2)
    ti = d * (n_tiles - 1) + (1 - 2 * d) * i
    tm = x_ref.shape[1]

    @pl.when(i == 0)
    def _():
        carry_ref[...] = jnp.zeros_like(carry_ref)

    x = x_ref[0]
    prev8 = jnp.where(ti == 0, 0.0, prev_ref[0])
    next8 = jnp.where(ti == n_tiles - 1, 0.0, next_ref[0])
    r = _rows(x.shape)
    xm1 = jnp.where(r == 0, prev8[7:8], pltpu.roll(x, 1, 0))
    xm2 = jnp.where(r == 0, prev8[6:7], jnp.where(r == 1, prev8[7:8], pltpu.roll(x, 2, 0)))
    xp1 = jnp.where(r == tm - 1, next8[0:1], pltpu.roll(x, tm - 1, 0))
    cw = cw_ref[...]
    u = cw[0:1] * xm2 + cw[1:2] * xm1 + cw[2:3] * x + cw[3:4] * xp1 + cb_ref[...]

    gates = _dot(u.astype(BF16), wg_ref[0]) + bg_ref[0]
    rg = _sigmoid(gates[:, :D_LRU])
    ig = _sigmoid(gates[:, D_LRU:])
    log_a = (-LRU_C) * rg * sp_ref[0]
    a = jnp.exp(log_a)
    bv = jnp.sqrt(1.0 - jnp.exp(2.0 * log_a)) * (ig * u)

    def scan(reverse):
        aa, bb = a, bv
        s = 1
        while s < tm:
            a_sh = _shift_rows(aa, s, 1.0, reverse)
            b_sh = _shift_rows(bb, s, 0.0, reverse)
            bb = aa * b_sh + bb
            aa = aa * a_sh
            s *= 2
        h = bb + aa * carry_ref[...]
        o_ref[0, 0] = h
        carry_ref[...] = h[0:1] if reverse else h[tm - 1: tm]

    @pl.when(d == 0)
    def _():
        scan(False)

    @pl.when(d == 1)
    def _():
        scan(True)


def _lru_call(proj_lru, conv_w, conv_b, wg, bg, sp, tm):
    bn, t, _ = proj_lru.shape
    nt = t // tm
    r8 = tm // SUBLANES
    n8 = t // SUBLANES

    def tile(d, i):
        return d * (nt - 1) + (1 - 2 * d) * i

    return pl.pallas_call(
        functools.partial(_lru_kernel, n_tiles=nt),
        grid=(bn, 2, nt),
        in_specs=[
            pl.BlockSpec((1, tm, D_LRU), lambda b, d, i: (b, tile(d, i), 0)),
            pl.BlockSpec((1, SUBLANES, D_LRU), lambda b, d, i: (b, jnp.maximum(tile(d, i) * r8 - 1, 0), 0)),
            pl.BlockSpec((1, SUBLANES, D_LRU), lambda b, d, i: (b, jnp.minimum((tile(d, i) + 1) * r8, n8 - 1), 0)),
            pl.BlockSpec((4, D_LRU), lambda b, d, i: (0, 0)),
            pl.BlockSpec((1, D_LRU), lambda b, d, i: (0, 0)),
            pl.BlockSpec((1, D_LRU, 2 * D_LRU), lambda b, d, i: (d, 0, 0)),
            pl.BlockSpec((1, 1, 2 * D_LRU), lambda b, d, i: (d, 0, 0)),
            pl.BlockSpec((1, 1, D_LRU), lambda b, d, i: (d, 0, 0)),
        ],
        out_specs=pl.BlockSpec((1, 1, tm, D_LRU), lambda b, d, i: (d, b, tile(d, i), 0)),
        out_shape=jax.ShapeDtypeStruct((2, bn, t, D_LRU), F32),
        scratch_shapes=[pltpu.VMEM((1, D_LRU), F32)],
        compiler_params=_cparams(("parallel", "arbitrary", "arbitrary")),
        name="rg_lru",
    )(proj_lru, proj_lru, proj_lru, conv_w, conv_b, wg, bg, sp)


def _rwprep_kernel(z_ref, prev_ref, next_ref, mup_ref, mun_ref, w0_ref, wup_ref, a0_ref, aup_ref, gup_ref,
                   kk_ref, ka_ref, rk_ref, ones_ref,
                   r_out, kkn_out, v_out, kd_out, kka_out, lw_out, bonus_out, g_out, *, n_tiles):
    i = pl.program_id(1)
    tm = z_ref.shape[1]
    zc = z_ref[0]
    prev8 = jnp.where(i == 0, 0.0, prev_ref[0])
    next8 = jnp.where(i == n_tiles - 1, 0.0, next_ref[0])
    rr = _rows(zc.shape)
    zp = jnp.where(rr == 0, prev8[7:8], pltpu.roll(zc, 1, 0))
    zn = jnp.where(rr == tm - 1, next8[0:1], pltpu.roll(zc, tm - 1, 0))
    z = zc + mup_ref[...] * (zp - zc) + mun_ref[...] * (zn - zc)

    r = z[:, 0:512]
    k = z[:, 512:1024]
    v = z[:, 1024:1536]
    wd = z[:, 1536:1664]
    ad = z[:, 1664:1792]
    gd = z[:, 1792:1920]
    ones_bd = ones_ref[...]

    kkr = k * kk_ref[...]
    ss = _dot_hilo(kkr * kkr, ones_bd)
    kkn = kkr / jnp.maximum(jnp.sqrt(ss), 1e-12)

    wlin = w0_ref[...] + _dot(jnp.tanh(wd).astype(BF16), wup_ref[...])
    lw = -jnp.exp(-_softplus(-wlin) - 0.5)
    a = _sigmoid(a0_ref[...] + _dot(ad.astype(BF16), aup_ref[...]))

    ka = ka_ref[...]
    k_sum = jnp.zeros_like(k)
    for d in range(2):
        a_d = a[:, d * D_RWKV:(d + 1) * D_RWKV]
        kd = k * (1.0 + (a_d - 1.0) * ka)
        kd_out[d, 0] = kd
        kka_out[d, 0] = kkn * a_d
        lw_out[d, 0] = lw[:, d * D_RWKV:(d + 1) * D_RWKV]
        k_sum = k_sum + kd
    r_out[0] = r
    kkn_out[0] = kkn
    v_out[0] = v
    bonus_out[0] = _dot_hilo(r * k_sum * rk_ref[...], ones_bd) * v
    g_out[0] = _dot(_sigmoid(gd).astype(BF16), gup_ref[...])


def _rwprep_call(proj_rw, mup, mun, w0, wup_bd, a0, aup_bd, gup, k_k, k_a, r_k, ones_bd, tm):
    bn, t, _ = proj_rw.shape
    nt = t // tm
    r8 = tm // SUBLANES
    n8 = t // SUBLANES
    c2 = lambda b, i: (0, 0)
    tok = pl.BlockSpec((1, tm, D_RWKV), lambda b, i: (b, i, 0))
    tok2 = pl.BlockSpec((2, 1, tm, D_RWKV), lambda b, i: (0, b, i, 0))
    s1 = jax.ShapeDtypeStruct((bn, t, D_RWKV), F32)
    s2 = jax.ShapeDtypeStruct((2, bn, t, D_RWKV), F32)
    return pl.pallas_call(
        functools.partial(_rwprep_kernel, n_tiles=nt),
        grid=(bn, nt),
        in_specs=[
            pl.BlockSpec((1, tm, D_RW_IN), lambda b, i: (b, i, 0)),
            pl.BlockSpec((1, SUBLANES, D_RW_IN), lambda b, i: (b, jnp.maximum(i * r8 - 1, 0), 0)),
            pl.BlockSpec((1, SUBLANES, D_RW_IN), lambda b, i: (b, jnp.minimum((i + 1) * r8, n8 - 1), 0)),
            pl.BlockSpec((1, D_RW_IN), c2),
            pl.BlockSpec((1, D_RW_IN), c2),
            pl.BlockSpec((1, 2 * D_RWKV), c2),
            pl.BlockSpec((LANES, 2 * D_RWKV), c2),
            pl.BlockSpec((1, 2 * D_RWKV), c2),
            pl.BlockSpec((LANES, 2 * D_RWKV), c2),
            pl.BlockSpec((LANES, D_RWKV), c2),
            pl.BlockSpec((1, D_RWKV), c2),
            pl.BlockSpec((1, D_RWKV), c2),
            pl.BlockSpec((1, D_RWKV), c2),
            pl.BlockSpec((D_RWKV, D_RWKV), c2),
        ],
        out_specs=[tok, tok, tok, tok2, tok2, tok2, tok, tok],
        out_shape=[s1, s1, s1, s2, s2, s2, s1, s1],
        compiler_params=_cparams(("parallel", "parallel")),
        name="rwkv_prep",
    )(proj_rw, proj_rw, proj_rw, mup, mun, w0, wup_bd, a0, aup_bd, gup, k_k, k_a, r_k, ones_bd)


def _stack2(y):
    yb = y.astype(BF16)
    lo = _cols(yb.shape) < RWKV_HEAD
    zero = jnp.zeros_like(yb)
    return jnp.concatenate([jnp.where(lo, yb, zero), jnp.where(lo, zero, yb)], axis=0)


def _pair_mm(x, y):
    return _dot(x.astype(BF16), _stack2(y))


class _Masks:
    def __init__(self, reverse):
        shp = (CHUNK, PAIR)
        t = _rows(shp)
        s = _cols(shp) % CHUNK
        self.strict = (s > t) if reverse else (s < t)
        self.incl = (s >= t) if reverse else (s <= t)
        self.eye = s == t
        self.blk16 = (t // 16) == (s // 16)
        self.lvl32 = ((t // 32) == (s // 32)) & ((t // 16) != (s // 16))
        self.lvl64 = (t // 32) != (s // 32)
        sq = (PAIR, PAIR)
        self.bd = (_rows(sq) // RWKV_HEAD) == (_cols(sq) // RWKV_HEAD)
        self.eye_sq = _rows(sq) == _cols(sq)
        tt = _rows((CHUNK, CHUNK))
        ss = _cols((CHUNK, CHUNK))
        self.tri = jnp.where((ss >= tt) if reverse else (ss <= tt), 1.0, 0.0).astype(BF16)
        self.reverse = reverse


def _wkv_local(r, kk, v, kd, kka, lw, m):
    cum = _dot_hilo_l(m.tri, lw)
    tot = cum[0:1] if m.reverse else cum[CHUNK - 1: CHUNK]
    p_in = jnp.exp(cum)
    rb = r * p_in
    ab = -kk * jnp.exp(cum - lw)
    ip = jnp.exp(-cum)
    bt = kka * ip
    kt = kd * ip
    ph = jnp.exp(tot - cum)
    bh = kka * ph
    kh = kd * ph

    lhs = jnp.concatenate([ab, rb], axis=0).astype(BF16)
    rhs_t = jnp.concatenate([_stack2(bt), _stack2(kt)], axis=0)
    s_all = _dot_nt(lhs, rhs_t)
    n_ab = jnp.where(m.strict, s_all[:CHUNK, :PAIR], 0.0)
    a_ak = jnp.where(m.strict, s_all[:CHUNK, PAIR:], 0.0)
    m_rb = jnp.where(m.incl, s_all[CHUNK:, :PAIR], 0.0)
    m_rk = jnp.where(m.incl, s_all[CHUNK:, PAIR:], 0.0)

    nd = jnp.where(m.blk16, n_ab, 0.0)
    t_inv = jnp.where(m.eye, 1.0, nd)
    pw = _pair_mm(nd, nd)
    t_inv = t_inv + _pair_mm(t_inv, pw)
    pw = _pair_mm(pw, pw)
    t_inv = t_inv + _pair_mm(t_inv, pw)
    pw = _pair_mm(pw, pw)
    t_inv = t_inv + _pair_mm(t_inv, pw)
    for lvl in (m.lvl32, m.lvl64):
        c = jnp.where(lvl, n_ab, 0.0)
        t_inv = t_inv + _pair_mm(_pair_mm(t_inv, c), t_inv)

    v2 = _stack2(v)
    akv = _dot(a_ak.astype(BF16), v2)
    wu = _dot(t_inv.astype(BF16), jnp.concatenate([_stack2(ab), _stack2(akv)], axis=1))
    w_m = wu[:, :PAIR]
    u_v = wu[:, PAIR:]
    qy = _dot(m_rb.astype(BF16), jnp.concatenate([_stack2(w_m), _stack2(u_v)], axis=1))
    q_hat = rb + qy[:, :PAIR]
    y_loc = qy[:, PAIR:] + _dot(m_rk.astype(BF16), v2)

    gd = _dot(bh.T.astype(BF16), wu.astype(BF16))
    kv = _dot(kh.T.astype(BF16), v.astype(BF16))
    g_m = jnp.where(m.bd, gd[:, :PAIR], 0.0) + jnp.where(m.eye_sq, jnp.exp(tot), 0.0)
    d_m = jnp.where(m.bd, gd[:, PAIR:] + kv, 0.0)
    return q_hat, y_loc, g_m, d_m


def _dot_hilo_l(w_bf16, x):
    hi, lo = _split(x)
    return _dot(w_bf16, hi) + _dot(w_bf16, lo)


def _wkv_kernel(rf_ref, kkf_ref, vf_ref, kdf_ref, kkaf_ref, lwf_ref,
                rb_ref, kkb_ref, vb_ref, kdb_ref, kkab_ref, lwb_ref,
                yf_ref, yb_ref, h_ref, *, n_sub):
    @pl.when(pl.program_id(1) == 0)
    def _():
        h_ref[...] = jnp.zeros_like(h_ref)

    dirs = (
        (False, rf_ref, kkf_ref, vf_ref, kdf_ref, kkaf_ref, lwf_ref, yf_ref),
        (True, rb_ref, kkb_ref, vb_ref, kdb_ref, kkab_ref, lwb_ref, yb_ref),
    )
    for di, (reverse, r_ref, kk_ref, v_ref, kd_ref, kka_ref, lw_ref, y_ref) in enumerate(dirs):
        m = _Masks(reverse)
        subs = range(n_sub - 1, -1, -1) if reverse else range(n_sub)
        for p in range(N_PAIRS):
            cs = slice(p * PAIR, (p + 1) * PAIR)
            h = h_ref[di, p]
            for j in subs:
                rs = slice(j * CHUNK, (j + 1) * CHUNK)
                q_hat, y_loc, g_m, d_m = _wkv_local(
                    r_ref[0, rs, cs], kk_ref[0, rs, cs], v_ref[0, rs, cs],
                    kd_ref[0, 0, rs, cs], kka_ref[0, 0, rs, cs], lw_ref[0, 0, rs, cs], m)
                hb = h.astype(BF16)
                y_ref[0, rs, cs] = _dot(q_hat.astype(BF16), hb) + y_loc
                h = _dot(g_m.astype(BF16), hb) + d_m
            h_ref[di, p] = h


def _wkv_call(r, kkn, v, kd, kka, lw, tm):
    bn, t, _ = r.shape
    nt = t // tm
    fwd = pl.BlockSpec((1, tm, D_RWKV), lambda b, i: (b, i, 0))
    bwd = pl.BlockSpec((1, tm, D_RWKV), lambda b, i: (b, nt - 1 - i, 0))
    fwd2 = pl.BlockSpec((1, 1, tm, D_RWKV), lambda b, i: (0, b, i, 0))
    bwd2 = pl.BlockSpec((1, 1, tm, D_RWKV), lambda b, i: (1, b, nt - 1 - i, 0))
    s1 = jax.ShapeDtypeStruct((bn, t, D_RWKV), F32)
    return pl.pallas_call(
        functools.partial(_wkv_kernel, n_sub=tm // CHUNK),
        grid=(bn, nt),
        in_specs=[fwd, fwd, fwd, fwd2, fwd2, fwd2, bwd, bwd, bwd, bwd2, bwd2, bwd2],
        out_specs=[fwd, bwd],
        out_shape=[s1, s1],
        scratch_shapes=[pltpu.VMEM((2, N_PAIRS, PAIR, PAIR), F32)],
        compiler_params=_cparams(("parallel", "arbitrary")),
        name="wkv7_chunked",
    )(r, kkn, v, kd, kka, lw, r, kkn, v, kd, kka, lw)


def _gelu_tanh(x):
    return 0.5 * x * (1.0 + jnp.tanh(0.7978845608028654 * (x + 0.044715 * (x * x * x))))


def _route(logits_t, b_col):
    rows = [logits_t[e:e + 1] for e in range(N_EXPERTS)]
    mx = functools.reduce(jnp.maximum, rows)
    ex = [jnp.exp(x - mx) for x in rows]
    den = functools.reduce(lambda a, b: a + b, ex)
    probs = [e / den for e in ex]
    sel = [probs[e] + b_col[e:e + 1] for e in range(N_EXPERTS)]
    scores = []
    for g in range(N_GROUPS):
        a, b, c, d = sel[4 * g: 4 * g + 4]
        hi1, lo1 = jnp.maximum(a, b), jnp.minimum(a, b)
        hi2, lo2 = jnp.maximum(c, d), jnp.minimum(c, d)
        scores.append(jnp.maximum(hi1, hi2) + jnp.maximum(jnp.minimum(hi1, hi2), jnp.maximum(lo1, lo2)))
    best = scores[0]
    bg = jnp.zeros_like(best)
    for g in range(1, N_GROUPS):
        upd = scores[g] > best
        best = jnp.where(upd, scores[g], best)
        bg = jnp.where(upd, float(g), bg)

    def pick(vals):
        out = []
        for j in range(EXP_PER_GROUP):
            x = vals[j]
            for g in range(1, N_GROUPS):
                x = jnp.where(bg == float(g), vals[4 * g + j], x)
            out.append(x)
        return out

    sg = pick(sel)
    pg = pick(probs)
    v1, i1 = sg[0], jnp.zeros_like(best)
    for j in range(1, EXP_PER_GROUP):
        upd = sg[j] > v1
        v1 = jnp.where(upd, sg[j], v1)
        i1 = jnp.where(upd, float(j), i1)
    neg = jnp.full_like(best, -jnp.inf)
    v2, i2 = neg, jnp.zeros_like(best)
    for j in range(EXP_PER_GROUP):
        cand = jnp.where(i1 == float(j), neg, sg[j])
        upd = cand > v2
        v2 = jnp.where(upd, cand, v2)
        i2 = jnp.where(upd, float(j), i2)
    p1 = jnp.zeros_like(best)
    p2 = jnp.zeros_like(best)
    for j in range(EXP_PER_GROUP):
        p1 = jnp.where(i1 == float(j), pg[j], p1)
        p2 = jnp.where(i2 == float(j), pg[j], p2)
    tot = p1 + p2
    return bg * float(EXP_PER_GROUP) + i1, bg * float(EXP_PER_GROUP) + i2, p1 / tot, p2 / tot


def _outproj_kernel(hl_ref, gate_ref, yf_ref, yb_ref, bonus_ref, g_ref, x_ref, lnw_ref, lnb_ref, g1_ref,
                    sh2_ref, sc2_ref, n2_ref, wout_ref, wr_ref, br_ref, ones_ref,
                    xn_ref, h2_ref, route_ref):
    lru_out = (hl_ref[0, 0] + hl_ref[1, 0]) * _gelu_tanh(gate_ref[0])
    ones_bd = ones_ref[...]
    y = yf_ref[0] + yb_ref[0]
    mu = _dot_hilo(y, ones_bd) * (1.0 / RWKV_HEAD)
    yc = y - mu
    var = _dot_hilo(yc * yc, ones_bd) * (1.0 / RWKV_HEAD)
    gn = yc * lax.rsqrt(var + GN_EPS) * lnw_ref[...] + lnb_ref[...]
    rw_out = (gn + bonus_ref[0]) * g_ref[0]
    o = _dot(lru_out.astype(BF16), wout_ref[:D_LRU]) + _dot(rw_out.astype(BF16), wout_ref[D_LRU:])
    xn = x_ref[0] + g1_ref[0] * o
    xn_ref[0] = xn
    h2 = _norm_mod(xn, n2_ref[...], sc2_ref[0], sh2_ref[0])
    h2_ref[0] = h2
    logits_t = _dot3_nt(wr_ref[...], h2)
    e0, e1, g0, g1 = _route(logits_t, br_ref[...])
    zero = jnp.zeros_like(e0)
    route_ref[0, 0] = jnp.concatenate([e0, e1, g0, g1, zero, zero, zero, zero], axis=0)


def _outproj_call(hl, proj_lru, yf, yb, bonus, g, x, lnw, lnb, g1, sh2, sc2, n2, w_out_bf16, wr_t, br, ones_bd, tm):
    bn, t, d = x.shape
    nt = t // tm
    c2 = lambda b, i: (0, 0)
    tok = pl.BlockSpec((1, tm, D_RWKV), lambda b, i: (b, i, 0))
    tokd = pl.BlockSpec((1, tm, d), lambda b, i: (b, i, 0))
    per_b = pl.BlockSpec((1, 1, d), lambda b, i: (b, 0, 0))
    return pl.pallas_call(
        _outproj_kernel,
        grid=(bn, nt),
        in_specs=[
            pl.BlockSpec((2, 1, tm, D_LRU), lambda b, i: (0, b, i, 0)),
            pl.BlockSpec((1, tm, D_LRU), lambda b, i: (b, i, 1)),
            tok, tok, tok, tok, tokd,
            pl.BlockSpec((1, D_RWKV), c2),
            pl.BlockSpec((1, D_RWKV), c2),
            per_b, per_b, per_b,
            pl.BlockSpec((1, d), c2),
            pl.BlockSpec((d, d), c2),
            pl.BlockSpec((N_EXPERTS, d), c2),
            pl.BlockSpec((N_EXPERTS, 1), c2),
            pl.BlockSpec((D_RWKV, D_RWKV), c2),
        ],
        out_specs=[tokd, tokd, pl.BlockSpec((1, 1, SUBLANES, tm), lambda b, i: (b, i, 0, 0))],
        out_shape=[
            jax.ShapeDtypeStruct((bn, t, d), F32),
            jax.ShapeDtypeStruct((bn, t, d), F32),
            jax.ShapeDtypeStruct((bn, nt, SUBLANES, tm), F32),
        ],
        compiler_params=_cparams(("parallel", "parallel")),
        name="outproj_router",
    )(hl, proj_lru, yf, yb, bonus, g, x, lnw, lnb, g1, sh2, sc2, n2, w_out_bf16, wr_t, br, ones_bd)


def _row_copy(src_ref, src_row, dst_ref, dst_row, sem):
    return pltpu.make_async_copy(src_ref.at[pl.ds(src_row, 1)], dst_ref.at[pl.ds(dst_row, 1)], sem)


def _dispatch_kernel(pos_ref, h_ref, xb_in_ref, xb_ref, sem):
    del xb_in_ref
    tm = h_ref.shape[0]

    def issue(r, carry):
        _row_copy(h_ref, r, xb_ref, pos_ref[0, 0, r], sem).start()
        _row_copy(h_ref, r, xb_ref, pos_ref[0, 0, tm + r], sem).start()
        return carry

    lax.fori_loop(0, tm, issue, 0)

    def drain(r, carry):
        _row_copy(h_ref, 0, xb_ref, 0, sem).wait()
        return carry

    lax.fori_loop(0, 2 * tm, drain, 0)


def _dispatch_call(h2, pos_tiles, n_rows, tm):
    n, d = h2.shape
    nt = n // tm
    xb0 = jnp.zeros((n_rows, d), F32)
    return pl.pallas_call(
        _dispatch_kernel,
        grid=(nt,),
        in_specs=[
            pl.BlockSpec((1, 1, 2 * tm), lambda i: (i, 0, 0), memory_space=pltpu.SMEM),
            pl.BlockSpec((tm, d), lambda i: (i, 0)),
            pl.BlockSpec(memory_space=pl.ANY),
        ],
        out_specs=pl.BlockSpec(memory_space=pl.ANY),
        out_shape=jax.ShapeDtypeStruct((n_rows, d), F32),
        scratch_shapes=[pltpu.SemaphoreType.DMA(())],
        input_output_aliases={2: 0},
        compiler_params=_cparams(("arbitrary",)),
        name="moe_dispatch",
    )(pos_tiles, h2, xb0)


def _expert_kernel(blk_e_ref, n_used_ref, x_ref, wg_ref, wu_ref, wd_ref, o_ref):
    i = pl.program_id(0)

    @pl.when(i < n_used_ref[0])
    def _():
        x = x_ref[...].astype(BF16)
        gate = _dot(x, wg_ref[0])
        hid = gate * _sigmoid(gate) * _dot(x, wu_ref[0])
        o_ref[...] = _dot(hid.astype(BF16), wd_ref[0])

    @pl.when(i >= n_used_ref[0])
    def _():
        o_ref[...] = jnp.zeros_like(o_ref)


def _expert_call(xb, blk_e, n_used, wg, wu, wd):
    n_rows, d = xb.shape
    n_blk = n_rows // MOE_BLOCK
    grid_spec = pltpu.PrefetchScalarGridSpec(
        num_scalar_prefetch=2,
        grid=(n_blk,),
        in_specs=[
            pl.BlockSpec((MOE_BLOCK, d), lambda i, be, nu: (i, 0)),
            pl.BlockSpec((1, d, D_EXPERT), lambda i, be, nu: (be[i], 0, 0)),
            pl.BlockSpec((1, d, D_EXPERT), lambda i, be, nu: (be[i], 0, 0)),
            pl.BlockSpec((1, D_EXPERT, d), lambda i, be, nu: (be[i], 0, 0)),
        ],
        out_specs=pl.BlockSpec((MOE_BLOCK, d), lambda i, be, nu: (i, 0)),
    )
    return pl.pallas_call(
        _expert_kernel,
        grid_spec=grid_spec,
        out_shape=jax.ShapeDtypeStruct((n_rows, d), F32),
        compiler_params=_cparams(("arbitrary",)),
        name="moe_experts",
    )(blk_e, n_used, xb, wg, wu, wd)


def _combine_kernel(pos_ref, gates_ref, x_ref, g2_ref, yb_ref, o_ref, buf_ref, sem):
    tm = x_ref.shape[0]

    def issue(r, carry):
        _row_copy(yb_ref, pos_ref[0, 0, r], buf_ref.at[0], r, sem).start()
        _row_copy(yb_ref, pos_ref[0, 0, tm + r], buf_ref.at[1], r, sem).start()
        return carry

    lax.fori_loop(0, tm, issue, 0)

    def drain(r, carry):
        _row_copy(yb_ref, 0, buf_ref.at[0], 0, sem).wait()
        return carry

    lax.fori_loop(0, 2 * tm, drain, 0)
    gts = gates_ref[...]
    y = gts[:, 0:1] * buf_ref[0] + gts[:, 1:2] * buf_ref[1]
    o_ref[...] = x_ref[...] + g2_ref[0] * y


def _combine_call(xn, yb, pos_tiles, gates, g2, t, tm):
    n, d = xn.shape
    nt = n // tm
    per_b = t // tm
    return pl.pallas_call(
        _combine_kernel,
        grid=(nt,),
        in_specs=[
            pl.BlockSpec((1, 1, 2 * tm), lambda i: (i, 0, 0), memory_space=pltpu.SMEM),
            pl.BlockSpec((tm, 2), lambda i: (i, 0)),
            pl.BlockSpec((tm, d), lambda i: (i, 0)),
            pl.BlockSpec((1, 1, d), lambda i: (i // per_b, 0, 0)),
            pl.BlockSpec(memory_space=pl.ANY),
        ],
        out_specs=pl.BlockSpec((tm, d), lambda i: (i, 0)),
        out_shape=jax.ShapeDtypeStruct((n, d), F32),
        scratch_shapes=[pltpu.VMEM((2, tm, d), F32), pltpu.SemaphoreType.DMA(())],
        compiler_params=_cparams(("arbitrary",)),
        name="moe_combine",
    )(pos_tiles, gates, xn, g2, yb)


def _final_kernel(x_ref, nw_ref, o_ref):
    x = x_ref[...]
    ms = jnp.mean(x * x, axis=-1, keepdims=True)
    o_ref[...] = x * lax.rsqrt(ms + NORM_EPS) * nw_ref[...]


def _final_call(x, nw, tm):
    n, d = x.shape
    return pl.pallas_call(
        _final_kernel,
        grid=(n // tm,),
        in_specs=[pl.BlockSpec((tm, d), lambda i: (i, 0)), pl.BlockSpec((1, d), lambda i: (0, 0))],
        out_specs=pl.BlockSpec((tm, d), lambda i: (i, 0)),
        out_shape=jax.ShapeDtypeStruct((n, d), F32),
        compiler_params=_cparams(("parallel",)),
        name="final_norm",
    )(x, nw)


def _block_diag(w):
    h, a, b = w.shape
    eye = jnp.eye(h, dtype=w.dtype)
    return jnp.einsum("hab,hg->hagb", w, eye).reshape(h * a, h * b)


def _head_ones():
    idx = np.arange(D_RWKV) // RWKV_HEAD
    return jnp.asarray((idx[:, None] == idx[None, :]).astype(np.float32), dtype=BF16)


def _routing_tables(route, n_tok, tm):
    bn, nt = route.shape[0], route.shape[1]
    flat = jnp.transpose(route, (2, 0, 1, 3)).reshape(SUBLANES, n_tok)
    e0 = flat[0].astype(jnp.int32)
    e1 = flat[1].astype(jnp.int32)
    gates = jnp.stack([flat[2], flat[3]], axis=1)
    ar = jnp.arange(N_EXPERTS, dtype=jnp.int32)
    oh = (e0[:, None] == ar).astype(jnp.int32) + (e1[:, None] == ar).astype(jnp.int32)
    cs = jnp.cumsum(oh, axis=0)
    counts = cs[-1]
    excl = cs - oh
    padded = (counts + MOE_BLOCK - 1) // MOE_BLOCK * MOE_BLOCK
    pad_end = jnp.cumsum(padded)
    pad_start = pad_end - padded
    base = excl + pad_start[None, :]
    pos0 = jnp.take_along_axis(base, e0[:, None], axis=1)[:, 0]
    pos1 = jnp.take_along_axis(base, e1[:, None], axis=1)[:, 0]
    n_asg = 2 * n_tok
    n_blk = (n_asg + N_EXPERTS * (MOE_BLOCK - 1) + MOE_BLOCK - 1) // MOE_BLOCK
    blk_start = jnp.arange(n_blk, dtype=jnp.int32) * MOE_BLOCK
    blk_e = jnp.minimum(jnp.sum(pad_end[None, :] <= blk_start[:, None], axis=-1), N_EXPERTS - 1).astype(jnp.int32)
    n_used = (pad_end[-1] // MOE_BLOCK).astype(jnp.int32).reshape(1)
    pos_tiles = jnp.concatenate(
        [pos0.reshape(n_tok // tm, 1, tm), pos1.reshape(n_tok // tm, 1, tm)], axis=2).astype(jnp.int32)
    return pos_tiles, gates, blk_e, n_used, n_blk * MOE_BLOCK


def _tile(t, want):
    return min(t, want)


def _trunk(x, c, w_mod, b_mod, norm1, norm2, w_in, w_out, conv_w, conv_b, lru_wa, lru_ba, lru_wx, lru_bx, lru_lam,
           mu_prev, mu_next, rw_w0, rw_wup, rw_a0, rw_aup, rw_gup, rw_kk, rw_ka, rw_rk, ln_x_w, ln_x_b,
           w_router, b_router, exp_gate, exp_up, exp_down, norm_f):
    bn, t, d = x.shape
    n_tok = bn * t
    depth = w_mod.shape[0]
    ones_bd = _head_ones()
    mod = _mod_call(c, w_mod, b_mod)
    wr_t = jnp.transpose(w_router)
    br = b_router.reshape(N_EXPERTS, 1)
    tm_moe = _tile(t, 256)

    for l in range(depth):
        sh1, sc1, g1, sh2, sc2, g2 = [m.reshape(bn, 1, d) for m in jnp.split(mod[l], 6, axis=-1)]
        proj_lru, proj_rw = _inproj_call(x, sh1, sc1, norm1[l].reshape(1, d), w_in[l].astype(BF16), _tile(t, 512))

        wg = jnp.stack([jnp.concatenate([_block_diag(lru_wa[l, dd]), _block_diag(lru_wx[l, dd])], axis=1)
                        for dd in range(2)]).astype(BF16)
        bg = jnp.concatenate([lru_ba[l], lru_bx[l]], axis=1).reshape(2, 1, 2 * D_LRU)
        sp = jax.nn.softplus(-lru_lam[l]).reshape(2, 1, D_LRU)
        hl = _lru_call(proj_lru, conv_w[l], conv_b[l].reshape(1, D_LRU), wg, bg, sp, _tile(t, 256))

        zeros = jnp.zeros((64, D_RWKV), F32)
        wup_bd = jnp.concatenate([jnp.concatenate([rw_wup[l, 0], zeros], axis=1),
                                  jnp.concatenate([zeros, rw_wup[l, 1]], axis=1)], axis=0).astype(BF16)
        aup_bd = jnp.concatenate([jnp.concatenate([rw_aup[l, 0], zeros], axis=1),
                                  jnp.concatenate([zeros, rw_aup[l, 1]], axis=1)], axis=0).astype(BF16)
        r, kkn, v, kd, kka, lw, bonus, g = _rwprep_call(
            proj_rw, mu_prev[l].reshape(1, -1), mu_next[l].reshape(1, -1),
            rw_w0[l].reshape(1, -1), wup_bd, rw_a0[l].reshape(1, -1), aup_bd, rw_gup[l].astype(BF16),
            rw_kk[l].reshape(1, -1), rw_ka[l].reshape(1, -1), rw_rk[l].reshape(1, -1), ones_bd, _tile(t, 256))
        yf, yb = _wkv_call(r, kkn, v, kd, kka, lw, _tile(t, 128))

        xn, h2, route = _outproj_call(
            hl, proj_lru, yf, yb, bonus, g, x, ln_x_w[l].reshape(1, -1), ln_x_b[l].reshape(1, -1),
            g1, sh2, sc2, norm2[l].reshape(1, d), w_out[l].astype(BF16), wr_t, br, ones_bd, tm_moe)

        pos_tiles, gates, blk_e, n_used, n_rows = _routing_tables(route, n_tok, tm_moe)
        xb =_dispatch_call(h2.reshape(n_tok, d), pos_tiles, n_rows, tm_moe)
        ybuf = _expert_call(xb, blk_e, n_used, exp_gate[l].astype(BF16), exp_up[l].astype(BF16),
                            exp_down[l].astype(BF16))
        x = _combine_call(xn.reshape(n_tok, d), ybuf, pos_tiles, gates, g2, t, tm_moe).reshape(bn, t, d)

    return _final_call(x.reshape(n_tok, d), norm_f.reshape(1, d), _tile(n_tok, 512)).reshape(bn, t, d)


def kernel(x_prompt, x_sample, c_prompt, c_sample, w_mod, b_mod, norm1, norm2, w_in, w_out, conv_w, conv_b, lru_wa, lru_ba, lru_wx, lru_bx, lru_lam, mu_prev, mu_next, rw_w0, rw_wup, rw_a0, rw_aup, rw_gup, rw_kk, rw_ka, rw_rk, ln_x_w, ln_x_b, w_router, b_router, exp_gate, exp_up, exp_down, norm_f):
    bp = x_prompt.shape[0]
    x = jnp.concatenate([x_prompt, x_sample], axis=0)
    c = jnp.concatenate([c_prompt, c_sample], axis=0).astype(F32)
    y = _trunk(x, c, w_mod, b_mod, norm1, norm2, w_in, w_out, conv_w, conv_b, lru_wa, lru_ba, lru_wx, lru_bx,
               lru_lam, mu_prev, mu_next, rw_w0, rw_wup, rw_a0, rw_aup, rw_gup, rw_kk, rw_ka, rw_rk, ln_x_w,
               ln_x_b, w_router, b_router, exp_gate, exp_up, exp_down, norm_f)
    return (y[:bp], y[bp:])
```

```python
import functools

import jax
import jax.numpy as jnp
import numpy as np
from jax import lax
from jax.experimental import pallas as pl
from jax.experimental.pallas import tpu as pltpu

F32 = jnp.float32
BF16 = jnp.bfloat16

D_MODEL = 1024
D_LRU = 512
D_RWKV = 512
LRU_HEADS = 8
LRU_C = 8.0
RWKV_HEAD = 64
N_EXPERTS = 16
N_GROUPS = 4
EXP_PER_GROUP = 4
D_EXPERT = 512
MOE_BLOCK = 512
NORM_EPS = 1e-6
GN_EPS = 64e-5
D_RW_IN = 1920
D_IN = 2944

LANES = 128
SUBLANES = 8
CHUNK = 64
PAIR = 2 * RWKV_HEAD
N_PAIRS = D_RWKV // PAIR


def _cparams(sem, vmem_mb=48):
    return pltpu.CompilerParams(dimension_semantics=sem, vmem_limit_bytes=vmem_mb * 1024 * 1024)


def _dot(a, b):
    return jnp.dot(a, b, preferred_element_type=F32)


def _dot_nt(a, b):
    return lax.dot_general(a, b, (((1,), (1,)), ((), ())), preferred_element_type=F32)


def _split(x):
    hi = x.astype(BF16)
    lo = (x - hi.astype(F32)).astype(BF16)
    return hi, lo


def _dot_hilo(x, w_bf16):
    hi, lo = _split(x)
    return _dot(hi, w_bf16) + _dot(lo, w_bf16)


def _dot3(a, b):
    ah, al = _split(a)
    bh, bl = _split(b)
    return _dot(ah, bh) + (_dot(ah, bl) + _dot(al, bh))


def _dot3_nt(a, b):
    ah, al = _split(a)
    bh, bl = _split(b)
    return _dot_nt(ah, bh) + (_dot_nt(ah, bl) + _dot_nt(al, bh))


def _sigmoid(x):
    return 1.0 / (1.0 + jnp.exp(-x))


def _softplus(x):
    return jnp.maximum(x, 0.0) + jnp.log(1.0 + jnp.exp(-jnp.abs(x)))


def _rows(shape):
    return lax.broadcasted_iota(jnp.int32, shape, 0)


def _cols(shape):
    return lax.broadcasted_iota(jnp.int32, shape, 1)


def _mod_kernel(c_ref, w_ref, b_ref, o_ref):
    c = c_ref[...]
    cs = c * _sigmoid(c)
    o_ref[0] = _dot3(cs, w_ref[0]) + b_ref[0]


def _mod_call(c, w_mod, b_mod):
    nl, d, n6 = w_mod.shape
    bn = c.shape[0]
    tn = 1536
    return pl.pallas_call(
        _mod_kernel,
        grid=(nl, n6 // tn),
        in_specs=[
            pl.BlockSpec((bn, d), lambda l, j: (0, 0)),
            pl.BlockSpec((1, d, tn), lambda l, j: (l, 0, j)),
            pl.BlockSpec((1, 1, tn), lambda l, j: (l, 0, j)),
        ],
        out_specs=pl.BlockSpec((1, bn, tn), lambda l, j: (l, 0, j)),
        out_shape=jax.ShapeDtypeStruct((nl, bn, n6), F32),
        compiler_params=_cparams(("arbitrary", "arbitrary")),
        name="adaln_mod",
    )(c, w_mod, b_mod.reshape(nl, 1, n6))


def _norm_mod(x, nw, sc, sh):
    ms = jnp.mean(x * x, axis=-1, keepdims=True)
    return (x * lax.rsqrt(ms + NORM_EPS) * nw) * (1.0 + sc) + sh


def _inproj_kernel(x_ref, sh_ref, sc_ref, nw_ref, w_ref, lru_ref, rw_ref):
    h = _norm_mod(x_ref[0], nw_ref[...], sc_ref[0], sh_ref[0]).astype(BF16)
    lru_ref[0] = _dot(h, w_ref[:, : 2 * D_LRU])
    rw_ref[0] = _dot(h, w_ref[:, 2 * D_LRU:])


def _inproj_call(x, sh, sc, nw, w_in_bf16, tm):
    bn, t, d = x.shape
    return pl.pallas_call(
        _inproj_kernel,
        grid=(bn, t // tm),
        in_specs=[
            pl.BlockSpec((1, tm, d), lambda b, i: (b, i, 0)),
            pl.BlockSpec((1, 1, d), lambda b, i: (b, 0, 0)),
            pl.BlockSpec((1, 1, d), lambda b, i: (b, 0, 0)),
            pl.BlockSpec((1, d), lambda b, i: (0, 0)),
            pl.BlockSpec((d, D_IN), lambda b, i: (0, 0)),
        ],
        out_specs=[
            pl.BlockSpec((1, tm, 2 * D_LRU), lambda b, i: (b, i, 0)),
            pl.BlockSpec((1, tm, D_RW_IN), lambda b, i: (b, i, 0)),
        ],
        out_shape=[
            jax.ShapeDtypeStruct((bn, t, 2 * D_LRU), F32),
            jax.ShapeDtypeStruct((bn, t, D_RW_IN), F32),
        ],
        compiler_params=_cparams(("parallel", "parallel")),
        name="norm_inproj",
    )(x, sh, sc, nw, w_in_bf16)


def _lru_kernel(x_ref, prev_ref, next_ref, cw_ref, cb_ref, wg_ref, bg_ref, sp_ref, o_ref,
                carry_ref, a_scr, b_scr, *, n_tiles):
    d = pl.program_id(1)
    i = pl.program_id(2)
    ti = d * (n_tiles - 1) + (1 - 2 * d) * i
    tm = x_ref.shape[1]

    @pl.when(i == 0)
    def _():
        carry_ref[...] = jnp.zeros_like(carry_ref)

    x = x_ref[0]
    prev8 = jnp.where(ti == 0, 0.0, prev_ref[0])
    next8 = jnp.where(ti == n_tiles - 1, 0.0, next_ref[0])
    r = _rows(x.shape)
    xm1 = jnp.where(r == 0, prev8[7:8], pltpu.roll(x, 1, 0))
    xm2 = jnp.where(r == 0, prev8[6:7], jnp.where(r == 1, prev8[7:8], pltpu.roll(x, 2, 0)))
    xp1 = jnp.where(r == tm - 1, next8[0:1], pltpu.roll(x, tm - 1, 0))
    cw = cw_ref[...]
    u = cw[0:1] * xm2 + cw[1:2] * xm1 + cw[2:3] * x + cw[3:4] * xp1 + cb_ref[...]

    gates = _dot(u.astype(BF16), wg_ref[0]) + bg_ref[0]
    rg = _sigmoid(gates[:, :D_LRU])
    ig = _sigmoid(gates[:, D_LRU:])
    log_a = (-LRU_C) * rg * sp_ref[0]
    a = jnp.exp(log_a)
    bv = jnp.sqrt(1.0 - jnp.exp(2.0 * log_a)) * (ig * u)

    n_lt = D_LRU // LANES
    for c in range(n_lt):
        a_scr[c] = a[:, c * LANES:(c + 1) * LANES]
        b_scr[c] = bv[:, c * LANES:(c + 1) * LANES]
    win = SUBLANES * SUBLANES

    def scan(reverse):
        r8 = _rows((SUBLANES, D_LRU))
        carry = carry_ref[...]
        n_win = tm // win
        for w in (range(n_win - 1, -1, -1) if reverse else range(n_win)):
            base = w * win
            hs = [jnp.zeros((SUBLANES, LANES), F32)] * n_lt
            accs = [jnp.ones((SUBLANES, LANES), F32)] * n_lt
            for g in (range(SUBLANES - 1, -1, -1) if reverse else range(SUBLANES)):
                rows = pl.ds(base + g, SUBLANES, stride=SUBLANES)
                for c in range(n_lt):
                    ag = a_scr[c, rows, :]
                    hs[c] = ag * hs[c] + b_scr[c, rows, :]
                    accs[c] = ag * accs[c]
                    b_scr[c, rows, :] = hs[c]
                    a_scr[c, rows, :] = accs[c]
            h = jnp.concatenate(hs, axis=1)
            acc = jnp.concatenate(accs, axis=1)
            s = 1
            while s < SUBLANES:
                keep = (r8 >= SUBLANES - s) if reverse else (r8 < s)
                sh = SUBLANES - s if reverse else s
                h = acc * jnp.where(keep, 0.0, pltpu.roll(h, sh, 0)) + h
                acc = acc * jnp.where(keep, 1.0, pltpu.roll(acc, sh, 0))
                s *= 2
            ends = h + acc * carry
            if reverse:
                enter = jnp.where(r8 == SUBLANES - 1, carry, pltpu.roll(ends, SUBLANES - 1, 0))
                carry = ends[0:1]
            else:
                enter = jnp.where(r8 == 0, carry, pltpu.roll(ends, 1, 0))
                carry = ends[SUBLANES - 1: SUBLANES]
            for r in range(SUBLANES):
                rs = slice(base + r * SUBLANES, base + (r + 1) * SUBLANES)
                for c in range(n_lt):
                    cs = slice(c * LANES, (c + 1) * LANES)
                    o_ref[0, 0, rs, cs] = b_scr[c, rs, :] + a_scr[c, rs, :] * enter[r:r + 1, cs]
        carry_ref[...] = carry

    @pl.when(d == 0)
    def _():
        scan(False)

    @pl.when(d == 1)
    def _():
        scan(True)


def _lru_call(proj_lru, conv_w, conv_b, wg, bg, sp, tm):
    bn, t, _ = proj_lru.shape
    nt = t // tm
    r8 = tm // SUBLANES
    n8 = t // SUBLANES

    def tile(d, i):
        return d * (nt - 1) + (1 - 2 * d) * i

    return pl.pallas_call(
        functools.partial(_lru_kernel, n_tiles=nt),
        grid=(bn, 2, nt),
        in_specs=[
            pl.BlockSpec((1, tm, D_LRU), lambda b, d, i: (b, tile(d, i), 0)),
            pl.BlockSpec((1, SUBLANES, D_LRU), lambda b, d, i: (b, jnp.maximum(tile(d, i) * r8 - 1, 0), 0)),
            pl.BlockSpec((1, SUBLANES, D_LRU), lambda b, d, i: (b, jnp.minimum((tile(d, i) + 1) * r8, n8 - 1), 0)),
            pl.BlockSpec((4, D_LRU), lambda b, d, i: (0, 0)),
            pl.BlockSpec((1, D_LRU), lambda b, d, i: (0, 0)),
            pl.BlockSpec((1, D_LRU, 2 * D_LRU), lambda b, d, i: (d, 0, 0)),
            pl.BlockSpec((1, 1, 2 * D_LRU), lambda b, d, i: (d, 0, 0)),
            pl.BlockSpec((1, 1, D_LRU), lambda b, d, i: (d, 0, 0)),
        ],
        out_specs=pl.BlockSpec((1, 1, tm, D_LRU), lambda b, d, i: (d, b, tile(d, i), 0)),
        out_shape=jax.ShapeDtypeStruct((2, bn, t, D_LRU), F32),
        scratch_shapes=[pltpu.VMEM((1, D_LRU), F32),
                        pltpu.VMEM((D_LRU // LANES, tm, LANES), F32),
                        pltpu.VMEM((D_LRU // LANES, tm, LANES), F32)],
        compiler_params=_cparams(("parallel", "arbitrary", "arbitrary")),
        name="rg_lru",
    )(proj_lru, proj_lru, proj_lru, conv_w, conv_b, wg, bg, sp)


def _rwprep_kernel(z_ref, prev_ref, next_ref, mup_ref, mun_ref, w0_ref, wup_ref, a0_ref, aup_ref, gup_ref,
                   kk_ref, ka_ref, rk_ref, ones_ref,
                   r_out, kkn_out, v_out, kd_out, kka_out, lw_out, bonus_out, g_out, *, n_tiles):
    i = pl.program_id(1)
    tm = z_ref.shape[1]
    zc = z_ref[0]
    prev8 = jnp.where(i == 0, 0.0, prev_ref[0])
    next8 = jnp.where(i == n_tiles - 1, 0.0, next_ref[0])
    rr = _rows(zc.shape)
    zp = jnp.where(rr == 0, prev8[7:8], pltpu.roll(zc, 1, 0))
    zn = jnp.where(rr == tm - 1, next8[0:1], pltpu.roll(zc, tm - 1, 0))
    z = zc + mup_ref[...] * (zp - zc) + mun_ref[...] * (zn - zc)

    r = z[:, 0:512]
    k = z[:, 512:1024]
    v = z[:, 1024:1536]
    wd = z[:, 1536:1664]
    ad = z[:, 1664:1792]
    gd = z[:, 1792:1920]
    ones_bd = ones_ref[...]

    kkr = k * kk_ref[...]
    ss = _dot_hilo(kkr * kkr, ones_bd)
    kkn = kkr / jnp.maximum(jnp.sqrt(ss), 1e-12)

    wlin = w0_ref[...] + _dot(jnp.tanh(wd).astype(BF16), wup_ref[...])
    lw = -jnp.exp(-_softplus(-wlin) - 0.5)
    a = _sigmoid(a0_ref[...] + _dot(ad.astype(BF16), aup_ref[...]))

    ka = ka_ref[...]
    k_sum = jnp.zeros_like(k)
    for d in range(2):
        a_d = a[:, d * D_RWKV:(d + 1) * D_RWKV]
        kd = k * (1.0 + (a_d - 1.0) * ka)
        kd_out[d, 0] = kd.astype(BF16)
        kka_out[d, 0] = (kkn * a_d).astype(BF16)
        lw_out[d, 0] = lw[:, d * D_RWKV:(d + 1) * D_RWKV]
        k_sum = k_sum + kd
    r_out[0] = r.astype(BF16)
    kkn_out[0] = kkn.astype(BF16)
    v_out[0] = v.astype(BF16)
    bonus_out[0] = (_dot_hilo(r * k_sum * rk_ref[...], ones_bd) * v).astype(BF16)
    g_out[0] = _dot(_sigmoid(gd).astype(BF16), gup_ref[...]).astype(BF16)


def _rwprep_call(proj_rw, mup, mun, w0, wup_bd, a0, aup_bd, gup, k_k, k_a, r_k, ones_bd, tm):
    bn, t, _ = proj_rw.shape
    nt = t // tm
    r8 = tm // SUBLANES
    n8 = t // SUBLANES
    c2 = lambda b, i: (0, 0)
    tok = pl.BlockSpec((1, tm, D_RWKV), lambda b, i: (b, i, 0))
    tok2 = pl.BlockSpec((2, 1, tm, D_RWKV), lambda b, i: (0, b, i, 0))
    s1 = jax.ShapeDtypeStruct((bn, t, D_RWKV), BF16)
    s2 = jax.ShapeDtypeStruct((2, bn, t, D_RWKV), BF16)
    s2f = jax.ShapeDtypeStruct((2, bn, t, D_RWKV), F32)
    return pl.pallas_call(
        functools.partial(_rwprep_kernel, n_tiles=nt),
        grid=(bn, nt),
        in_specs=[
            pl.BlockSpec((1, tm, D_RW_IN), lambda b, i: (b, i, 0)),
            pl.BlockSpec((1, SUBLANES, D_RW_IN), lambda b, i: (b, jnp.maximum(i * r8 - 1, 0), 0)),
            pl.BlockSpec((1, SUBLANES, D_RW_IN), lambda b, i: (b, jnp.minimum((i + 1) * r8, n8 - 1), 0)),
            pl.BlockSpec((1, D_RW_IN), c2),
            pl.BlockSpec((1, D_RW_IN), c2),
            pl.BlockSpec((1, 2 * D_RWKV), c2),
            pl.BlockSpec((LANES, 2 * D_RWKV), c2),
            pl.BlockSpec((1, 2 * D_RWKV), c2),
            pl.BlockSpec((LANES, 2 * D_RWKV), c2),
            pl.BlockSpec((LANES, D_RWKV), c2),
            pl.BlockSpec((1, D_RWKV), c2),
            pl.BlockSpec((1, D_RWKV), c2),
            pl.BlockSpec((1, D_RWKV), c2),
            pl.BlockSpec((D_RWKV, D_RWKV), c2),
        ],
        out_specs=[tok, tok, tok, tok2, tok2, tok2, tok, tok],
        out_shape=[s1, s1, s1, s2, s2, s2f, s1, s1],
        compiler_params=_cparams(("parallel", "parallel")),
        name="rwkv_prep",
    )(proj_rw, proj_rw, proj_rw, mup, mun, w0, wup_bd, a0, aup_bd, gup, k_k, k_a, r_k, ones_bd)


def _stack2(y):
    yb = y.astype(BF16)
    lo = _cols(yb.shape) < RWKV_HEAD
    zero = jnp.zeros_like(yb)
    return jnp.concatenate([jnp.where(lo, yb, zero), jnp.where(lo, zero, yb)], axis=0)


def _pair_mm(x, y):
    return _dot(x.astype(BF16), _stack2(y))


class _Masks:
    def __init__(self, reverse):
        shp = (CHUNK, PAIR)
        t = _rows(shp)
        s = _cols(shp) % CHUNK
        self.strict = (s > t) if reverse else (s < t)
        self.incl = (s >= t) if reverse else (s <= t)
        self.eye = s == t
        self.blk16 = (t // 16) == (s // 16)
        self.lvl32 = ((t // 32) == (s // 32)) & ((t // 16) != (s // 16))
        self.lvl64 = (t // 32) != (s // 32)
        sq = (PAIR, PAIR)
        self.bd = (_rows(sq) // RWKV_HEAD) == (_cols(sq) // RWKV_HEAD)
        self.eye_sq = _rows(sq) == _cols(sq)
        tt = _rows((CHUNK, CHUNK))
        ss = _cols((CHUNK, CHUNK))
        self.tri = jnp.where((ss >= tt) if reverse else (ss <= tt), 1.0, 0.0).astype(BF16)
        self.reverse = reverse


def _dot_hilo_l(w_bf16, x):
    hi, lo = _split(x)
    return _dot(w_bf16, hi) + _dot(w_bf16, lo)


def _col(x, p):
    return x[:, p * PAIR:(p + 1) * PAIR]


def _wkv_kernel(rf_ref, kkf_ref, vf_ref, kdf_ref, kkaf_ref, lwf_ref,
                rb_ref, kkb_ref, vb_ref, kdb_ref, kkab_ref, lwb_ref,
                yf_ref, yb_ref, h_ref, *, n_sub):
    @pl.when(pl.program_id(1) == 0)
    def _():
        h_ref[...] = jnp.zeros_like(h_ref)

    dirs = (
        (False, rf_ref, kkf_ref, vf_ref, kdf_ref, kkaf_ref, lwf_ref, yf_ref),
        (True, rb_ref, kkb_ref, vb_ref, kdb_ref, kkab_ref, lwb_ref, yb_ref),
    )
    masks = (_Masks(False), _Masks(True))
    chunks = [(di, j) for di in range(2) for j in range(n_sub)]
    units = [(di, j, p) for (di, j) in chunks for p in range(N_PAIRS)]

    rb, ab, bt, kt, bh, kh, vv, etot = {}, {}, {}, {}, {}, {}, {}, {}
    for c in chunks:
        di, j = c
        reverse, r_ref, kk_ref, v_ref, kd_ref, kka_ref, lw_ref, _ = dirs[di]
        rs = slice(j * CHUNK, (j + 1) * CHUNK)
        lw = lw_ref[0, 0, rs, :]
        kd = kd_ref[0, 0, rs, :].astype(F32)
        kka = kka_ref[0, 0, rs, :].astype(F32)
        cum = _dot_hilo_l(masks[di].tri, lw)
        tot = cum[0:1] if reverse else cum[CHUNK - 1: CHUNK]
        rb[c] = r_ref[0, rs, :].astype(F32) * jnp.exp(cum)
        ab[c] = -kk_ref[0, rs, :].astype(F32) * jnp.exp(cum - lw)
        ip = jnp.exp(-cum)
        bt[c] = kka * ip
        kt[c] = kd * ip
        ph = jnp.exp(tot - cum)
        bh[c] = kka * ph
        kh[c] = kd * ph
        vv[c] = v_ref[0, rs, :]
        etot[c] = jnp.exp(tot)

    def per_unit(fn):
        return {u: fn(u, (u[0], u[1]), u[2], masks[u[0]]) for u in units}

    s_all = per_unit(lambda u, c, p, m: _dot_nt(
        jnp.concatenate([_col(ab[c], p), _col(rb[c], p)], axis=0).astype(BF16),
        jnp.concatenate([_stack2(_col(bt[c], p)), _stack2(_col(kt[c], p))], axis=0)))
    n_ab = per_unit(lambda u, c, p, m: jnp.where(m.strict, s_all[u][:CHUNK, :PAIR], 0.0))
    a_ak = per_unit(lambda u, c, p, m: jnp.where(m.strict, s_all[u][:CHUNK, PAIR:], 0.0).astype(BF16))
    m_rb = per_unit(lambda u, c, p, m: jnp.where(m.incl, s_all[u][CHUNK:, :PAIR], 0.0).astype(BF16))
    m_rk = per_unit(lambda u, c, p, m: jnp.where(m.incl, s_all[u][CHUNK:, PAIR:], 0.0).astype(BF16))

    nd = per_unit(lambda u, c, p, m: jnp.where(m.blk16, n_ab[u], 0.0))
    t_inv = per_unit(lambda u, c, p, m: jnp.where(m.eye, 1.0, nd[u]))
    pw = per_unit(lambda u, c, p, m: _pair_mm(nd[u], nd[u]))
    for step in range(3):
        t_inv = per_unit(lambda u, c, p, m: t_inv[u] + _pair_mm(t_inv[u], pw[u]))
        if step < 2:
            pw = per_unit(lambda u, c, p, m: _pair_mm(pw[u], pw[u]))
    for lvl in ("lvl32", "lvl64"):
        tc = per_unit(lambda u, c, p, m: _pair_mm(t_inv[u], jnp.where(getattr(m, lvl), n_ab[u], 0.0)))
        t_inv = per_unit(lambda u, c, p, m: t_inv[u] + _pair_mm(tc[u], t_inv[u]))

    v2 = per_unit(lambda u, c, p, m: _stack2(_col(vv[c], p)))
    akv = per_unit(lambda u, c, p, m: _dot(a_ak[u], v2[u]))
    wu = per_unit(lambda u, c, p, m: _dot(
        t_inv[u].astype(BF16), jnp.concatenate([_stack2(_col(ab[c], p)), _stack2(akv[u])], axis=1)))
    qy = per_unit(lambda u, c, p, m: _dot(
        m_rb[u], jnp.concatenate([_stack2(wu[u][:, :PAIR]), _stack2(wu[u][:, PAIR:])], axis=1)))
    q_hat = per_unit(lambda u, c, p, m: (_col(rb[c], p) + qy[u][:, :PAIR]).astype(BF16))
    y_loc = per_unit(lambda u, c, p, m: qy[u][:, PAIR:] + _dot(m_rk[u], v2[u]))
    gd = per_unit(lambda u, c, p, m: _dot(_col(bh[c], p).T.astype(BF16), wu[u].astype(BF16)))
    kv = per_unit(lambda u, c, p, m: _dot(_col(kh[c], p).T.astype(BF16), _col(vv[c], p).astype(BF16)))
    g_m = per_unit(lambda u, c, p, m: (jnp.where(m.bd, gd[u][:, :PAIR], 0.0)
                                       + jnp.where(m.eye_sq, _col(etot[c], p), 0.0)).astype(BF16))
    d_m = per_unit(lambda u, c, p, m: jnp.where(m.bd, gd[u][:, PAIR:] + kv[u], 0.0))

    h = {(di, p): h_ref[di, p] for di in range(2) for p in range(N_PAIRS)}
    for step in range(n_sub):
        for di in range(2):
            j = n_sub - 1 - step if dirs[di][0] else step
            y_ref = dirs[di][7]
            rs = slice(j * CHUNK, (j + 1) * CHUNK)
            for p in range(N_PAIRS):
                u = (di, j, p)
                hb = h[(di, p)].astype(BF16)
                y_ref[0, rs, p * PAIR:(p + 1) * PAIR] = (_dot(q_hat[u], hb) + y_loc[u]).astype(BF16)
                h[(di, p)] = _dot(g_m[u], hb) + d_m[u]
    for di in range(2):
        for p in range(N_PAIRS):
            h_ref[di, p] = h[(di, p)]


def _wkv_call(r, kkn, v, kd, kka, lw, tm):
    bn, t, _ = r.shape
    nt = t // tm
    fwd = pl.BlockSpec((1, tm, D_RWKV), lambda b, i: (b, i, 0))
    bwd = pl.BlockSpec((1, tm, D_RWKV), lambda b, i: (b, nt - 1 - i, 0))
    fwd2 = pl.BlockSpec((1, 1, tm, D_RWKV), lambda b, i: (0, b, i, 0))
    bwd2 = pl.BlockSpec((1, 1, tm, D_RWKV), lambda b, i: (1, b, nt - 1 - i, 0))
    s1 = jax.ShapeDtypeStruct((bn, t, D_RWKV), BF16)
    return pl.pallas_call(
        functools.partial(_wkv_kernel, n_sub=tm // CHUNK),
        grid=(bn, nt),
        in_specs=[fwd, fwd, fwd, fwd2, fwd2, fwd2, bwd, bwd, bwd, bwd2, bwd2, bwd2],
        out_specs=[fwd, bwd],
        out_shape=[s1, s1],
        scratch_shapes=[pltpu.VMEM((2, N_PAIRS, PAIR, PAIR), F32)],
        compiler_params=_cparams(("parallel", "arbitrary")),
        name="wkv7_chunked",
    )(r, kkn, v, kd, kka, lw, r, kkn, v, kd, kka, lw)


def _gelu_tanh(x):
    return 0.5 * x * (1.0 + jnp.tanh(0.7978845608028654 * (x + 0.044715 * (x * x * x))))


def _route(logits_t, b_col):
    rows = [logits_t[e:e + 1] for e in range(N_EXPERTS)]
    mx = functools.reduce(jnp.maximum, rows)
    ex = [jnp.exp(x - mx) for x in rows]
    den = functools.reduce(lambda a, b: a + b, ex)
    probs = [e / den for e in ex]
    sel = [probs[e] + b_col[e:e + 1] for e in range(N_EXPERTS)]
    scores = []
    for g in range(N_GROUPS):
        a, b, c, d = sel[4 * g: 4 * g + 4]
        hi1, lo1 = jnp.maximum(a, b), jnp.minimum(a, b)
        hi2, lo2 = jnp.maximum(c, d), jnp.minimum(c, d)
        scores.append(jnp.maximum(hi1, hi2) + jnp.maximum(jnp.minimum(hi1, hi2), jnp.maximum(lo1, lo2)))
    best = scores[0]
    bg = jnp.zeros_like(best)
    for g in range(1, N_GROUPS):
        upd = scores[g] > best
        best = jnp.where(upd, scores[g], best)
        bg = jnp.where(upd, float(g), bg)

    def pick(vals):
        out = []
        for j in range(EXP_PER_GROUP):
            x = vals[j]
            for g in range(1, N_GROUPS):
                x = jnp.where(bg == float(g), vals[4 * g + j], x)
            out.append(x)
        return out

    sg = pick(sel)
    pg = pick(probs)
    v1, i1 = sg[0], jnp.zeros_like(best)
    for j in range(1, EXP_PER_GROUP):
        upd = sg[j] > v1
        v1 = jnp.where(upd, sg[j], v1)
        i1 = jnp.where(upd, float(j), i1)
    neg = jnp.full_like(best, -jnp.inf)
    v2, i2 = neg, jnp.zeros_like(best)
    for j in range(EXP_PER_GROUP):
        cand = jnp.where(i1 == float(j), neg, sg[j])
        upd = cand > v2
        v2 = jnp.where(upd, cand, v2)
        i2 = jnp.where(upd, float(j), i2)
    p1 = jnp.zeros_like(best)
    p2 = jnp.zeros_like(best)
    for j in range(EXP_PER_GROUP):
        p1 = jnp.where(i1 == float(j), pg[j], p1)
        p2 = jnp.where(i2 == float(j), pg[j], p2)
    tot = p1 + p2
    return bg * float(EXP_PER_GROUP) + i1, bg * float(EXP_PER_GROUP) + i2, p1 / tot, p2 / tot


def _outproj_kernel(hl_ref, gate_ref, yf_ref, yb_ref, bonus_ref, g_ref, x_ref, lnw_ref, lnb_ref, g1_ref,
                    sh2_ref, sc2_ref, n2_ref, wout_ref, wr_ref, br_ref, ones_ref,
                    xn_ref, h2_ref, route_ref):
    lru_out = (hl_ref[0, 0] + hl_ref[1, 0]) * _gelu_tanh(gate_ref[0])
    ones_bd = ones_ref[...]
    y = yf_ref[0].astype(F32) + yb_ref[0].astype(F32)
    mu = _dot_hilo(y, ones_bd) * (1.0 / RWKV_HEAD)
    yc = y - mu
    var = _dot_hilo(yc * yc, ones_bd) * (1.0 / RWKV_HEAD)
    gn = yc * lax.rsqrt(var + GN_EPS) * lnw_ref[...] + lnb_ref[...]
    rw_out = (gn + bonus_ref[0].astype(F32)) * g_ref[0].astype(F32)
    o = _dot(lru_out.astype(BF16), wout_ref[:D_LRU]) + _dot(rw_out.astype(BF16), wout_ref[D_LRU:])
    xn = x_ref[0] + g1_ref[0] * o
    xn_ref[0] = xn
    h2 = _norm_mod(xn, n2_ref[...], sc2_ref[0], sh2_ref[0])
    h2_ref[0] = h2
    logits_t = _dot3_nt(wr_ref[...], h2)
    e0, e1, g0, g1 = _route(logits_t, br_ref[...])
    zero = jnp.zeros_like(e0)
    route_ref[0, 0] = jnp.concatenate([e0, e1, g0, g1, zero, zero, zero, zero], axis=0)


def _outproj_call(hl, proj_lru, yf, yb, bonus, g, x, lnw, lnb, g1, sh2, sc2, n2, w_out_bf16, wr_t, br, ones_bd, tm):
    bn, t, d = x.shape
    nt = t // tm
    c2 = lambda b, i: (0, 0)
    tok = pl.BlockSpec((1, tm, D_RWKV), lambda b, i: (b, i, 0))
    tokd = pl.BlockSpec((1, tm, d), lambda b, i: (b, i, 0))
    per_b = pl.BlockSpec((1, 1, d), lambda b, i: (b, 0, 0))
    return pl.pallas_call(
        _outproj_kernel,
        grid=(bn, nt),
        in_specs=[
            pl.BlockSpec((2, 1, tm, D_LRU), lambda b, i: (0, b, i, 0)),
            pl.BlockSpec((1, tm, D_LRU), lambda b, i: (b, i, 1)),
            tok, tok, tok, tok, tokd,
            pl.BlockSpec((1, D_RWKV), c2),
            pl.BlockSpec((1, D_RWKV), c2),
            per_b, per_b, per_b,
            pl.BlockSpec((1, d), c2),
            pl.BlockSpec((d, d), c2),
            pl.BlockSpec((N_EXPERTS, d), c2),
            pl.BlockSpec((N_EXPERTS, 1), c2),
            pl.BlockSpec((D_RWKV, D_RWKV), c2),
        ],
        out_specs=[tokd, tokd, pl.BlockSpec((1, 1, SUBLANES, tm), lambda b, i: (b, i, 0, 0))],
        out_shape=[
            jax.ShapeDtypeStruct((bn, t, d), F32),
            jax.ShapeDtypeStruct((bn, t, d), F32),
            jax.ShapeDtypeStruct((bn, nt, SUBLANES, tm), F32),
        ],
        compiler_params=_cparams(("parallel", "parallel")),
        name="outproj_router",
    )(hl, proj_lru, yf, yb, bonus, g, x, lnw, lnb, g1, sh2, sc2, n2, w_out_bf16, wr_t, br, ones_bd)


def _row_copy(src_ref, src_row, dst_ref, dst_row, sem):
    return pltpu.make_async_copy(src_ref.at[pl.ds(src_row, 1)], dst_ref.at[pl.ds(dst_row, 1)], sem)


def _dispatch_kernel(pos_ref, h_ref, xb_in_ref, xb_ref, sem):
    del xb_in_ref
    tm = h_ref.shape[0]

    def issue(r, carry):
        _row_copy(h_ref, r, xb_ref, pos_ref[0, 0, r], sem).start()
        _row_copy(h_ref, r, xb_ref, pos_ref[0, 0, tm + r], sem).start()
        return carry

    lax.fori_loop(0, tm, issue, 0)

    def drain(r, carry):
        _row_copy(h_ref, 0, xb_ref, 0, sem).wait()
        return carry

    lax.fori_loop(0, 2 * tm, drain, 0)


def _dispatch_call(h2, pos_tiles, xb_init, tm):
    n, d = h2.shape
    nt = n // tm
    n_rows = xb_init.shape[0]
    return pl.pallas_call(
        _dispatch_kernel,
        grid=(nt,),
        in_specs=[
            pl.BlockSpec((1, 1, 2 * tm), lambda i: (i, 0, 0), memory_space=pltpu.SMEM),
            pl.BlockSpec((tm, d), lambda i: (i, 0)),
            pl.BlockSpec(memory_space=pl.ANY),
        ],
        out_specs=pl.BlockSpec(memory_space=pl.ANY),
        out_shape=jax.ShapeDtypeStruct((n_rows, d), F32),
        scratch_shapes=[pltpu.SemaphoreType.DMA(())],
        input_output_aliases={2: 0},
        compiler_params=_cparams(("arbitrary",)),
        name="moe_dispatch",
    )(pos_tiles, h2, xb_init)


def _expert_kernel(blk_e_ref, n_used_ref, x_ref, wg_ref, wu_ref, wd_ref, o_ref):
    i = pl.program_id(0)

    @pl.when(i < n_used_ref[0])
    def _():
        x = x_ref[...].astype(BF16)
        gate = _dot(x, wg_ref[0])
        hid = gate * _sigmoid(gate) * _dot(x, wu_ref[0])
        o_ref[...] = _dot(hid.astype(BF16), wd_ref[0])

    @pl.when(i >= n_used_ref[0])
    def _():
        o_ref[...] = jnp.zeros_like(o_ref)


def _expert_call(xb, blk_e, n_used, wg, wu, wd):
    n_rows, d = xb.shape
    n_blk = n_rows // MOE_BLOCK
    grid_spec = pltpu.PrefetchScalarGridSpec(
        num_scalar_prefetch=2,
        grid=(n_blk,),
        in_specs=[
            pl.BlockSpec((MOE_BLOCK, d), lambda i, be, nu: (i, 0)),
            pl.BlockSpec((1, d, D_EXPERT), lambda i, be, nu: (be[i], 0, 0)),
            pl.BlockSpec((1, d, D_EXPERT), lambda i, be, nu: (be[i], 0, 0)),
            pl.BlockSpec((1, D_EXPERT, d), lambda i, be, nu: (be[i], 0, 0)),
        ],
        out_specs=pl.BlockSpec((MOE_BLOCK, d), lambda i, be, nu: (i, 0)),
    )
    return pl.pallas_call(
        _expert_kernel,
        grid_spec=grid_spec,
        out_shape=jax.ShapeDtypeStruct((n_rows, d), F32),
        compiler_params=_cparams(("arbitrary",)),
        name="moe_experts",
    )(blk_e, n_used, xb, wg, wu, wd)


def _combine_kernel(pos_ref, gates_ref, x_ref, g2_ref, yb_ref, o_ref, buf_ref, sem):
    tm = x_ref.shape[0]

    def issue(r, carry):
        _row_copy(yb_ref, pos_ref[0, 0, r], buf_ref.at[0], r, sem).start()
        _row_copy(yb_ref, pos_ref[0, 0, tm + r], buf_ref.at[1], r, sem).start()
        return carry

    lax.fori_loop(0, tm, issue, 0)

    def drain(r, carry):
        _row_copy(yb_ref, 0, buf_ref.at[0], 0, sem).wait()
        return carry

    lax.fori_loop(0, 2 * tm, drain, 0)
    gts = gates_ref[...]
    y = gts[:, 0:1] * buf_ref[0] + gts[:, 1:2] * buf_ref[1]
    o_ref[...] = x_ref[...] + g2_ref[0] * y


def _combine_call(xn, yb, pos_tiles, gates, g2, t, tm):
    n, d = xn.shape
    nt = n // tm
    per_b = t // tm
    return pl.pallas_call(
        _combine_kernel,
        grid=(nt,),
        in_specs=[
            pl.BlockSpec((1, 1, 2 * tm), lambda i: (i, 0, 0), memory_space=pltpu.SMEM),
            pl.BlockSpec((tm, 2), lambda i: (i, 0)),
            pl.BlockSpec((tm, d), lambda i: (i, 0)),
            pl.BlockSpec((1, 1, d), lambda i: (i // per_b, 0, 0)),
            pl.BlockSpec(memory_space=pl.ANY),
        ],
        out_specs=pl.BlockSpec((tm, d), lambda i: (i, 0)),
        out_shape=jax.ShapeDtypeStruct((n, d), F32),
        scratch_shapes=[pltpu.VMEM((2, tm, d), F32), pltpu.SemaphoreType.DMA(())],
        compiler_params=_cparams(("arbitrary",)),
        name="moe_combine",
    )(pos_tiles, gates, xn, g2, yb)


def _final_kernel(x_ref, nw_ref, o_ref):
    x = x_ref[...]
    ms = jnp.mean(x * x, axis=-1, keepdims=True)
    o_ref[...] = x * lax.rsqrt(ms + NORM_EPS) * nw_ref[...]


def _final_call(x, nw, tm):
    n, d = x.shape
    return pl.pallas_call(
        _final_kernel,
        grid=(n // tm,),
        in_specs=[pl.BlockSpec((tm, d), lambda i: (i, 0)), pl.BlockSpec((1, d), lambda i: (0, 0))],
        out_specs=pl.BlockSpec((tm, d), lambda i: (i, 0)),
        out_shape=jax.ShapeDtypeStruct((n, d), F32),
        compiler_params=_cparams(("parallel",)),
        name="final_norm",
    )(x, nw)


def _block_diag(w):
    h, a, b = w.shape
    eye = jnp.eye(h, dtype=w.dtype)
    return jnp.einsum("hab,hg->hagb", w, eye).reshape(h * a, h * b)


def _head_ones():
    idx = np.arange(D_RWKV) // RWKV_HEAD
    return jnp.asarray((idx[:, None] == idx[None, :]).astype(np.float32), dtype=BF16)


def _routing_tables(route, n_tok, tm):
    bn, nt = route.shape[0], route.shape[1]
    flat = jnp.transpose(route, (2, 0, 1, 3)).reshape(SUBLANES, n_tok)
    e0 = flat[0].astype(jnp.int32)
    e1 = flat[1].astype(jnp.int32)
    gates = jnp.stack([flat[2], flat[3]], axis=1)
    ar = jnp.arange(N_EXPERTS, dtype=jnp.int32)
    oh = (e0[:, None] == ar).astype(jnp.int32) + (e1[:, None] == ar).astype(jnp.int32)
    cs = jnp.cumsum(oh, axis=0)
    counts = cs[-1]
    excl = cs - oh
    padded = (counts + MOE_BLOCK - 1) // MOE_BLOCK * MOE_BLOCK
    pad_end = jnp.cumsum(padded)
    pad_start = pad_end - padded
    base = excl + pad_start[None, :]
    pos0 = jnp.take_along_axis(base, e0[:, None], axis=1)[:, 0]
    pos1 = jnp.take_along_axis(base, e1[:, None], axis=1)[:, 0]
    n_asg = 2 * n_tok
    n_blk = (n_asg + N_EXPERTS * (MOE_BLOCK - 1) + MOE_BLOCK - 1) // MOE_BLOCK
    blk_start = jnp.arange(n_blk, dtype=jnp.int32) * MOE_BLOCK
    blk_e = jnp.minimum(jnp.sum(pad_end[None, :] <= blk_start[:, None], axis=-1), N_EXPERTS - 1).astype(jnp.int32)
    n_used = (pad_end[-1] // MOE_BLOCK).astype(jnp.int32).reshape(1)
    pos_tiles = jnp.concatenate(
        [pos0.reshape(n_tok // tm, 1, tm), pos1.reshape(n_tok // tm, 1, tm)], axis=2).astype(jnp.int32)
    return pos_tiles, gates, blk_e, n_used, n_blk * MOE_BLOCK


def _tile(t, want):
    return min(t, want)


def _trunk(x, c, w_mod, b_mod, norm1, norm2, w_in, w_out, conv_w, conv_b, lru_wa, lru_ba, lru_wx, lru_bx, lru_lam,
           mu_prev, mu_next, rw_w0, rw_wup, rw_a0, rw_aup, rw_gup, rw_kk, rw_ka, rw_rk, ln_x_w, ln_x_b,
           w_router, b_router, exp_gate, exp_up, exp_down, norm_f):
    bn, t, d = x.shape
    n_tok = bn * t
    depth = w_mod.shape[0]
    ones_bd = _head_ones()
    mod = _mod_call(c, w_mod, b_mod)
    wr_t = jnp.transpose(w_router)
    br = b_router.reshape(N_EXPERTS, 1)
    tm_moe = _tile(t, 256)
    xb = None

    for l in range(depth):
        sh1, sc1, g1, sh2, sc2, g2 = [m.reshape(bn, 1, d) for m in jnp.split(mod[l], 6, axis=-1)]
        proj_lru, proj_rw = _inproj_call(x, sh1, sc1, norm1[l].reshape(1, d), w_in[l].astype(BF16), _tile(t, 512))

        wg = jnp.stack([jnp.concatenate([_block_diag(lru_wa[l, dd]), _block_diag(lru_wx[l, dd])], axis=1)
                        for dd in range(2)]).astype(BF16)
        bg = jnp.concatenate([lru_ba[l], lru_bx[l]], axis=1).reshape(2, 1, 2 * D_LRU)
        sp = jax.nn.softplus(-lru_lam[l]).reshape(2, 1, D_LRU)
        hl = _lru_call(proj_lru, conv_w[l], conv_b[l].reshape(1, D_LRU), wg, bg, sp, _tile(t, 512))

        zeros = jnp.zeros((64, D_RWKV), F32)
        wup_bd = jnp.concatenate([jnp.concatenate([rw_wup[l, 0], zeros], axis=1),
                                  jnp.concatenate([zeros, rw_wup[l, 1]], axis=1)], axis=0).astype(BF16)
        aup_bd = jnp.concatenate([jnp.concatenate([rw_aup[l, 0], zeros], axis=1),
                                  jnp.concatenate([zeros, rw_aup[l, 1]], axis=1)], axis=0).astype(BF16)
        r, kkn, v, kd, kka, lw, bonus, g = _rwprep_call(
            proj_rw, mu_prev[l].reshape(1, -1), mu_next[l].reshape(1, -1),
            rw_w0[l].reshape(1, -1), wup_bd, rw_a0[l].reshape(1, -1), aup_bd, rw_gup[l].astype(BF16),
            rw_kk[l].reshape(1, -1), rw_ka[l].reshape(1, -1), rw_rk[l].reshape(1, -1), ones_bd, _tile(t, 256))
        yf, yb = _wkv_call(r, kkn, v, kd, kka, lw, _tile(t, 128))

        xn, h2, route = _outproj_call(
            hl, proj_lru, yf, yb, bonus, g, x, ln_x_w[l].reshape(1, -1), ln_x_b[l].reshape(1, -1),
            g1, sh2, sc2, norm2[l].reshape(1, d), w_out[l].astype(BF16), wr_t, br, ones_bd, tm_moe)

        pos_tiles, gates, blk_e, n_used, n_rows = _routing_tables(route, n_tok, tm_moe)
        xb_init = jnp.zeros((n_rows, d), F32) if xb is None else xb
        xb = _dispatch_call(h2.reshape(n_tok, d), pos_tiles, xb_init, tm_moe)
        ybuf = _expert_call(xb, blk_e, n_used, exp_gate[l].astype(BF16), exp_up[l].astype(BF16),
                            exp_down[l].astype(BF16))
        x = _combine_call(xn.reshape(n_tok, d), ybuf, pos_tiles, gates, g2, t, tm_moe).reshape(bn, t, d)

    return _final_call(x.reshape(n_tok, d), norm_f.reshape(1, d), _tile(n_tok, 512)).reshape(bn, t, d)


def kernel(x_prompt, x_sample, c_prompt, c_sample, w_mod, b_mod, norm1, norm2, w_in, w_out, conv_w, conv_b, lru_wa, lru_ba, lru_wx, lru_bx, lru_lam, mu_prev, mu_next, rw_w0, rw_wup, rw_a0, rw_aup, rw_gup, rw_kk, rw_ka, rw_rk, ln_x_w, ln_x_b, w_router, b_router, exp_gate, exp_up, exp_down, norm_f):
    bp = x_prompt.shape[0]
    x = jnp.concatenate([x_prompt, x_sample], axis=0)
    c = jnp.concatenate([c_prompt, c_sample], axis=0).astype(F32)
    y = _trunk(x, c, w_mod, b_mod, norm1, norm2, w_in, w_out, conv_w, conv_b, lru_wa, lru_ba, lru_wx, lru_bx,
               lru_lam, mu_prev, mu_next, rw_w0, rw_wup, rw_a0, rw_aup, rw_gup, rw_kk, rw_ka, rw_rk, ln_x_w,
               ln_x_b, w_router, b_router, exp_gate, exp_up, exp_down, norm_f)
    return (y[:bp], y[bp:])
```

```python
import functools

import jax
import jax.numpy as jnp
import numpy as np
from jax import lax
from jax.experimental import pallas as pl
from jax.experimental.pallas import tpu as pltpu

F32 = jnp.float32
BF16 = jnp.bfloat16

D_MODEL = 1024
D_LRU = 512
D_RWKV = 512
LRU_HEADS = 8
LRU_C = 8.0
RWKV_HEAD = 64
N_EXPERTS = 16
N_GROUPS = 4
EXP_PER_GROUP = 4
D_EXPERT = 512
MOE_BLOCK = 512
NORM_EPS = 1e-6
GN_EPS = 64e-5
D_RW_IN = 1920
D_IN = 2944

LANES = 128
SUBLANES = 8
CHUNK = 64
PAIR = 2 * RWKV_HEAD
N_PAIRS = D_RWKV // PAIR
TOK_ROWS = D_MODEL // LANES


def _cparams(sem, vmem_mb=48):
    return pltpu.CompilerParams(dimension_semantics=sem, vmem_limit_bytes=vmem_mb * 1024 * 1024)


def _dot(a, b):
    return jnp.dot(a, b, preferred_element_type=F32)


def _dot_nt(a, b):
    return lax.dot_general(a, b, (((1,), (1,)), ((), ())), preferred_element_type=F32)


def _split(x):
    hi = x.astype(BF16)
    lo = (x - hi.astype(F32)).astype(BF16)
    return hi, lo


def _dot_hilo(x, w_bf16):
    hi, lo = _split(x)
    return _dot(hi, w_bf16) + _dot(lo, w_bf16)


def _dot3(a, b):
    ah, al = _split(a)
    bh, bl = _split(b)
    return _dot(ah, bh) + (_dot(ah, bl) + _dot(al, bh))


def _dot3_nt(a, b):
    ah, al = _split(a)
    bh, bl = _split(b)
    return _dot_nt(ah, bh) + (_dot_nt(ah, bl) + _dot_nt(al, bh))


def _sigmoid(x):
    return 1.0 / (1.0 + jnp.exp(-x))


def _softplus(x):
    return jnp.maximum(x, 0.0) + jnp.log(1.0 + jnp.exp(-jnp.abs(x)))


def _rows(shape):
    return lax.broadcasted_iota(jnp.int32, shape, 0)


def _cols(shape):
    return lax.broadcasted_iota(jnp.int32, shape, 1)


def _mod_kernel(c_ref, w_ref, b_ref, o_ref):
    c = c_ref[...]
    cs = c * _sigmoid(c)
    o_ref[0] = _dot3(cs, w_ref[0]) + b_ref[0]


def _mod_call(c, w_mod, b_mod):
    nl, d, n6 = w_mod.shape
    bn = c.shape[0]
    tn = 1536
    return pl.pallas_call(
        _mod_kernel,
        grid=(nl, n6 // tn),
        in_specs=[
            pl.BlockSpec((bn, d), lambda l, j: (0, 0)),
            pl.BlockSpec((1, d, tn), lambda l, j: (l, 0, j)),
            pl.BlockSpec((1, 1, tn), lambda l, j: (l, 0, j)),
        ],
        out_specs=pl.BlockSpec((1, bn, tn), lambda l, j: (l, 0, j)),
        out_shape=jax.ShapeDtypeStruct((nl, bn, n6), F32),
        compiler_params=_cparams(("arbitrary", "arbitrary")),
        name="adaln_mod",
    )(c, w_mod, b_mod.reshape(nl, 1, n6))


def _norm_mod(x, nw, sc, sh):
    ms = jnp.mean(x * x, axis=-1, keepdims=True)
    return (x * lax.rsqrt(ms + NORM_EPS) * nw) * (1.0 + sc) + sh


def _inproj_kernel(x_ref, sh_ref, sc_ref, nw_ref, w_ref, lru_ref, rw_ref):
    h = _norm_mod(x_ref[0], nw_ref[...], sc_ref[0], sh_ref[0]).astype(BF16)
    lru_ref[0] = _dot(h, w_ref[:, : 2 * D_LRU])
    rw_ref[0] = _dot(h, w_ref[:, 2 * D_LRU:])


def _inproj_call(x, sh, sc, nw, w_in_bf16, tm):
    bn, t, d = x.shape
    return pl.pallas_call(
        _inproj_kernel,
        grid=(bn, t // tm),
        in_specs=[
            pl.BlockSpec((1, tm, d), lambda b, i: (b, i, 0)),
            pl.BlockSpec((1, 1, d), lambda b, i: (b, 0, 0)),
            pl.BlockSpec((1, 1, d), lambda b, i: (b, 0, 0)),
            pl.BlockSpec((1, d), lambda b, i: (0, 0)),
            pl.BlockSpec((d, D_IN), lambda b, i: (0, 0)),
        ],
        out_specs=[
            pl.BlockSpec((1, tm, 2 * D_LRU), lambda b, i: (b, i, 0)),
            pl.BlockSpec((1, tm, D_RW_IN), lambda b, i: (b, i, 0)),
        ],
        out_shape=[
            jax.ShapeDtypeStruct((bn, t, 2 * D_LRU), F32),
            jax.ShapeDtypeStruct((bn, t, D_RW_IN), F32),
        ],
        compiler_params=_cparams(("parallel", "parallel")),
        name="norm_inproj",
    )(x, sh, sc, nw, w_in_bf16)


def _lru_kernel(x_ref, prev_ref, next_ref, cw_ref, cb_ref, wg_ref, bg_ref, sp_ref, o_ref,
                carry_ref, a_scr, b_scr, *, n_tiles):
    d = pl.program_id(1)
    i = pl.program_id(2)
    ti = d * (n_tiles - 1) + (1 - 2 * d) * i
    tm = x_ref.shape[1]

    @pl.when(i == 0)
    def _():
        carry_ref[...] = jnp.zeros_like(carry_ref)

    x = x_ref[0]
    prev8 = jnp.where(ti == 0, 0.0, prev_ref[0])
    next8 = jnp.where(ti == n_tiles - 1, 0.0, next_ref[0])
    r = _rows(x.shape)
    xm1 = jnp.where(r == 0, prev8[7:8], pltpu.roll(x, 1, 0))
    xm2 = jnp.where(r == 0, prev8[6:7], jnp.where(r == 1, prev8[7:8], pltpu.roll(x, 2, 0)))
    xp1 = jnp.where(r == tm - 1, next8[0:1], pltpu.roll(x, tm - 1, 0))
    cw = cw_ref[...]
    u = cw[0:1] * xm2 + cw[1:2] * xm1 + cw[2:3] * x + cw[3:4] * xp1 + cb_ref[...]

    gates = _dot(u.astype(BF16), wg_ref[0]) + bg_ref[0]
    rg = _sigmoid(gates[:, :D_LRU])
    ig = _sigmoid(gates[:, D_LRU:])
    log_a = (-LRU_C) * rg * sp_ref[0]
    a = jnp.exp(log_a)
    bv = jnp.sqrt(1.0 - jnp.exp(2.0 * log_a)) * (ig * u)

    n_lt = D_LRU // LANES
    for c in range(n_lt):
        a_scr[c] = a[:, c * LANES:(c + 1) * LANES]
        b_scr[c] = bv[:, c * LANES:(c + 1) * LANES]
    win = SUBLANES * SUBLANES

    def scan(reverse):
        r8 = _rows((SUBLANES, D_LRU))
        carry = carry_ref[...]
        n_win = tm // win
        for w in (range(n_win - 1, -1, -1) if reverse else range(n_win)):
            base = w * win
            hs = [jnp.zeros((SUBLANES, LANES), F32)] * n_lt
            accs = [jnp.ones((SUBLANES, LANES), F32)] * n_lt
            for g in (range(SUBLANES - 1, -1, -1) if reverse else range(SUBLANES)):
                rows = pl.ds(base + g, SUBLANES, stride=SUBLANES)
                for c in range(n_lt):
                    ag = a_scr[c, rows, :]
                    hs[c] = ag * hs[c] + b_scr[c, rows, :]
                    accs[c] = ag * accs[c]
                    b_scr[c, rows, :] = hs[c]
                    a_scr[c, rows, :] = accs[c]
            h = jnp.concatenate(hs, axis=1)
            acc = jnp.concatenate(accs, axis=1)
            s = 1
            while s < SUBLANES:
                keep = (r8 >= SUBLANES - s) if reverse else (r8 < s)
                sh = SUBLANES - s if reverse else s
                h = acc * jnp.where(keep, 0.0, pltpu.roll(h, sh, 0)) + h
                acc = acc * jnp.where(keep, 1.0, pltpu.roll(acc, sh, 0))
                s *= 2
            ends = h + acc * carry
            if reverse:
                enter = jnp.where(r8 == SUBLANES - 1, carry, pltpu.roll(ends, SUBLANES - 1, 0))
                carry = ends[0:1]
            else:
                enter = jnp.where(r8 == 0, carry, pltpu.roll(ends, 1, 0))
                carry = ends[SUBLANES - 1: SUBLANES]
            for r in range(SUBLANES):
                rs = slice(base + r * SUBLANES, base + (r + 1) * SUBLANES)
                for c in range(n_lt):
                    cs = slice(c * LANES, (c + 1) * LANES)
                    o_ref[0, 0, rs, cs] = b_scr[c, rs, :] + a_scr[c, rs, :] * enter[r:r + 1, cs]
        carry_ref[...] = carry

    @pl.when(d == 0)
    def _():
        scan(False)

    @pl.when(d == 1)
    def _():
        scan(True)


def _lru_call(proj_lru, conv_w, conv_b, wg, bg, sp, tm):
    bn, t, _ = proj_lru.shape
    nt = t // tm
    r8 = tm // SUBLANES
    n8 = t // SUBLANES

    def tile(d, i):
        return d * (nt - 1) + (1 - 2 * d) * i

    return pl.pallas_call(
        functools.partial(_lru_kernel, n_tiles=nt),
        grid=(bn, 2, nt),
        in_specs=[
            pl.BlockSpec((1, tm, D_LRU), lambda b, d, i: (b, tile(d, i), 0)),
            pl.BlockSpec((1, SUBLANES, D_LRU), lambda b, d, i: (b, jnp.maximum(tile(d, i) * r8 - 1, 0), 0)),
            pl.BlockSpec((1, SUBLANES, D_LRU), lambda b, d, i: (b, jnp.minimum((tile(d, i) + 1) * r8, n8 - 1), 0)),
            pl.BlockSpec((4, D_LRU), lambda b, d, i: (0, 0)),
            pl.BlockSpec((1, D_LRU), lambda b, d, i: (0, 0)),
            pl.BlockSpec((1, D_LRU, 2 * D_LRU), lambda b, d, i: (d, 0, 0)),
            pl.BlockSpec((1, 1, 2 * D_LRU), lambda b, d, i: (d, 0, 0)),
            pl.BlockSpec((1, 1, D_LRU), lambda b, d, i: (d, 0, 0)),
        ],
        out_specs=pl.BlockSpec((1, 1, tm, D_LRU), lambda b, d, i: (d, b, tile(d, i), 0)),
        out_shape=jax.ShapeDtypeStruct((2, bn, t, D_LRU), F32),
        scratch_shapes=[pltpu.VMEM((1, D_LRU), F32),
                        pltpu.VMEM((D_LRU // LANES, tm, LANES), F32),
                        pltpu.VMEM((D_LRU // LANES, tm, LANES), F32)],
        compiler_params=_cparams(("parallel", "arbitrary", "arbitrary")),
        name="rg_lru",
    )(proj_lru, proj_lru, proj_lru, conv_w, conv_b, wg, bg, sp)


def _rwprep_kernel(z_ref, prev_ref, next_ref, mup_ref, mun_ref, w0_ref, wup_ref, a0_ref, aup_ref, gup_ref,
                   kk_ref, ka_ref, rk_ref, ones_ref,
                   r_out, kkn_out, v_out, kd_out, kka_out, lw_out, bonus_out, g_out, *, n_tiles):
    i = pl.program_id(1)
    tm = z_ref.shape[1]
    zc = z_ref[0]
    prev8 = jnp.where(i == 0, 0.0, prev_ref[0])
    next8 = jnp.where(i == n_tiles - 1, 0.0, next_ref[0])
    rr = _rows(zc.shape)
    zp = jnp.where(rr == 0, prev8[7:8], pltpu.roll(zc, 1, 0))
    zn = jnp.where(rr == tm - 1, next8[0:1], pltpu.roll(zc, tm - 1, 0))
    z = zc + mup_ref[...] * (zp - zc) + mun_ref[...] * (zn - zc)

    r = z[:, 0:512]
    k = z[:, 512:1024]
    v = z[:, 1024:1536]
    wd = z[:, 1536:1664]
    ad = z[:, 1664:1792]
    gd = z[:, 1792:1920]
    ones_bd = ones_ref[...]

    kkr = k * kk_ref[...]
    ss = _dot_hilo(kkr * kkr, ones_bd)
    kkn = kkr / jnp.maximum(jnp.sqrt(ss), 1e-12)

    wlin = w0_ref[...] + _dot(jnp.tanh(wd).astype(BF16), wup_ref[...])
    lw = -jnp.exp(-_softplus(-wlin) - 0.5)
    a = _sigmoid(a0_ref[...] + _dot(ad.astype(BF16), aup_ref[...]))

    ka = ka_ref[...]
    k_sum = jnp.zeros_like(k)
    for d in range(2):
        a_d = a[:, d * D_RWKV:(d + 1) * D_RWKV]
        kd = k * (1.0 + (a_d - 1.0) * ka)
        kd_out[d, 0] = kd.astype(BF16)
        kka_out[d, 0] = (kkn * a_d).astype(BF16)
        lw_out[d, 0] = lw[:, d * D_RWKV:(d + 1) * D_RWKV]
        k_sum = k_sum + kd
    r_out[0] = r.astype(BF16)
    kkn_out[0] = kkn.astype(BF16)
    v_out[0] = v.astype(BF16)
    bonus_out[0] = (_dot_hilo(r * k_sum * rk_ref[...], ones_bd) * v).astype(BF16)
    g_out[0] = _dot(_sigmoid(gd).astype(BF16), gup_ref[...]).astype(BF16)


def _rwprep_call(proj_rw, mup, mun, w0, wup_bd, a0, aup_bd, gup, k_k, k_a, r_k, ones_bd, tm):
    bn, t, _ = proj_rw.shape
    nt = t // tm
    r8 = tm // SUBLANES
    n8 = t // SUBLANES
    c2 = lambda b, i: (0, 0)
    tok = pl.BlockSpec((1, tm, D_RWKV), lambda b, i: (b, i, 0))
    tok2 = pl.BlockSpec((2, 1, tm, D_RWKV), lambda b, i: (0, b, i, 0))
    s1 = jax.ShapeDtypeStruct((bn, t, D_RWKV), BF16)
    s2 = jax.ShapeDtypeStruct((2, bn, t, D_RWKV), BF16)
    s2f = jax.ShapeDtypeStruct((2, bn, t, D_RWKV), F32)
    return pl.pallas_call(
        functools.partial(_rwprep_kernel, n_tiles=nt),
        grid=(bn, nt),
        in_specs=[
            pl.BlockSpec((1, tm, D_RW_IN), lambda b, i: (b, i, 0)),
            pl.BlockSpec((1, SUBLANES, D_RW_IN), lambda b, i: (b, jnp.maximum(i * r8 - 1, 0), 0)),
            pl.BlockSpec((1, SUBLANES, D_RW_IN), lambda b, i: (b, jnp.minimum((i + 1) * r8, n8 - 1), 0)),
            pl.BlockSpec((1, D_RW_IN), c2),
            pl.BlockSpec((1, D_RW_IN), c2),
            pl.BlockSpec((1, 2 * D_RWKV), c2),
            pl.BlockSpec((LANES, 2 * D_RWKV), c2),
            pl.BlockSpec((1, 2 * D_RWKV), c2),
            pl.BlockSpec((LANES, 2 * D_RWKV), c2),
            pl.BlockSpec((LANES, D_RWKV), c2),
            pl.BlockSpec((1, D_RWKV), c2),
            pl.BlockSpec((1, D_RWKV), c2),
            pl.BlockSpec((1, D_RWKV), c2),
            pl.BlockSpec((D_RWKV, D_RWKV), c2),
        ],
        out_specs=[tok, tok, tok, tok2, tok2, tok2, tok, tok],
        out_shape=[s1, s1, s1, s2, s2, s2f, s1, s1],
        compiler_params=_cparams(("parallel", "parallel")),
        name="rwkv_prep",
    )(proj_rw, proj_rw, proj_rw, mup, mun, w0, wup_bd, a0, aup_bd, gup, k_k, k_a, r_k, ones_bd)


def _stack2(y):
    yb = y.astype(BF16)
    lo = _cols(yb.shape) < RWKV_HEAD
    zero = jnp.zeros_like(yb)
    return jnp.concatenate([jnp.where(lo, yb, zero), jnp.where(lo, zero, yb)], axis=0)


def _pair_mm(x, y):
    return _dot(x.astype(BF16), _stack2(y))


class _Masks:
    def __init__(self, reverse):
        shp = (CHUNK, PAIR)
        t = _rows(shp)
        s = _cols(shp) % CHUNK
        self.strict = (s > t) if reverse else (s < t)
        self.incl = (s >= t) if reverse else (s <= t)
        self.eye = s == t
        self.blk16 = (t // 16) == (s // 16)
        self.lvl32 = ((t // 32) == (s // 32)) & ((t // 16) != (s // 16))
        self.lvl64 = (t // 32) != (s // 32)
        sq = (PAIR, PAIR)
        self.bd = (_rows(sq) // RWKV_HEAD) == (_cols(sq) // RWKV_HEAD)
        self.eye_sq = _rows(sq) == _cols(sq)
        tt = _rows((CHUNK, CHUNK))
        ss = _cols((CHUNK, CHUNK))
        self.tri = jnp.where((ss >= tt) if reverse else (ss <= tt), 1.0, 0.0).astype(BF16)
        self.reverse = reverse


def _dot_hilo_l(w_bf16, x):
    hi, lo = _split(x)
    return _dot(w_bf16, hi) + _dot(w_bf16, lo)


def _col(x, p):
    return x[:, p * PAIR:(p + 1) * PAIR]


def _wkv_kernel(rf_ref, kkf_ref, vf_ref, kdf_ref, kkaf_ref, lwf_ref,
                rb_ref, kkb_ref, vb_ref, kdb_ref, kkab_ref, lwb_ref,
                yf_ref, yb_ref, h_ref, *, n_sub):
    @pl.when(pl.program_id(1) == 0)
    def _():
        h_ref[...] = jnp.zeros_like(h_ref)

    dirs = (
        (False, rf_ref, kkf_ref, vf_ref, kdf_ref, kkaf_ref, lwf_ref, yf_ref),
        (True, rb_ref, kkb_ref, vb_ref, kdb_ref, kkab_ref, lwb_ref, yb_ref),
    )
    masks = (_Masks(False), _Masks(True))
    chunks = [(di, j) for di in range(2) for j in range(n_sub)]
    units = [(di, j, p) for (di, j) in chunks for p in range(N_PAIRS)]

    rb, ab, bt, kt, bh, kh, vv, etot = {}, {}, {}, {}, {}, {}, {}, {}
    for c in chunks:
        di, j = c
        reverse, r_ref, kk_ref, v_ref, kd_ref, kka_ref, lw_ref, _ = dirs[di]
        rs = slice(j * CHUNK, (j + 1) * CHUNK)
        lw = lw_ref[0, 0, rs, :]
        kd = kd_ref[0, 0, rs, :].astype(F32)
        kka = kka_ref[0, 0, rs, :].astype(F32)
        cum = _dot_hilo_l(masks[di].tri, lw)
        tot = cum[0:1] if reverse else cum[CHUNK - 1: CHUNK]
        rb[c] = r_ref[0, rs, :].astype(F32) * jnp.exp(cum)
        ab[c] = -kk_ref[0, rs, :].astype(F32) * jnp.exp(cum - lw)
        ip = jnp.exp(-cum)
        bt[c] = kka * ip
        kt[c] = kd * ip
        ph = jnp.exp(tot - cum)
        bh[c] = kka * ph
        kh[c] = kd * ph
        vv[c] = v_ref[0, rs, :]
        etot[c] = jnp.exp(tot)

    def per_unit(fn):
        return {u: fn(u, (u[0], u[1]), u[2], masks[u[0]]) for u in units}

    s_all = per_unit(lambda u, c, p, m: _dot_nt(
        jnp.concatenate([_col(ab[c], p), _col(rb[c], p)], axis=0).astype(BF16),
        jnp.concatenate([_stack2(_col(bt[c], p)), _stack2(_col(kt[c], p))], axis=0)))
    n_ab = per_unit(lambda u, c, p, m: jnp.where(m.strict, s_all[u][:CHUNK, :PAIR], 0.0))
    a_ak = per_unit(lambda u, c, p, m: jnp.where(m.strict, s_all[u][:CHUNK, PAIR:], 0.0).astype(BF16))
    m_rb = per_unit(lambda u, c, p, m: jnp.where(m.incl, s_all[u][CHUNK:, :PAIR], 0.0).astype(BF16))
    m_rk = per_unit(lambda u, c, p, m: jnp.where(m.incl, s_all[u][CHUNK:, PAIR:], 0.0).astype(BF16))

    nd = per_unit(lambda u, c, p, m: jnp.where(m.blk16, n_ab[u], 0.0))
    t_inv = per_unit(lambda u, c, p, m: jnp.where(m.eye, 1.0, nd[u]))
    pw = per_unit(lambda u, c, p, m: _pair_mm(nd[u], nd[u]))
    for step in range(3):
        t_inv = per_unit(lambda u, c, p, m: t_inv[u] + _pair_mm(t_inv[u], pw[u]))
        if step < 2:
            pw = per_unit(lambda u, c, p, m: _pair_mm(pw[u], pw[u]))
    for lvl in ("lvl32", "lvl64"):
        tc = per_unit(lambda u, c, p, m: _pair_mm(t_inv[u], jnp.where(getattr(m, lvl), n_ab[u], 0.0)))
        t_inv = per_unit(lambda u, c, p, m: t_inv[u] + _pair_mm(tc[u], t_inv[u]))

    v2 = per_unit(lambda u, c, p, m: _stack2(_col(vv[c], p)))
    akv = per_unit(lambda u, c, p, m: _dot(a_ak[u], v2[u]))
    wu = per_unit(lambda u, c, p, m: _dot(
        t_inv[u].astype(BF16), jnp.concatenate([_stack2(_col(ab[c], p)), _stack2(akv[u])], axis=1)))
    qy = per_unit(lambda u, c, p, m: _dot(
        m_rb[u], jnp.concatenate([_stack2(wu[u][:, :PAIR]), _stack2(wu[u][:, PAIR:])], axis=1)))
    q_hat = per_unit(lambda u, c, p, m: (_col(rb[c], p) + qy[u][:, :PAIR]).astype(BF16))
    y_loc = per_unit(lambda u, c, p, m: qy[u][:, PAIR:] + _dot(m_rk[u], v2[u]))
    gd = per_unit(lambda u, c, p, m: _dot(_col(bh[c], p).T.astype(BF16), wu[u].astype(BF16)))
    kv = per_unit(lambda u, c, p, m: _dot(_col(kh[c], p).T.astype(BF16), _col(vv[c], p).astype(BF16)))
    g_m = per_unit(lambda u, c, p, m: (jnp.where(m.bd, gd[u][:, :PAIR], 0.0)
                                       + jnp.where(m.eye_sq, _col(etot[c], p), 0.0)).astype(BF16))
    d_m = per_unit(lambda u, c, p, m: jnp.where(m.bd, gd[u][:, PAIR:] + kv[u], 0.0))

    h = {(di, p): h_ref[di, p] for di in range(2) for p in range(N_PAIRS)}
    for step in range(n_sub):
        for di in range(2):
            j = n_sub - 1 - step if dirs[di][0] else step
            y_ref = dirs[di][7]
            rs = slice(j * CHUNK, (j + 1) * CHUNK)
            for p in range(N_PAIRS):
                u = (di, j, p)
                hb = h[(di, p)].astype(BF16)
                y_ref[0, rs, p * PAIR:(p + 1) * PAIR] = (_dot(q_hat[u], hb) + y_loc[u]).astype(BF16)
                h[(di, p)] = _dot(g_m[u], hb) + d_m[u]
    for di in range(2):
        for p in range(N_PAIRS):
            h_ref[di, p] = h[(di, p)]


def _wkv_call(r, kkn, v, kd, kka, lw, tm):
    bn, t, _ = r.shape
    nt = t // tm
    fwd = pl.BlockSpec((1, tm, D_RWKV), lambda b, i: (b, i, 0))
    bwd = pl.BlockSpec((1, tm, D_RWKV), lambda b, i: (b, nt - 1 - i, 0))
    fwd2 = pl.BlockSpec((1, 1, tm, D_RWKV), lambda b, i: (0, b, i, 0))
    bwd2 = pl.BlockSpec((1, 1, tm, D_RWKV), lambda b, i: (1, b, nt - 1 - i, 0))
    s1 = jax.ShapeDtypeStruct((bn, t, D_RWKV), BF16)
    return pl.pallas_call(
        functools.partial(_wkv_kernel, n_sub=tm // CHUNK),
        grid=(bn, nt),
        in_specs=[fwd, fwd, fwd, fwd2, fwd2, fwd2, bwd, bwd, bwd, bwd2, bwd2, bwd2],
        out_specs=[fwd, bwd],
        out_shape=[s1, s1],
        scratch_shapes=[pltpu.VMEM((2, N_PAIRS, PAIR, PAIR), F32)],
        compiler_params=_cparams(("parallel", "arbitrary")),
        name="wkv7_chunked",
    )(r, kkn, v, kd, kka, lw, r, kkn, v, kd, kka, lw)


def _gelu_tanh(x):
    return 0.5 * x * (1.0 + jnp.tanh(0.7978845608028654 * (x + 0.044715 * (x * x * x))))


def _route(logits_t, b_col):
    rows = [logits_t[e:e + 1] for e in range(N_EXPERTS)]
    mx = functools.reduce(jnp.maximum, rows)
    ex = [jnp.exp(x - mx) for x in rows]
    den = functools.reduce(lambda a, b: a + b, ex)
    probs = [e / den for e in ex]
    sel = [probs[e] + b_col[e:e + 1] for e in range(N_EXPERTS)]
    scores = []
    for g in range(N_GROUPS):
        a, b, c, d = sel[4 * g: 4 * g + 4]
        hi1, lo1 = jnp.maximum(a, b), jnp.minimum(a, b)
        hi2, lo2 = jnp.maximum(c, d), jnp.minimum(c, d)
        scores.append(jnp.maximum(hi1, hi2) + jnp.maximum(jnp.minimum(hi1, hi2), jnp.maximum(lo1, lo2)))
    best = scores[0]
    bg = jnp.zeros_like(best)
    for g in range(1, N_GROUPS):
        upd = scores[g] > best
        best = jnp.where(upd, scores[g], best)
        bg = jnp.where(upd, float(g), bg)

    def pick(vals):
        out = []
        for j in range(EXP_PER_GROUP):
            x = vals[j]
            for g in range(1, N_GROUPS):
                x = jnp.where(bg == float(g), vals[4 * g + j], x)
            out.append(x)
        return out

    sg = pick(sel)
    pg = pick(probs)
    v1, i1 = sg[0], jnp.zeros_like(best)
    for j in range(1, EXP_PER_GROUP):
        upd = sg[j] > v1
        v1 = jnp.where(upd, sg[j], v1)
        i1 = jnp.where(upd, float(j), i1)
    neg = jnp.full_like(best, -jnp.inf)
    v2, i2 = neg, jnp.zeros_like(best)
    for j in range(EXP_PER_GROUP):
        cand = jnp.where(i1 == float(j), neg, sg[j])
        upd = cand > v2
        v2 = jnp.where(upd, cand, v2)
        i2 = jnp.where(upd, float(j), i2)
    p1 = jnp.zeros_like(best)
    p2 = jnp.zeros_like(best)
    for j in range(EXP_PER_GROUP):
        p1 = jnp.where(i1 == float(j), pg[j], p1)
        p2 = jnp.where(i2 == float(j), pg[j], p2)
    tot = p1 + p2
    return bg * float(EXP_PER_GROUP) + i1, bg * float(EXP_PER_GROUP) + i2, p1 / tot, p2 / tot


def _outproj_kernel(hl_ref, gate_ref, yf_ref, yb_ref, bonus_ref, g_ref, x_ref, lnw_ref, lnb_ref, g1_ref,
                    sh2_ref, sc2_ref, n2_ref, wout_ref, wr_ref, br_ref, ones_ref,
                    xn_ref, h2_ref, route_ref):
    lru_out = (hl_ref[0, 0] + hl_ref[1, 0]) * _gelu_tanh(gate_ref[0])
    ones_bd = ones_ref[...]
    y = yf_ref[0].astype(F32) + yb_ref[0].astype(F32)
    mu = _dot_hilo(y, ones_bd) * (1.0 / RWKV_HEAD)
    yc = y - mu
    var = _dot_hilo(yc * yc, ones_bd) * (1.0 / RWKV_HEAD)
    gn = yc * lax.rsqrt(var + GN_EPS) * lnw_ref[...] + lnb_ref[...]
    rw_out = (gn + bonus_ref[0].astype(F32)) * g_ref[0].astype(F32)
    o = _dot(lru_out.astype(BF16), wout_ref[:D_LRU]) + _dot(rw_out.astype(BF16), wout_ref[D_LRU:])
    xn = x_ref[0] + g1_ref[0] * o
    xn_ref[0] = xn
    h2 = _norm_mod(xn, n2_ref[...], sc2_ref[0], sh2_ref[0])
    for j in range(TOK_ROWS):
        h2_ref[0, pl.ds(j, xn.shape[0], stride=TOK_ROWS), :] = h2[:, j * LANES:(j + 1) * LANES]
    logits_t = _dot3_nt(wr_ref[...], h2)
    e0, e1, g0, g1 = _route(logits_t, br_ref[...])
    zero = jnp.zeros_like(e0)
    route_ref[0, 0] = jnp.concatenate([e0, e1, g0, g1, zero, zero, zero, zero], axis=0)


def _outproj_call(hl, proj_lru, yf, yb, bonus, g, x, lnw, lnb, g1, sh2, sc2, n2, w_out_bf16, wr_t, br, ones_bd, tm):
    bn, t, d = x.shape
    nt = t // tm
    c2 = lambda b, i: (0, 0)
    tok = pl.BlockSpec((1, tm, D_RWKV), lambda b, i: (b, i, 0))
    tokd = pl.BlockSpec((1, tm, d), lambda b, i: (b, i, 0))
    per_b = pl.BlockSpec((1, 1, d), lambda b, i: (b, 0, 0))
    return pl.pallas_call(
        _outproj_kernel,
        grid=(bn, nt),
        in_specs=[
            pl.BlockSpec((2, 1, tm, D_LRU), lambda b, i: (0, b, i, 0)),
            pl.BlockSpec((1, tm, D_LRU), lambda b, i: (b, i, 1)),
            tok, tok, tok, tok, tokd,
            pl.BlockSpec((1, D_RWKV), c2),
            pl.BlockSpec((1, D_RWKV), c2),
            per_b, per_b, per_b,
            pl.BlockSpec((1, d), c2),
            pl.BlockSpec((d, d), c2),
            pl.BlockSpec((N_EXPERTS, d), c2),
            pl.BlockSpec((N_EXPERTS, 1), c2),
            pl.BlockSpec((D_RWKV, D_RWKV), c2),
        ],
        out_specs=[tokd, pl.BlockSpec((1, tm * TOK_ROWS, LANES), lambda b, i: (b, i, 0)),
                   pl.BlockSpec((1, 1, SUBLANES, tm), lambda b, i: (b, i, 0, 0))],
        out_shape=[
            jax.ShapeDtypeStruct((bn, t, d), F32),
            jax.ShapeDtypeStruct((bn, t * TOK_ROWS, LANES), F32),
            jax.ShapeDtypeStruct((bn, nt, SUBLANES, tm), F32),
        ],
        compiler_params=_cparams(("parallel", "parallel")),
        name="outproj_router",
    )(hl, proj_lru, yf, yb, bonus, g, x, lnw, lnb, g1, sh2, sc2, n2, w_out_bf16, wr_t, br, ones_bd)


def _row_copy(src_ref, src_row, dst_ref, dst_row, sem):
    src = src_ref.at[pl.ds(pl.multiple_of(src_row * TOK_ROWS, TOK_ROWS), TOK_ROWS)]
    dst = dst_ref.at[pl.ds(pl.multiple_of(dst_row * TOK_ROWS, TOK_ROWS), TOK_ROWS)]
    return pltpu.make_async_copy(src, dst, sem)


def _dispatch_kernel(pos_ref, h_ref, xb_in_ref, xb_ref, sem):
    del xb_in_ref
    tm = h_ref.shape[0] // TOK_ROWS

    def issue(r, carry):
        _row_copy(h_ref, r, xb_ref, pos_ref[0, 0, r], sem).start()
        _row_copy(h_ref, r, xb_ref, pos_ref[0, 0, tm + r], sem).start()
        return carry

    lax.fori_loop(0, tm, issue, 0)

    def drain(r, carry):
        _row_copy(h_ref, 0, xb_ref, 0, sem).wait()
        return carry

    lax.fori_loop(0, 2 * tm, drain, 0)


def _dispatch_call(h2, pos_tiles, xb_init, tm):
    n = h2.shape[0] // TOK_ROWS
    nt = n // tm
    return pl.pallas_call(
        _dispatch_kernel,
        grid=(nt,),
        in_specs=[
            pl.BlockSpec((1, 1, 2 * tm), lambda i: (i, 0, 0), memory_space=pltpu.SMEM),
            pl.BlockSpec((tm * TOK_ROWS, LANES), lambda i: (i, 0)),
            pl.BlockSpec(memory_space=pl.ANY),
        ],
        out_specs=pl.BlockSpec(memory_space=pl.ANY),
        out_shape=jax.ShapeDtypeStruct(xb_init.shape, F32),
        scratch_shapes=[pltpu.SemaphoreType.DMA(())],
        input_output_aliases={2: 0},
        compiler_params=_cparams(("arbitrary",)),
        name="moe_dispatch",
    )(pos_tiles, h2, xb_init)


def _expert_kernel(blk_e_ref, n_used_ref, x_ref, wg_ref, wu_ref, wd_ref, o_ref):
    i = pl.program_id(0)

    @pl.when(i < n_used_ref[0])
    def _():
        x = jnp.concatenate([x_ref[pl.ds(j, MOE_BLOCK, stride=TOK_ROWS), :] for j in range(TOK_ROWS)],
                            axis=1).astype(BF16)
        gate = _dot(x, wg_ref[0])
        hid = gate * _sigmoid(gate) * _dot(x, wu_ref[0])
        y = _dot(hid.astype(BF16), wd_ref[0])
        for j in range(TOK_ROWS):
            o_ref[pl.ds(j, MOE_BLOCK, stride=TOK_ROWS), :] = y[:, j * LANES:(j + 1) * LANES]

    @pl.when(i >= n_used_ref[0])
    def _():
        o_ref[...] = jnp.zeros_like(o_ref)


def _expert_call(xb, blk_e, n_used, wg, wu, wd):
    n_blk = xb.shape[0] // (MOE_BLOCK * TOK_ROWS)
    d = D_MODEL
    grid_spec = pltpu.PrefetchScalarGridSpec(
        num_scalar_prefetch=2,
        grid=(n_blk,),
        in_specs=[
            pl.BlockSpec((MOE_BLOCK * TOK_ROWS, LANES), lambda i, be, nu: (i, 0)),
            pl.BlockSpec((1, d, D_EXPERT), lambda i, be, nu: (be[i], 0, 0)),
            pl.BlockSpec((1, d, D_EXPERT), lambda i, be, nu: (be[i], 0, 0)),
            pl.BlockSpec((1, D_EXPERT, d), lambda i, be, nu: (be[i], 0, 0)),
        ],
        out_specs=pl.BlockSpec((MOE_BLOCK * TOK_ROWS, LANES), lambda i, be, nu: (i, 0)),
    )
    return pl.pallas_call(
        _expert_kernel,
        grid_spec=grid_spec,
        out_shape=jax.ShapeDtypeStruct(xb.shape, F32),
        compiler_params=_cparams(("arbitrary",)),
        name="moe_experts",
    )(blk_e, n_used, xb, wg, wu, wd)


def _combine_kernel(pos_ref, posn_ref, gates_ref, x_ref, g2_ref, yb_ref, o_ref, buf_ref, sem):
    i = pl.program_id(0)
    n = pl.num_programs(0)
    tm = x_ref.shape[0]
    slot = i % 2

    def gather(p_ref, s):
        def issue(r, carry):
            _row_copy(yb_ref, p_ref[0, 0, r], buf_ref.at[s, 0], r, sem.at[s]).start()
            _row_copy(yb_ref, p_ref[0, 0, tm + r], buf_ref.at[s, 1], r, sem.at[s]).start()
            return carry

        lax.fori_loop(0, tm, issue, 0)

    @pl.when(i == 0)
    def _():
        gather(pos_ref, 0)

    @pl.when(i + 1 < n)
    def _():
        gather(posn_ref, 1 - slot)

    def drain(r, carry):
        _row_copy(yb_ref, 0, buf_ref.at[slot, 0], 0, sem.at[slot]).wait()
        return carry

    lax.fori_loop(0, 2 * tm, drain, 0)
    gts = gates_ref[...]
    g0 = gts[:, 0:1]
    g1 = gts[:, 1:2]
    for j in range(TOK_ROWS):
        rows = pl.ds(j, tm, stride=TOK_ROWS)
        cs = slice(j * LANES, (j + 1) * LANES)
        y = g0 * buf_ref[slot, 0, rows, :] + g1 * buf_ref[slot, 1, rows, :]
        o_ref[:, cs] = x_ref[:, cs] + g2_ref[0, :, cs] * y


def _combine_call(xn, yb, pos_tiles, gates, g2, t, tm):
    n, d = xn.shape
    nt = n // tm
    per_b = t // tm
    return pl.pallas_call(
        _combine_kernel,
        grid=(nt,),
        in_specs=[
            pl.BlockSpec((1, 1, 2 * tm), lambda i: (i, 0, 0), memory_space=pltpu.SMEM),
            pl.BlockSpec((1, 1, 2 * tm), lambda i: (jnp.minimum(i + 1, nt - 1), 0, 0), memory_space=pltpu.SMEM),
            pl.BlockSpec((tm, 2), lambda i: (i, 0)),
            pl.BlockSpec((tm, d), lambda i: (i, 0)),
            pl.BlockSpec((1, 1, d), lambda i: (i // per_b, 0, 0)),
            pl.BlockSpec(memory_space=pl.ANY),
        ],
        out_specs=pl.BlockSpec((tm, d), lambda i: (i, 0)),
        out_shape=jax.ShapeDtypeStruct((n, d), F32),
        scratch_shapes=[pltpu.VMEM((2, 2, tm * TOK_ROWS, LANES), F32), pltpu.SemaphoreType.DMA((2,))],
        compiler_params=_cparams(("arbitrary",)),
        name="moe_combine",
    )(pos_tiles, pos_tiles, gates, xn, g2, yb)


def _final_kernel(x_ref, nw_ref, o_ref):
    x = x_ref[...]
    ms = jnp.mean(x * x, axis=-1, keepdims=True)
    o_ref[...] = x * lax.rsqrt(ms + NORM_EPS) * nw_ref[...]


def _final_call(x, nw, tm):
    n, d = x.shape
    return pl.pallas_call(
        _final_kernel,
        grid=(n // tm,),
        in_specs=[pl.BlockSpec((tm, d), lambda i: (i, 0)), pl.BlockSpec((1, d), lambda i: (0, 0))],
        out_specs=pl.BlockSpec((tm, d), lambda i: (i, 0)),
        out_shape=jax.ShapeDtypeStruct((n, d), F32),
        compiler_params=_cparams(("parallel",)),
        name="final_norm",
    )(x, nw)


def _block_diag(w):
    h, a, b = w.shape
    eye = jnp.eye(h, dtype=w.dtype)
    return jnp.einsum("hab,hg->hagb", w, eye).reshape(h * a, h * b)


def _head_ones():
    idx = np.arange(D_RWKV) // RWKV_HEAD
    return jnp.asarray((idx[:, None] == idx[None, :]).astype(np.float32), dtype=BF16)


def _routing_tables(route, n_tok, tm):
    bn, nt = route.shape[0], route.shape[1]
    flat = jnp.transpose(route, (2, 0, 1, 3)).reshape(SUBLANES, n_tok)
    e0 = flat[0].astype(jnp.int32)
    e1 = flat[1].astype(jnp.int32)
    gates = jnp.stack([flat[2], flat[3]], axis=1)
    ar = jnp.arange(N_EXPERTS, dtype=jnp.int32)
    oh = (e0[:, None] == ar).astype(jnp.int32) + (e1[:, None] == ar).astype(jnp.int32)
    cs = jnp.cumsum(oh, axis=0)
    counts = cs[-1]
    excl = cs - oh
    padded = (counts + MOE_BLOCK - 1) // MOE_BLOCK * MOE_BLOCK
    pad_end = jnp.cumsum(padded)
    pad_start = pad_end - padded
    base = excl + pad_start[None, :]
    pos0 = jnp.take_along_axis(base, e0[:, None], axis=1)[:, 0]
    pos1 = jnp.take_along_axis(base, e1[:, None], axis=1)[:, 0]
    n_asg = 2 * n_tok
    n_blk = (n_asg + N_EXPERTS * (MOE_BLOCK - 1) + MOE_BLOCK - 1) // MOE_BLOCK
    blk_start = jnp.arange(n_blk, dtype=jnp.int32) * MOE_BLOCK
    blk_e = jnp.minimum(jnp.sum(pad_end[None, :] <= blk_start[:, None], axis=-1), N_EXPERTS - 1).astype(jnp.int32)
    n_used = (pad_end[-1] // MOE_BLOCK).astype(jnp.int32).reshape(1)
    pos_tiles = jnp.concatenate(
        [pos0.reshape(n_tok // tm, 1, tm), pos1.reshape(n_tok // tm, 1, tm)], axis=2).astype(jnp.int32)
    return pos_tiles, gates, blk_e, n_used, n_blk * MOE_BLOCK


def _tile(t, want):
    return min(t, want)


def _trunk(x, c, w_mod, b_mod, norm1, norm2, w_in, w_out, conv_w, conv_b, lru_wa, lru_ba, lru_wx, lru_bx, lru_lam,
           mu_prev, mu_next, rw_w0, rw_wup, rw_a0, rw_aup, rw_gup, rw_kk, rw_ka, rw_rk, ln_x_w, ln_x_b,
           w_router, b_router, exp_gate, exp_up, exp_down, norm_f):
    bn, t, d = x.shape
    n_tok = bn * t
    depth = w_mod.shape[0]
    ones_bd = _head_ones()
    mod = _mod_call(c, w_mod, b_mod)
    wr_t = jnp.transpose(w_router)
    br = b_router.reshape(N_EXPERTS, 1)
    tm_moe = _tile(t, 256)
    xb = None

    for l in range(depth):
        sh1, sc1, g1, sh2, sc2, g2 = [m.reshape(bn, 1, d) for m in jnp.split(mod[l], 6, axis=-1)]
        proj_lru, proj_rw = _inproj_call(x, sh1, sc1, norm1[l].reshape(1, d), w_in[l].astype(BF16), _tile(t, 512))

        wg = jnp.stack([jnp.concatenate([_block_diag(lru_wa[l, dd]), _block_diag(lru_wx[l, dd])], axis=1)
                        for dd in range(2)]).astype(BF16)
        bg = jnp.concatenate([lru_ba[l], lru_bx[l]], axis=1).reshape(2, 1, 2 * D_LRU)
        sp = jax.nn.softplus(-lru_lam[l]).reshape(2, 1, D_LRU)
        hl = _lru_call(proj_lru, conv_w[l], conv_b[l].reshape(1, D_LRU), wg, bg, sp, _tile(t, 512))

        zeros = jnp.zeros((64, D_RWKV), F32)
        wup_bd = jnp.concatenate([jnp.concatenate([rw_wup[l, 0], zeros], axis=1),
                                  jnp.concatenate([zeros, rw_wup[l, 1]], axis=1)], axis=0).astype(BF16)
        aup_bd = jnp.concatenate([jnp.concatenate([rw_aup[l, 0], zeros], axis=1),
                                  jnp.concatenate([zeros, rw_aup[l, 1]], axis=1)], axis=0).astype(BF16)
        r, kkn, v, kd, kka, lw, bonus, g = _rwprep_call(
            proj_rw, mu_prev[l].reshape(1, -1), mu_next[l].reshape(1, -1),
            rw_w0[l].reshape(1, -1), wup_bd, rw_a0[l].reshape(1, -1), aup_bd, rw_gup[l].astype(BF16),
            rw_kk[l].reshape(1, -1), rw_ka[l].reshape(1, -1), rw_rk[l].reshape(1, -1), ones_bd, _tile(t, 256))
        yf, yb = _wkv_call(r, kkn, v, kd, kka, lw, _tile(t, 128))

        xn, h2, route = _outproj_call(
            hl, proj_lru, yf, yb, bonus, g, x, ln_x_w[l].reshape(1, -1), ln_x_b[l].reshape(1, -1),
            g1, sh2, sc2, norm2[l].reshape(1, d), w_out[l].astype(BF16), wr_t, br, ones_bd, tm_moe)

        pos_tiles, gates, blk_e, n_used, n_rows = _routing_tables(route, n_tok, tm_moe)
        xb_init = jnp.zeros((n_rows * TOK_ROWS, LANES), F32) if xb is None else xb
        xb = _dispatch_call(h2.reshape(n_tok * TOK_ROWS, LANES), pos_tiles, xb_init, tm_moe)
        ybuf = _expert_call(xb, blk_e, n_used, exp_gate[l].astype(BF16), exp_up[l].astype(BF16),
                            exp_down[l].astype(BF16))
        x = _combine_call(xn.reshape(n_tok, d), ybuf, pos_tiles, gates, g2, t, tm_moe).reshape(bn, t, d)

    return _final_call(x.reshape(n_tok, d), norm_f.reshape(1, d), _tile(n_tok, 512)).reshape(bn, t, d)


def kernel(x_prompt, x_sample, c_prompt, c_sample, w_mod, b_mod, norm1, norm2, w_in, w_out, conv_w, conv_b, lru_wa, lru_ba, lru_wx, lru_bx, lru_lam, mu_prev, mu_next, rw_w0, rw_wup, rw_a0, rw_aup, rw_gup, rw_kk, rw_ka, rw_rk, ln_x_w, ln_x_b, w_router, b_router, exp_gate, exp_up, exp_down, norm_f):
    bp = x_prompt.shape[0]
    x = jnp.concatenate([x_prompt, x_sample], axis=0)
    c = jnp.concatenate([c_prompt, c_sample], axis=0).astype(F32)
    y = _trunk(x, c, w_mod, b_mod, norm1, norm2, w_in, w_out, conv_w, conv_b, lru_wa, lru_ba, lru_wx, lru_bx,
               lru_lam, mu_prev, mu_next, rw_w0, rw_wup, rw_a0, rw_aup, rw_gup, rw_kk, rw_ka, rw_rk, ln_x_w,
               ln_x_b, w_router, b_router, exp_gate, exp_up, exp_down, norm_f)
    return (y[:bp], y[bp:])
```

```python
import functools

import jax
import jax.numpy as jnp
import numpy as np
from jax import lax
from jax.experimental import pallas as pl
from jax.experimental.pallas import tpu as pltpu

F32 = jnp.float32
BF16 = jnp.bfloat16

D_MODEL = 1024
D_LRU = 512
D_RWKV = 512
LRU_HEADS = 8
LRU_C = 8.0
RWKV_HEAD = 64
N_EXPERTS = 16
N_GROUPS = 4
EXP_PER_GROUP = 4
D_EXPERT = 512
MOE_BLOCK = 512
NORM_EPS = 1e-6
GN_EPS = 64e-5
D_RW_IN = 1920
D_IN = 2944

LANES = 128
SUBLANES = 8
CHUNK = 64
PAIR = 2 * RWKV_HEAD
N_PAIRS = D_RWKV // PAIR
TOK_ROWS = D_MODEL // LANES


def _cparams(sem, vmem_mb=48):
    return pltpu.CompilerParams(dimension_semantics=sem, vmem_limit_bytes=vmem_mb * 1024 * 1024)


def _dot(a, b):
    return jnp.dot(a, b, preferred_element_type=F32)


def _dot_nt(a, b):
    return lax.dot_general(a, b, (((1,), (1,)), ((), ())), preferred_element_type=F32)


def _split(x):
    hi = x.astype(BF16)
    lo = (x - hi.astype(F32)).astype(BF16)
    return hi, lo


def _dot_hilo(x, w_bf16):
    hi, lo = _split(x)
    return _dot(hi, w_bf16) + _dot(lo, w_bf16)


def _dot3(a, b):
    ah, al = _split(a)
    bh, bl = _split(b)
    return _dot(ah, bh) + (_dot(ah, bl) + _dot(al, bh))


def _dot3_nt(a, b):
    ah, al = _split(a)
    bh, bl = _split(b)
    return _dot_nt(ah, bh) + (_dot_nt(ah, bl) + _dot_nt(al, bh))


def _sigmoid(x):
    return 1.0 / (1.0 + jnp.exp(-x))


def _softplus(x):
    return jnp.maximum(x, 0.0) + jnp.log(1.0 + jnp.exp(-jnp.abs(x)))


def _rows(shape):
    return lax.broadcasted_iota(jnp.int32, shape, 0)


def _cols(shape):
    return lax.broadcasted_iota(jnp.int32, shape, 1)


def _mod_kernel(c_ref, w_ref, b_ref, o_ref):
    c = c_ref[...]
    cs = c * _sigmoid(c)
    o_ref[0] = _dot3(cs, w_ref[0]) + b_ref[0]


def _mod_call(c, w_mod, b_mod):
    nl, d, n6 = w_mod.shape
    bn = c.shape[0]
    tn = 1536
    return pl.pallas_call(
        _mod_kernel,
        grid=(nl, n6 // tn),
        in_specs=[
            pl.BlockSpec((bn, d), lambda l, j: (0, 0)),
            pl.BlockSpec((1, d, tn), lambda l, j: (l, 0, j)),
            pl.BlockSpec((1, 1, tn), lambda l, j: (l, 0, j)),
        ],
        out_specs=pl.BlockSpec((1, bn, tn), lambda l, j: (l, 0, j)),
        out_shape=jax.ShapeDtypeStruct((nl, bn, n6), F32),
        compiler_params=_cparams(("arbitrary", "arbitrary")),
        name="adaln_mod",
    )(c, w_mod, b_mod.reshape(nl, 1, n6))


def _norm_mod(x, nw, sc, sh):
    ms = jnp.mean(x * x, axis=-1, keepdims=True)
    return (x * lax.rsqrt(ms + NORM_EPS) * nw) * (1.0 + sc) + sh


def _x_specs(tm, d, n_a, nt):
    spec_a = pl.BlockSpec((1, tm, d), lambda b, i: (jnp.minimum(b, n_a - 1), jnp.where(b < n_a, i, nt - 1), 0))
    spec_b = pl.BlockSpec((1, tm, d), lambda b, i: (jnp.maximum(b - n_a, 0), jnp.where(b < n_a, 0, i), 0))
    return spec_a, spec_b


def _x_block(xa_ref, xb_ref, n_a):
    return jnp.where(pl.program_id(0) < n_a, xa_ref[0], xb_ref[0])


def _inproj_kernel(xa_ref, xb_ref, sh_ref, sc_ref, nw_ref, w_ref, lru_ref, rw_ref, *, n_a):
    h = _norm_mod(_x_block(xa_ref, xb_ref, n_a), nw_ref[...], sc_ref[0], sh_ref[0]).astype(BF16)
    lru_ref[0] = _dot(h, w_ref[:, : 2 * D_LRU])
    rw_ref[0] = _dot(h, w_ref[:, 2 * D_LRU:])


def _inproj_call(xa, xb, sh, sc, nw, w_in_bf16, tm):
    n_a, t, d = xa.shape
    bn = sh.shape[0]
    nt = t // tm
    spec_a, spec_b = _x_specs(tm, d, n_a, nt)
    return pl.pallas_call(
        functools.partial(_inproj_kernel, n_a=n_a),
        grid=(bn, nt),
        in_specs=[
            spec_a, spec_b,
            pl.BlockSpec((1, 1, d), lambda b, i: (b, 0, 0)),
            pl.BlockSpec((1, 1, d), lambda b, i: (b, 0, 0)),
            pl.BlockSpec((1, d), lambda b, i: (0, 0)),
            pl.BlockSpec((d, D_IN), lambda b, i: (0, 0)),
        ],
        out_specs=[
            pl.BlockSpec((1, tm, 2 * D_LRU), lambda b, i: (b, i, 0)),
            pl.BlockSpec((1, tm, D_RW_IN), lambda b, i: (b, i, 0)),
        ],
        out_shape=[
            jax.ShapeDtypeStruct((bn, t, 2 * D_LRU), F32),
            jax.ShapeDtypeStruct((bn, t, D_RW_IN), F32),
        ],
        compiler_params=_cparams(("parallel", "parallel")),
        name="norm_inproj",
    )(xa, xb, sh, sc, nw, w_in_bf16)


def _lru_kernel(x_ref, prev_ref, next_ref, cw_ref, cb_ref, wg_ref, bg_ref, sp_ref, o_ref,
                carry_ref, a_scr, b_scr, *, n_tiles):
    d = pl.program_id(1)
    i = pl.program_id(2)
    ti = d * (n_tiles - 1) + (1 - 2 * d) * i
    tm = x_ref.shape[1]

    @pl.when(i == 0)
    def _():
        carry_ref[...] = jnp.zeros_like(carry_ref)

    x = x_ref[0]
    prev8 = jnp.where(ti == 0, 0.0, prev_ref[0])
    next8 = jnp.where(ti == n_tiles - 1, 0.0, next_ref[0])
    r = _rows(x.shape)
    xm1 = jnp.where(r == 0, prev8[7:8], pltpu.roll(x, 1, 0))
    xm2 = jnp.where(r == 0, prev8[6:7], jnp.where(r == 1, prev8[7:8], pltpu.roll(x, 2, 0)))
    xp1 = jnp.where(r == tm - 1, next8[0:1], pltpu.roll(x, tm - 1, 0))
    cw = cw_ref[...]
    u = cw[0:1] * xm2 + cw[1:2] * xm1 + cw[2:3] * x + cw[3:4] * xp1 + cb_ref[...]

    gates = _dot(u.astype(BF16), wg_ref[0]) + bg_ref[0]
    rg = _sigmoid(gates[:, :D_LRU])
    ig = _sigmoid(gates[:, D_LRU:])
    log_a = (-LRU_C) * rg * sp_ref[0]
    a = jnp.exp(log_a)
    bv = jnp.sqrt(1.0 - jnp.exp(2.0 * log_a)) * (ig * u)

    n_lt = D_LRU // LANES
    for c in range(n_lt):
        a_scr[c] = a[:, c * LANES:(c + 1) * LANES]
        b_scr[c] = bv[:, c * LANES:(c + 1) * LANES]
    win = SUBLANES * SUBLANES

    def scan(reverse):
        r8 = _rows((SUBLANES, D_LRU))
        carry = carry_ref[...]
        n_win = tm // win
        for w in (range(n_win - 1, -1, -1) if reverse else range(n_win)):
            base = w * win
            hs = [jnp.zeros((SUBLANES, LANES), F32)] * n_lt
            accs = [jnp.ones((SUBLANES, LANES), F32)] * n_lt
            for g in (range(SUBLANES - 1, -1, -1) if reverse else range(SUBLANES)):
                rows = pl.ds(base + g, SUBLANES, stride=SUBLANES)
                for c in range(n_lt):
                    ag = a_scr[c, rows, :]
                    hs[c] = ag * hs[c] + b_scr[c, rows, :]
                    accs[c] = ag * accs[c]
                    b_scr[c, rows, :] = hs[c]
                    a_scr[c, rows, :] = accs[c]
            h = jnp.concatenate(hs, axis=1)
            acc = jnp.concatenate(accs, axis=1)
            s = 1
            while s < SUBLANES:
                keep = (r8 >= SUBLANES - s) if reverse else (r8 < s)
                sh = SUBLANES - s if reverse else s
                h = acc * jnp.where(keep, 0.0, pltpu.roll(h, sh, 0)) + h
                acc = acc * jnp.where(keep, 1.0, pltpu.roll(acc, sh, 0))
                s *= 2
            ends = h + acc * carry
            if reverse:
                enter = jnp.where(r8 == SUBLANES - 1, carry, pltpu.roll(ends, SUBLANES - 1, 0))
                carry = ends[0:1]
            else:
                enter = jnp.where(r8 == 0, carry, pltpu.roll(ends, 1, 0))
                carry = ends[SUBLANES - 1: SUBLANES]
            for r in range(SUBLANES):
                rs = slice(base + r * SUBLANES, base + (r + 1) * SUBLANES)
                for c in range(n_lt):
                    cs = slice(c * LANES, (c + 1) * LANES)
                    o_ref[0, 0, rs, cs] = b_scr[c, rs, :] + a_scr[c, rs, :] * enter[r:r + 1, cs]
        carry_ref[...] = carry

    @pl.when(d == 0)
    def _():
        scan(False)

    @pl.when(d == 1)
    def _():
        scan(True)


def _lru_call(proj_lru, conv_w, conv_b, wg, bg, sp, tm):
    bn, t, _ = proj_lru.shape
    nt = t // tm
    r8 = tm // SUBLANES
    n8 = t // SUBLANES

    def tile(d, i):
        return d * (nt - 1) + (1 - 2 * d) * i

    return pl.pallas_call(
        functools.partial(_lru_kernel, n_tiles=nt),
        grid=(bn, 2, nt),
        in_specs=[
            pl.BlockSpec((1, tm, D_LRU), lambda b, d, i: (b, tile(d, i), 0)),
            pl.BlockSpec((1, SUBLANES, D_LRU), lambda b, d, i: (b, jnp.maximum(tile(d, i) * r8 - 1, 0), 0)),
            pl.BlockSpec((1, SUBLANES, D_LRU), lambda b, d, i: (b, jnp.minimum((tile(d, i) + 1) * r8, n8 - 1), 0)),
            pl.BlockSpec((4, D_LRU), lambda b, d, i: (0, 0)),
            pl.BlockSpec((1, D_LRU), lambda b, d, i: (0, 0)),
            pl.BlockSpec((1, D_LRU, 2 * D_LRU), lambda b, d, i: (d, 0, 0)),
            pl.BlockSpec((1, 1, 2 * D_LRU), lambda b, d, i: (d, 0, 0)),
            pl.BlockSpec((1, 1, D_LRU), lambda b, d, i: (d, 0, 0)),
        ],
        out_specs=pl.BlockSpec((1, 1, tm, D_LRU), lambda b, d, i: (d, b, tile(d, i), 0)),
        out_shape=jax.ShapeDtypeStruct((2, bn, t, D_LRU), F32),
        scratch_shapes=[pltpu.VMEM((1, D_LRU), F32),
                        pltpu.VMEM((D_LRU // LANES, tm, LANES), F32),
                        pltpu.VMEM((D_LRU // LANES, tm, LANES), F32)],
        compiler_params=_cparams(("parallel", "arbitrary", "arbitrary")),
        name="rg_lru",
    )(proj_lru, proj_lru, proj_lru, conv_w, conv_b, wg, bg, sp)


def _rwprep_kernel(z_ref, prev_ref, next_ref, mup_ref, mun_ref, w0_ref, wup_ref, a0_ref, aup_ref, gup_ref,
                   kk_ref, ka_ref, rk_ref, ones_ref,
                   r_out, kkn_out, v_out, kd_out, kka_out, lw_out, bonus_out, g_out, *, n_tiles):
    i = pl.program_id(1)
    tm = z_ref.shape[1]
    zc = z_ref[0]
    prev8 = jnp.where(i == 0, 0.0, prev_ref[0])
    next8 = jnp.where(i == n_tiles - 1, 0.0, next_ref[0])
    rr = _rows(zc.shape)
    zp = jnp.where(rr == 0, prev8[7:8], pltpu.roll(zc, 1, 0))
    zn = jnp.where(rr == tm - 1, next8[0:1], pltpu.roll(zc, tm - 1, 0))
    z = zc + mup_ref[...] * (zp - zc) + mun_ref[...] * (zn - zc)

    r = z[:, 0:512]
    k = z[:, 512:1024]
    v = z[:, 1024:1536]
    wd = z[:, 1536:1664]
    ad = z[:, 1664:1792]
    gd = z[:, 1792:1920]
    ones_bd = ones_ref[...]

    kkr = k * kk_ref[...]
    ss = _dot_hilo(kkr * kkr, ones_bd)
    kkn = kkr / jnp.maximum(jnp.sqrt(ss), 1e-12)

    wlin = w0_ref[...] + _dot(jnp.tanh(wd).astype(BF16), wup_ref[...])
    lw = -jnp.exp(-_softplus(-wlin) - 0.5)
    a = _sigmoid(a0_ref[...] + _dot(ad.astype(BF16), aup_ref[...]))

    ka = ka_ref[...]
    k_sum = jnp.zeros_like(k)
    for d in range(2):
        a_d = a[:, d * D_RWKV:(d + 1) * D_RWKV]
        kd = k * (1.0 + (a_d - 1.0) * ka)
        kd_out[d, 0] = kd.astype(BF16)
        kka_out[d, 0] = (kkn * a_d).astype(BF16)
        lw_out[d, 0] = lw[:, d * D_RWKV:(d + 1) * D_RWKV]
        k_sum = k_sum + kd
    r_out[0] = r.astype(BF16)
    kkn_out[0] = kkn.astype(BF16)
    v_out[0] = v.astype(BF16)
    bonus_out[0] = (_dot_hilo(r * k_sum * rk_ref[...], ones_bd) * v).astype(BF16)
    g_out[0] = _dot(_sigmoid(gd).astype(BF16), gup_ref[...]).astype(BF16)


def _rwprep_call(proj_rw, mup, mun, w0, wup_bd, a0, aup_bd, gup, k_k, k_a, r_k, ones_bd, tm):
    bn, t, _ = proj_rw.shape
    nt = t // tm
    r8 = tm // SUBLANES
    n8 = t // SUBLANES
    c2 = lambda b, i: (0, 0)
    tok = pl.BlockSpec((1, tm, D_RWKV), lambda b, i: (b, i, 0))
    tok2 = pl.BlockSpec((2, 1, tm, D_RWKV), lambda b, i: (0, b, i, 0))
    s1 = jax.ShapeDtypeStruct((bn, t, D_RWKV), BF16)
    s2 = jax.ShapeDtypeStruct((2, bn, t, D_RWKV), BF16)
    s2f = jax.ShapeDtypeStruct((2, bn, t, D_RWKV), F32)
    return pl.pallas_call(
        functools.partial(_rwprep_kernel, n_tiles=nt),
        grid=(bn, nt),
        in_specs=[
            pl.BlockSpec((1, tm, D_RW_IN), lambda b, i: (b, i, 0)),
            pl.BlockSpec((1, SUBLANES, D_RW_IN), lambda b, i: (b, jnp.maximum(i * r8 - 1, 0), 0)),
            pl.BlockSpec((1, SUBLANES, D_RW_IN), lambda b, i: (b, jnp.minimum((i + 1) * r8, n8 - 1), 0)),
            pl.BlockSpec((1, D_RW_IN), c2),
            pl.BlockSpec((1, D_RW_IN), c2),
            pl.BlockSpec((1, 2 * D_RWKV), c2),
            pl.BlockSpec((LANES, 2 * D_RWKV), c2),
            pl.BlockSpec((1, 2 * D_RWKV), c2),
            pl.BlockSpec((LANES, 2 * D_RWKV), c2),
            pl.BlockSpec((LANES, D_RWKV), c2),
            pl.BlockSpec((1, D_RWKV), c2),
            pl.BlockSpec((1, D_RWKV), c2),
            pl.BlockSpec((1, D_RWKV), c2),
            pl.BlockSpec((D_RWKV, D_RWKV), c2),
        ],
        out_specs=[tok, tok, tok, tok2, tok2, tok2, tok, tok],
        out_shape=[s1, s1, s1, s2, s2, s2f, s1, s1],
        compiler_params=_cparams(("parallel", "parallel")),
        name="rwkv_prep",
    )(proj_rw, proj_rw, proj_rw, mup, mun, w0, wup_bd, a0, aup_bd, gup, k_k, k_a, r_k, ones_bd)


def _stack2(y):
    yb = y.astype(BF16)
    lo = _cols(yb.shape) < RWKV_HEAD
    zero = jnp.zeros_like(yb)
    return jnp.concatenate([jnp.where(lo, yb, zero), jnp.where(lo, zero, yb)], axis=0)


def _pair_mm(x, y):
    return _dot(x.astype(BF16), _stack2(y))


class _Masks:
    def __init__(self, reverse):
        shp = (CHUNK, PAIR)
        t = _rows(shp)
        s = _cols(shp) % CHUNK
        self.strict = (s > t) if reverse else (s < t)
        self.incl = (s >= t) if reverse else (s <= t)
        self.eye = s == t
        self.blk16 = (t // 16) == (s // 16)
        self.lvl32 = ((t // 32) == (s // 32)) & ((t // 16) != (s // 16))
        self.lvl64 = (t // 32) != (s // 32)
        sq = (PAIR, PAIR)
        self.bd = (_rows(sq) // RWKV_HEAD) == (_cols(sq) // RWKV_HEAD)
        self.eye_sq = _rows(sq) == _cols(sq)
        tt = _rows((CHUNK, CHUNK))
        ss = _cols((CHUNK, CHUNK))
        self.tri = jnp.where((ss >= tt) if reverse else (ss <= tt), 1.0, 0.0).astype(BF16)
        self.reverse = reverse


def _dot_hilo_l(w_bf16, x):
    hi, lo = _split(x)
    return _dot(w_bf16, hi) + _dot(w_bf16, lo)


def _col(x, p):
    return x[:, p * PAIR:(p + 1) * PAIR]


def _wkv_kernel(rf_ref, kkf_ref, vf_ref, kdf_ref, kkaf_ref, lwf_ref,
                rb_ref, kkb_ref, vb_ref, kdb_ref, kkab_ref, lwb_ref,
                yf_ref, yb_ref, h_ref, *, n_sub):
    @pl.when(pl.program_id(1) == 0)
    def _():
        h_ref[...] = jnp.zeros_like(h_ref)

    dirs = (
        (False, rf_ref, kkf_ref, vf_ref, kdf_ref, kkaf_ref, lwf_ref, yf_ref),
        (True, rb_ref, kkb_ref, vb_ref, kdb_ref, kkab_ref, lwb_ref, yb_ref),
    )
    masks = (_Masks(False), _Masks(True))
    chunks = [(di, j) for di in range(2) for j in range(n_sub)]
    units = [(di, j, p) for (di, j) in chunks for p in range(N_PAIRS)]

    rb, ab, bt, kt, bh, kh, vv, etot = {}, {}, {}, {}, {}, {}, {}, {}
    for c in chunks:
        di, j = c
        reverse, r_ref, kk_ref, v_ref, kd_ref, kka_ref, lw_ref, _ = dirs[di]
        rs = slice(j * CHUNK, (j + 1) * CHUNK)
        lw = lw_ref[0, 0, rs, :]
        kd = kd_ref[0, 0, rs, :].astype(F32)
        kka = kka_ref[0, 0, rs, :].astype(F32)
        cum = _dot_hilo_l(masks[di].tri, lw)
        tot = cum[0:1] if reverse else cum[CHUNK - 1: CHUNK]
        rb[c] = r_ref[0, rs, :].astype(F32) * jnp.exp(cum)
        ab[c] = -kk_ref[0, rs, :].astype(F32) * jnp.exp(cum - lw)
        ip = jnp.exp(-cum)
        bt[c] = kka * ip
        kt[c] = kd * ip
        ph = jnp.exp(tot - cum)
        bh[c] = kka * ph
        kh[c] = kd * ph
        vv[c] = v_ref[0, rs, :]
        etot[c] = jnp.exp(tot)

    def per_unit(fn):
        return {u: fn(u, (u[0], u[1]), u[2], masks[u[0]]) for u in units}

    s_all = per_unit(lambda u, c, p, m: _dot_nt(
        jnp.concatenate([_col(ab[c], p), _col(rb[c], p)], axis=0).astype(BF16),
        jnp.concatenate([_stack2(_col(bt[c], p)), _stack2(_col(kt[c], p))], axis=0)))
    n_ab = per_unit(lambda u, c, p, m: jnp.where(m.strict, s_all[u][:CHUNK, :PAIR], 0.0))
    a_ak = per_unit(lambda u, c, p, m: jnp.where(m.strict, s_all[u][:CHUNK, PAIR:], 0.0).astype(BF16))
    m_rb = per_unit(lambda u, c, p, m: jnp.where(m.incl, s_all[u][CHUNK:, :PAIR], 0.0).astype(BF16))
    m_rk = per_unit(lambda u, c, p, m: jnp.where(m.incl, s_all[u][CHUNK:, PAIR:], 0.0).astype(BF16))

    nd = per_unit(lambda u, c, p, m: jnp.where(m.blk16, n_ab[u], 0.0))
    t_inv = per_unit(lambda u, c, p, m: jnp.where(m.eye, 1.0, nd[u]))
    pw = per_unit(lambda u, c, p, m: _pair_mm(nd[u], nd[u]))
    for step in range(3):
        t_inv = per_unit(lambda u, c, p, m: t_inv[u] + _pair_mm(t_inv[u], pw[u]))
        if step < 2:
            pw = per_unit(lambda u, c, p, m: _pair_mm(pw[u], pw[u]))
    for lvl in ("lvl32", "lvl64"):
        tc = per_unit(lambda u, c, p, m: _pair_mm(t_inv[u], jnp.where(getattr(m, lvl), n_ab[u], 0.0)))
        t_inv = per_unit(lambda u, c, p, m: t_inv[u] + _pair_mm(tc[u], t_inv[u]))

    v2 = per_unit(lambda u, c, p, m: _stack2(_col(vv[c], p)))
    akv = per_unit(lambda u, c, p, m: _dot(a_ak[u], v2[u]))
    wu = per_unit(lambda u, c, p, m: _dot(
        t_inv[u].astype(BF16), jnp.concatenate([_stack2(_col(ab[c], p)), _stack2(akv[u])], axis=1)))
    qy = per_unit(lambda u, c, p, m: _dot(
        m_rb[u], jnp.concatenate([_stack2(wu[u][:, :PAIR]), _stack2(wu[u][:, PAIR:])], axis=1)))
    q_hat = per_unit(lambda u, c, p, m: (_col(rb[c], p) + qy[u][:, :PAIR]).astype(BF16))
    y_loc = per_unit(lambda u, c, p, m: qy[u][:, PAIR:] + _dot(m_rk[u], v2[u]))
    gd = per_unit(lambda u, c, p, m: _dot(_col(bh[c], p).T.astype(BF16), wu[u].astype(BF16)))
    kv = per_unit(lambda u, c, p, m: _dot(_col(kh[c], p).T.astype(BF16), _col(vv[c], p).astype(BF16)))
    g_m = per_unit(lambda u, c, p, m: (jnp.where(m.bd, gd[u][:, :PAIR], 0.0)
                                       + jnp.where(m.eye_sq, _col(etot[c], p), 0.0)).astype(BF16))
    d_m = per_unit(lambda u, c, p, m: jnp.where(m.bd, gd[u][:, PAIR:] + kv[u], 0.0))

    h = {(di, p): h_ref[di, p] for di in range(2) for p in range(N_PAIRS)}
    for step in range(n_sub):
        for di in range(2):
            j = n_sub - 1 - step if dirs[di][0] else step
            y_ref = dirs[di][7]
            rs = slice(j * CHUNK, (j + 1) * CHUNK)
            for p in range(N_PAIRS):
                u = (di, j, p)
                hb = h[(di, p)].astype(BF16)
                y_ref[0, rs, p * PAIR:(p + 1) * PAIR] = (_dot(q_hat[u], hb) + y_loc[u]).astype(BF16)
                h[(di, p)] = _dot(g_m[u], hb) + d_m[u]
    for di in range(2):
        for p in range(N_PAIRS):
            h_ref[di, p] = h[(di, p)]


def _wkv_call(r, kkn, v, kd, kka, lw, tm):
    bn, t, _ = r.shape
    nt = t // tm
    fwd = pl.BlockSpec((1, tm, D_RWKV), lambda b, i: (b, i, 0))
    bwd = pl.BlockSpec((1, tm, D_RWKV), lambda b, i: (b, nt - 1 - i, 0))
    fwd2 = pl.BlockSpec((1, 1, tm, D_RWKV), lambda b, i: (0, b, i, 0))
    bwd2 = pl.BlockSpec((1, 1, tm, D_RWKV), lambda b, i: (1, b, nt - 1 - i, 0))
    s1 = jax.ShapeDtypeStruct((bn, t, D_RWKV), BF16)
    return pl.pallas_call(
        functools.partial(_wkv_kernel, n_sub=tm // CHUNK),
        grid=(bn, nt),
        in_specs=[fwd, fwd, fwd, fwd2, fwd2, fwd2, bwd, bwd, bwd, bwd2, bwd2, bwd2],
        out_specs=[fwd, bwd],
        out_shape=[s1, s1],
        scratch_shapes=[pltpu.VMEM((2, N_PAIRS, PAIR, PAIR), F32)],
        compiler_params=_cparams(("parallel", "arbitrary")),
        name="wkv7_chunked",
    )(r, kkn, v, kd, kka, lw, r, kkn, v, kd, kka, lw)


def _gelu_tanh(x):
    return 0.5 * x * (1.0 + jnp.tanh(0.7978845608028654 * (x + 0.044715 * (x * x * x))))


def _route(logits_t, b_col):
    rows = [logits_t[e:e + 1] for e in range(N_EXPERTS)]
    mx = functools.reduce(jnp.maximum, rows)
    ex = [jnp.exp(x - mx) for x in rows]
    den = functools.reduce(lambda a, b: a + b, ex)
    probs = [e / den for e in ex]
    sel = [probs[e] + b_col[e:e + 1] for e in range(N_EXPERTS)]
    scores = []
    for g in range(N_GROUPS):
        a, b, c, d = sel[4 * g: 4 * g + 4]
        hi1, lo1 = jnp.maximum(a, b), jnp.minimum(a, b)
        hi2, lo2 = jnp.maximum(c, d), jnp.minimum(c, d)
        scores.append(jnp.maximum(hi1, hi2) + jnp.maximum(jnp.minimum(hi1, hi2), jnp.maximum(lo1, lo2)))
    best = scores[0]
    bg = jnp.zeros_like(best)
    for g in range(1, N_GROUPS):
        upd = scores[g] > best
        best = jnp.where(upd, scores[g], best)
        bg = jnp.where(upd, float(g), bg)

    def pick(vals):
        out = []
        for j in range(EXP_PER_GROUP):
            x = vals[j]
            for g in range(1, N_GROUPS):
                x = jnp.where(bg == float(g), vals[4 * g + j], x)
            out.append(x)
        return out

    sg = pick(sel)
    pg = pick(probs)
    v1, i1 = sg[0], jnp.zeros_like(best)
    for j in range(1, EXP_PER_GROUP):
        upd = sg[j] > v1
        v1 = jnp.where(upd, sg[j], v1)
        i1 = jnp.where(upd, float(j), i1)
    neg = jnp.full_like(best, -jnp.inf)
    v2, i2 = neg, jnp.zeros_like(best)
    for j in range(EXP_PER_GROUP):
        cand = jnp.where(i1 == float(j), neg, sg[j])
        upd = cand > v2
        v2 = jnp.where(upd, cand, v2)
        i2 = jnp.where(upd, float(j), i2)
    p1 = jnp.zeros_like(best)
    p2 = jnp.zeros_like(best)
    for j in range(EXP_PER_GROUP):
        p1 = jnp.where(i1 == float(j), pg[j], p1)
        p2 = jnp.where(i2 == float(j), pg[j], p2)
    tot = p1 + p2
    return bg * float(EXP_PER_GROUP) + i1, bg * float(EXP_PER_GROUP) + i2, p1 / tot, p2 / tot


def _outproj_kernel(hl_ref, gate_ref, yf_ref, yb_ref, bonus_ref, g_ref, xa_ref, xb_ref, lnw_ref, lnb_ref, g1_ref,
                    sh2_ref, sc2_ref, n2_ref, wout_ref, wr_ref, br_ref, ones_ref,
                    xn_ref, h2_ref, route_ref, *, n_a):
    lru_out = (hl_ref[0, 0] + hl_ref[1, 0]) * _gelu_tanh(gate_ref[0])
    ones_bd = ones_ref[...]
    y = yf_ref[0].astype(F32) + yb_ref[0].astype(F32)
    mu = _dot_hilo(y, ones_bd) * (1.0 / RWKV_HEAD)
    yc = y - mu
    var = _dot_hilo(yc * yc, ones_bd) * (1.0 / RWKV_HEAD)
    gn = yc * lax.rsqrt(var + GN_EPS) * lnw_ref[...] + lnb_ref[...]
    rw_out = (gn + bonus_ref[0].astype(F32)) * g_ref[0].astype(F32)
    o = _dot(lru_out.astype(BF16), wout_ref[:D_LRU]) + _dot(rw_out.astype(BF16), wout_ref[D_LRU:])
    xn = _x_block(xa_ref, xb_ref, n_a) + g1_ref[0] * o
    xn_ref[0] = xn
    h2 = _norm_mod(xn, n2_ref[...], sc2_ref[0], sh2_ref[0])
    for j in range(TOK_ROWS):
        h2_ref[0, pl.ds(j, xn.shape[0], stride=TOK_ROWS), :] = h2[:, j * LANES:(j + 1) * LANES]
    logits_t = _dot3_nt(wr_ref[...], h2)
    e0, e1, g0, g1 = _route(logits_t, br_ref[...])
    zero = jnp.zeros_like(e0)
    route_ref[0, 0] = jnp.concatenate([e0, e1, g0, g1, zero, zero, zero, zero], axis=0)


def _outproj_call(hl, proj_lru, yf, yb, bonus, g, xa, xb, lnw, lnb, g1, sh2, sc2, n2, w_out_bf16, wr_t, br, ones_bd,
                  tm):
    n_a, t, d = xa.shape
    bn = g1.shape[0]
    nt = t // tm
    spec_a, spec_b = _x_specs(tm, d, n_a, nt)
    c2 = lambda b, i: (0, 0)
    tok = pl.BlockSpec((1, tm, D_RWKV), lambda b, i: (b, i, 0))
    tokd = pl.BlockSpec((1, tm, d), lambda b, i: (b, i, 0))
    per_b = pl.BlockSpec((1, 1, d), lambda b, i: (b, 0, 0))
    return pl.pallas_call(
        functools.partial(_outproj_kernel, n_a=n_a),
        grid=(bn, nt),
        in_specs=[
            pl.BlockSpec((2, 1, tm, D_LRU), lambda b, i: (0, b, i, 0)),
            pl.BlockSpec((1, tm, D_LRU), lambda b, i: (b, i, 1)),
            tok, tok, tok, tok, spec_a, spec_b,
            pl.BlockSpec((1, D_RWKV), c2),
            pl.BlockSpec((1, D_RWKV), c2),
            per_b, per_b, per_b,
            pl.BlockSpec((1, d), c2),
            pl.BlockSpec((d, d), c2),
            pl.BlockSpec((N_EXPERTS, d), c2),
            pl.BlockSpec((N_EXPERTS, 1), c2),
            pl.BlockSpec((D_RWKV, D_RWKV), c2),
        ],
        out_specs=[tokd, pl.BlockSpec((1, tm * TOK_ROWS, LANES), lambda b, i: (b, i, 0)),
                   pl.BlockSpec((1, 1, SUBLANES, tm), lambda b, i: (b, i, 0, 0))],
        out_shape=[
            jax.ShapeDtypeStruct((bn, t, d), F32),
            jax.ShapeDtypeStruct((bn, t * TOK_ROWS, LANES), F32),
            jax.ShapeDtypeStruct((bn, nt, SUBLANES, tm), F32),
        ],
        compiler_params=_cparams(("parallel", "parallel")),
        name="outproj_router",
    )(hl, proj_lru, yf, yb, bonus, g, xa, xb, lnw, lnb, g1, sh2, sc2, n2, w_out_bf16, wr_t, br, ones_bd)


def _row_copy(src_ref, src_row, dst_ref, dst_row, sem):
    src = src_ref.at[pl.ds(pl.multiple_of(src_row * TOK_ROWS, TOK_ROWS), TOK_ROWS)]
    dst = dst_ref.at[pl.ds(pl.multiple_of(dst_row * TOK_ROWS, TOK_ROWS), TOK_ROWS)]
    return pltpu.make_async_copy(src, dst, sem)


def _dispatch_kernel(pos_ref, h_ref, xb_in_ref, xb_ref, sem):
    del xb_in_ref
    tm = h_ref.shape[0] // TOK_ROWS

    def issue(r, carry):
        _row_copy(h_ref, r, xb_ref, pos_ref[0, 0, r], sem).start()
        _row_copy(h_ref, r, xb_ref, pos_ref[0, 0, tm + r], sem).start()
        return carry

    lax.fori_loop(0, tm, issue, 0)

    def drain(r, carry):
        _row_copy(h_ref, 0, xb_ref, 0, sem).wait()
        return carry

    lax.fori_loop(0, 2 * tm, drain, 0)


def _dispatch_call(h2, pos_tiles, xb_init, tm):
    n = h2.shape[0] // TOK_ROWS
    nt = n // tm
    return pl.pallas_call(
        _dispatch_kernel,
        grid=(nt,),
        in_specs=[
            pl.BlockSpec((1, 1, 2 * tm), lambda i: (i, 0, 0), memory_space=pltpu.SMEM),
            pl.BlockSpec((tm * TOK_ROWS, LANES), lambda i: (i, 0)),
            pl.BlockSpec(memory_space=pl.ANY),
        ],
        out_specs=pl.BlockSpec(memory_space=pl.ANY),
        out_shape=jax.ShapeDtypeStruct(xb_init.shape, F32),
        scratch_shapes=[pltpu.SemaphoreType.DMA(())],
        input_output_aliases={2: 0},
        compiler_params=_cparams(("arbitrary",)),
        name="moe_dispatch",
    )(pos_tiles, h2, xb_init)


def _expert_kernel(blk_e_ref, n_used_ref, x_ref, wg_ref, wu_ref, wd_ref, o_ref):
    i = pl.program_id(0)

    @pl.when(i < n_used_ref[0])
    def _():
        x = jnp.concatenate([x_ref[pl.ds(j, MOE_BLOCK, stride=TOK_ROWS), :] for j in range(TOK_ROWS)],
                            axis=1).astype(BF16)
        gate = _dot(x, wg_ref[0])
        hid = gate * _sigmoid(gate) * _dot(x, wu_ref[0])
        y = _dot(hid.astype(BF16), wd_ref[0])
        for j in range(TOK_ROWS):
            o_ref[pl.ds(j, MOE_BLOCK, stride=TOK_ROWS), :] = y[:, j * LANES:(j + 1) * LANES]

    @pl.when(i >= n_used_ref[0])
    def _():
        o_ref[...] = jnp.zeros_like(o_ref)


def _expert_call(xb, blk_e, n_used, wg, wu, wd):
    n_blk = xb.shape[0] // (MOE_BLOCK * TOK_ROWS)
    d = D_MODEL
    grid_spec = pltpu.PrefetchScalarGridSpec(
        num_scalar_prefetch=2,
        grid=(n_blk,),
        in_specs=[
            pl.BlockSpec((MOE_BLOCK * TOK_ROWS, LANES), lambda i, be, nu: (i, 0)),
            pl.BlockSpec((1, d, D_EXPERT), lambda i, be, nu: (be[i], 0, 0)),
            pl.BlockSpec((1, d, D_EXPERT), lambda i, be, nu: (be[i], 0, 0)),
            pl.BlockSpec((1, D_EXPERT, d), lambda i, be, nu: (be[i], 0, 0)),
        ],
        out_specs=pl.BlockSpec((MOE_BLOCK * TOK_ROWS, LANES), lambda i, be, nu: (i, 0)),
    )
    return pl.pallas_call(
        _expert_kernel,
        grid_spec=grid_spec,
        out_shape=jax.ShapeDtypeStruct(xb.shape, F32),
        compiler_params=_cparams(("arbitrary",)),
        name="moe_experts",
    )(blk_e, n_used, xb, wg, wu, wd)


def _combine_kernel(pos_ref, posn_ref, gates_ref, x_ref, g2_ref, nf_ref, yb_ref, *rest, n_first):
    *o_refs, buf_ref, sem = rest
    i = pl.program_id(0)
    n = pl.num_programs(0)
    tm = x_ref.shape[0]
    slot = i % 2

    def gather(p_ref, s):
        def issue(r, carry):
            _row_copy(yb_ref, p_ref[0, 0, r], buf_ref.at[s, 0], r, sem.at[s]).start()
            _row_copy(yb_ref, p_ref[0, 0, tm + r], buf_ref.at[s, 1], r, sem.at[s]).start()
            return carry

        lax.fori_loop(0, tm, issue, 0)

    @pl.when(i == 0)
    def _():
        gather(pos_ref, 0)

    @pl.when(i + 1 < n)
    def _():
        gather(posn_ref, 1 - slot)

    def drain(r, carry):
        _row_copy(yb_ref, 0, buf_ref.at[slot, 0], 0, sem.at[slot]).wait()
        return carry

    lax.fori_loop(0, 2 * tm, drain, 0)
    gts = gates_ref[...]
    g0 = gts[:, 0:1]
    g1 = gts[:, 1:2]
    for j in range(TOK_ROWS):
        rows = pl.ds(j, tm, stride=TOK_ROWS)
        cs = slice(j * LANES, (j + 1) * LANES)
        y = g0 * buf_ref[slot, 0, rows, :] + g1 * buf_ref[slot, 1, rows, :]
        o_refs[-1][:, cs] = x_ref[:, cs] + g2_ref[0, :, cs] * y
    if n_first is not None:
        o_first, o_second = o_refs
        xo = o_second[...]
        ms = jnp.mean(xo * xo, axis=-1, keepdims=True)
        res = xo * lax.rsqrt(ms + NORM_EPS) * nf_ref[...]
        o_second[...] = res

        @pl.when(i < n_first)
        def _():
            o_first[...] = res


def _combine_call(xn, yb, pos_tiles, gates, g2, nf, t, tm, n_first=None):
    n, d = xn.shape
    nt = n // tm
    per_b = t // tm
    if n_first is None:
        out_specs = pl.BlockSpec((tm, d), lambda i: (i, 0))
        out_shape = jax.ShapeDtypeStruct((n, d), F32)
    else:
        out_specs = [pl.BlockSpec((tm, d), lambda i: (jnp.minimum(i, n_first - 1), 0)),
                     pl.BlockSpec((tm, d), lambda i: (jnp.maximum(i - n_first, 0), 0))]
        out_shape = [jax.ShapeDtypeStruct((n_first * tm, d), F32), jax.ShapeDtypeStruct((n - n_first * tm, d), F32)]
    return pl.pallas_call(
        functools.partial(_combine_kernel, n_first=n_first),
        grid=(nt,),
        in_specs=[
            pl.BlockSpec((1, 1, 2 * tm), lambda i: (i, 0, 0), memory_space=pltpu.SMEM),
            pl.BlockSpec((1, 1, 2 * tm), lambda i: (jnp.minimum(i + 1, nt - 1), 0, 0), memory_space=pltpu.SMEM),
            pl.BlockSpec((tm, 2), lambda i: (i, 0)),
            pl.BlockSpec((tm, d), lambda i: (i, 0)),
            pl.BlockSpec((1, 1, d), lambda i: (i // per_b, 0, 0)),
            pl.BlockSpec((1, d), lambda i: (0, 0)),
            pl.BlockSpec(memory_space=pl.ANY),
        ],
        out_specs=out_specs,
        out_shape=out_shape,
        scratch_shapes=[pltpu.VMEM((2, 2, tm * TOK_ROWS, LANES), F32), pltpu.SemaphoreType.DMA((2,))],
        compiler_params=_cparams(("arbitrary",)),
        name="moe_combine",
    )(pos_tiles, pos_tiles, gates, xn, g2, nf, yb)


def _block_diag(w):
    h, a, b = w.shape
    eye = jnp.eye(h, dtype=w.dtype)
    return jnp.einsum("hab,hg->hagb", w, eye).reshape(h * a, h * b)


def _head_ones():
    idx = np.arange(D_RWKV) // RWKV_HEAD
    return jnp.asarray((idx[:, None] == idx[None, :]).astype(np.float32), dtype=BF16)


def _routing_tables(route, n_tok, tm):
    bn, nt = route.shape[0], route.shape[1]
    flat = jnp.transpose(route, (2, 0, 1, 3)).reshape(SUBLANES, n_tok)
    e0 = flat[0].astype(jnp.int32)
    e1 = flat[1].astype(jnp.int32)
    gates = jnp.stack([flat[2], flat[3]], axis=1)
    ar = jnp.arange(N_EXPERTS, dtype=jnp.int32)
    oh = (e0[:, None] == ar).astype(jnp.int32) + (e1[:, None] == ar).astype(jnp.int32)
    cs = jnp.cumsum(oh, axis=0)
    counts = cs[-1]
    excl = cs - oh
    padded = (counts + MOE_BLOCK - 1) // MOE_BLOCK * MOE_BLOCK
    pad_end = jnp.cumsum(padded)
    pad_start = pad_end - padded
    base = excl + pad_start[None, :]
    pos0 = jnp.take_along_axis(base, e0[:, None], axis=1)[:, 0]
    pos1 = jnp.take_along_axis(base, e1[:, None], axis=1)[:, 0]
    n_asg = 2 * n_tok
    n_blk = (n_asg + N_EXPERTS * (MOE_BLOCK - 1) + MOE_BLOCK - 1) // MOE_BLOCK
    blk_start = jnp.arange(n_blk, dtype=jnp.int32) * MOE_BLOCK
    blk_e = jnp.minimum(jnp.sum(pad_end[None, :] <= blk_start[:, None], axis=-1), N_EXPERTS - 1).astype(jnp.int32)
    n_used = (pad_end[-1] // MOE_BLOCK).astype(jnp.int32).reshape(1)
    pos_tiles = jnp.concatenate(
        [pos0.reshape(n_tok // tm, 1, tm), pos1.reshape(n_tok // tm, 1, tm)], axis=2).astype(jnp.int32)
    return pos_tiles, gates, blk_e, n_used, n_blk * MOE_BLOCK


def _tiles(t):
    want = dict(inproj=512, lru=512, prep=512, wkv=256, moe=512)
    return {k: min(t, v) for k, v in want.items()}


def _trunk(x_a, x_b, c, w_mod, b_mod, norm1, norm2, w_in, w_out, conv_w, conv_b, lru_wa, lru_ba, lru_wx, lru_bx, lru_lam,
           mu_prev, mu_next, rw_w0, rw_wup, rw_a0, rw_aup, rw_gup, rw_kk, rw_ka, rw_rk, ln_x_w, ln_x_b,
           w_router, b_router, exp_gate, exp_up, exp_down, norm_f):
    n_first, t, d = x_a.shape
    bn = n_first + x_b.shape[0]
    n_tok = bn * t
    depth = w_mod.shape[0]
    ones_bd = _head_ones()
    mod = _mod_call(c, w_mod, b_mod)
    wr_t = jnp.transpose(w_router)
    br = b_router.reshape(N_EXPERTS, 1)
    tiles = _tiles(t)
    tm_moe = tiles["moe"]
    xb = None

    for l in range(depth):
        sh1, sc1, g1, sh2, sc2, g2 = [m.reshape(bn, 1, d) for m in jnp.split(mod[l], 6, axis=-1)]
        proj_lru, proj_rw = _inproj_call(x_a, x_b, sh1, sc1, norm1[l].reshape(1, d), w_in[l].astype(BF16),
                                         tiles["inproj"])

        wg = jnp.stack([jnp.concatenate([_block_diag(lru_wa[l, dd]), _block_diag(lru_wx[l, dd])], axis=1)
                        for dd in range(2)]).astype(BF16)
        bg = jnp.concatenate([lru_ba[l], lru_bx[l]], axis=1).reshape(2, 1, 2 * D_LRU)
        sp = jax.nn.softplus(-lru_lam[l]).reshape(2, 1, D_LRU)
        hl = _lru_call(proj_lru, conv_w[l], conv_b[l].reshape(1, D_LRU), wg, bg, sp, tiles["lru"])

        zeros = jnp.zeros((64, D_RWKV), F32)
        wup_bd = jnp.concatenate([jnp.concatenate([rw_wup[l, 0], zeros], axis=1),
                                  jnp.concatenate([zeros, rw_wup[l, 1]], axis=1)], axis=0).astype(BF16)
        aup_bd = jnp.concatenate([jnp.concatenate([rw_aup[l, 0], zeros], axis=1),
                                  jnp.concatenate([zeros, rw_aup[l, 1]], axis=1)], axis=0).astype(BF16)
        r, kkn, v, kd, kka, lw, bonus, g = _rwprep_call(
            proj_rw, mu_prev[l].reshape(1, -1), mu_next[l].reshape(1, -1),
            rw_w0[l].reshape(1, -1), wup_bd, rw_a0[l].reshape(1, -1), aup_bd, rw_gup[l].astype(BF16),
            rw_kk[l].reshape(1, -1), rw_ka[l].reshape(1, -1), rw_rk[l].reshape(1, -1), ones_bd, tiles["prep"])
        yf, yb = _wkv_call(r, kkn, v, kd, kka, lw, tiles["wkv"])

        xn, h2, route = _outproj_call(
            hl, proj_lru, yf, yb, bonus, g, x_a, x_b, ln_x_w[l].reshape(1, -1), ln_x_b[l].reshape(1, -1),
            g1, sh2, sc2, norm2[l].reshape(1, d), w_out[l].astype(BF16), wr_t, br, ones_bd, tm_moe)

        pos_tiles, gates, blk_e, n_used, n_rows = _routing_tables(route, n_tok, tm_moe)
        xb_init = jnp.zeros((n_rows * TOK_ROWS, LANES), F32) if xb is None else xb
        xb = _dispatch_call(h2.reshape(n_tok * TOK_ROWS, LANES), pos_tiles, xb_init, tm_moe)
        ybuf = _expert_call(xb, blk_e, n_used, exp_gate[l].astype(BF16), exp_up[l].astype(BF16),
                            exp_down[l].astype(BF16))
        last = l == depth - 1
        out = _combine_call(xn.reshape(n_tok, d), ybuf, pos_tiles, gates, g2, norm_f.reshape(1, d), t, tm_moe,
                            n_first=n_first * (t // tm_moe) if last else None)
        if not last:
            x_a = x_b = out.reshape(bn, t, d)
    return out[0].reshape(n_first, t, d), out[1].reshape(bn - n_first, t, d)


def kernel(x_prompt, x_sample, c_prompt, c_sample, w_mod, b_mod, norm1, norm2, w_in, w_out, conv_w, conv_b, lru_wa, lru_ba, lru_wx, lru_bx, lru_lam, mu_prev, mu_next, rw_w0, rw_wup, rw_a0, rw_aup, rw_gup, rw_kk, rw_ka, rw_rk, ln_x_w, ln_x_b, w_router, b_router, exp_gate, exp_up, exp_down, norm_f):
    c = jnp.concatenate([c_prompt, c_sample], axis=0).astype(F32)
    return _trunk(x_prompt, x_sample, c, w_mod, b_mod, norm1, norm2, w_in, w_out, conv_w, conv_b, lru_wa, lru_ba,
                  lru_wx, lru_bx, lru_lam, mu_prev, mu_next, rw_w0, rw_wup, rw_a0, rw_aup, rw_gup, rw_kk, rw_ka,
                  rw_rk, ln_x_w, ln_x_b, w_router, b_router, exp_gate, exp_up, exp_down, norm_f)
```

```python
import functools

import jax
import jax.numpy as jnp
import numpy as np
from jax import lax
from jax.experimental import pallas as pl
from jax.experimental.pallas import tpu as pltpu

F32 = jnp.float32
BF16 = jnp.bfloat16

D_MODEL = 1024
D_LRU = 512
D_RWKV = 512
LRU_HEADS = 8
LRU_C = 8.0
RWKV_HEAD = 64
N_EXPERTS = 16
N_GROUPS = 4
EXP_PER_GROUP = 4
D_EXPERT = 512
MOE_BLOCK = 512
NORM_EPS = 1e-6
GN_EPS = 64e-5
D_RW_IN = 1920
D_IN = 2944

LANES = 128
SUBLANES = 8
CHUNK = 64
PAIR = 2 * RWKV_HEAD
N_PAIRS = D_RWKV // PAIR
TOK_ROWS = D_MODEL // LANES


def _cparams(sem, vmem_mb=48):
    return pltpu.CompilerParams(dimension_semantics=sem, vmem_limit_bytes=vmem_mb * 1024 * 1024)


def _dot(a, b):
    return jnp.dot(a, b, preferred_element_type=F32)


def _dot_nt(a, b):
    return lax.dot_general(a, b, (((1,), (1,)), ((), ())), preferred_element_type=F32)


def _split(x):
    hi = x.astype(BF16)
    lo = (x - hi.astype(F32)).astype(BF16)
    return hi, lo


def _dot_hilo(x, w_bf16):
    hi, lo = _split(x)
    return _dot(hi, w_bf16) + _dot(lo, w_bf16)


def _dot3(a, b):
    ah, al = _split(a)
    bh, bl = _split(b)
    return _dot(ah, bh) + (_dot(ah, bl) + _dot(al, bh))


def _dot3_nt(a, b):
    ah, al = _split(a)
    bh, bl = _split(b)
    return _dot_nt(ah, bh) + (_dot_nt(ah, bl) + _dot_nt(al, bh))


def _sigmoid(x):
    return 1.0 / (1.0 + jnp.exp(-x))


def _softplus(x):
    return jnp.maximum(x, 0.0) + jnp.log(1.0 + jnp.exp(-jnp.abs(x)))


def _rows(shape):
    return lax.broadcasted_iota(jnp.int32, shape, 0)


def _cols(shape):
    return lax.broadcasted_iota(jnp.int32, shape, 1)


def _mod_kernel(c_ref, w_ref, b_ref, o_ref):
    c = c_ref[...]
    cs = c * _sigmoid(c)
    o_ref[0] = _dot3(cs, w_ref[0]) + b_ref[0]


def _mod_call(c, w_mod, b_mod):
    nl, d, n6 = w_mod.shape
    bn = c.shape[0]
    tn = 1536
    return pl.pallas_call(
        _mod_kernel,
        grid=(nl, n6 // tn),
        in_specs=[
            pl.BlockSpec((bn, d), lambda l, j: (0, 0)),
            pl.BlockSpec((1, d, tn), lambda l, j: (l, 0, j)),
            pl.BlockSpec((1, 1, tn), lambda l, j: (l, 0, j)),
        ],
        out_specs=pl.BlockSpec((1, bn, tn), lambda l, j: (l, 0, j)),
        out_shape=jax.ShapeDtypeStruct((nl, bn, n6), F32),
        compiler_params=_cparams(("arbitrary", "arbitrary")),
        name="adaln_mod",
    )(c, w_mod, b_mod.reshape(nl, 1, n6))


def _norm_mod(x, nw, sc, sh):
    ms = jnp.mean(x * x, axis=-1, keepdims=True)
    return (x * lax.rsqrt(ms + NORM_EPS) * nw) * (1.0 + sc) + sh


def _x_specs(tm, d, n_a, nt):
    spec_a = pl.BlockSpec((1, tm, d), lambda b, i: (jnp.minimum(b, n_a - 1), jnp.where(b < n_a, i, nt - 1), 0))
    spec_b = pl.BlockSpec((1, tm, d), lambda b, i: (jnp.maximum(b - n_a, 0), jnp.where(b < n_a, 0, i), 0))
    return spec_a, spec_b


def _x_block(xa_ref, xb_ref, n_a):
    return jnp.where(pl.program_id(0) < n_a, xa_ref[0], xb_ref[0])


def _inproj_kernel(xa_ref, xb_ref, sh_ref, sc_ref, nw_ref, w_ref, lru_ref, rw_ref, *, n_a):
    h = _norm_mod(_x_block(xa_ref, xb_ref, n_a), nw_ref[...], sc_ref[0], sh_ref[0]).astype(BF16)
    lru_ref[0] = _dot(h, w_ref[:, : 2 * D_LRU])
    rw_ref[0] = _dot(h, w_ref[:, 2 * D_LRU:])


def _inproj_call(xa, xb, sh, sc, nw, w_in_bf16, tm):
    n_a, t, d = xa.shape
    bn = sh.shape[0]
    nt = t // tm
    spec_a, spec_b = _x_specs(tm, d, n_a, nt)
    return pl.pallas_call(
        functools.partial(_inproj_kernel, n_a=n_a),
        grid=(bn, nt),
        in_specs=[
            spec_a, spec_b,
            pl.BlockSpec((1, 1, d), lambda b, i: (b, 0, 0)),
            pl.BlockSpec((1, 1, d), lambda b, i: (b, 0, 0)),
            pl.BlockSpec((1, d), lambda b, i: (0, 0)),
            pl.BlockSpec((d, D_IN), lambda b, i: (0, 0)),
        ],
        out_specs=[
            pl.BlockSpec((1, tm, 2 * D_LRU), lambda b, i: (b, i, 0)),
            pl.BlockSpec((1, tm, D_RW_IN), lambda b, i: (b, i, 0)),
        ],
        out_shape=[
            jax.ShapeDtypeStruct((bn, t, 2 * D_LRU), F32),
            jax.ShapeDtypeStruct((bn, t, D_RW_IN), F32),
        ],
        compiler_params=_cparams(("parallel", "parallel")),
        name="norm_inproj",
    )(xa, xb, sh, sc, nw, w_in_bf16)


def _lru_kernel(x_ref, prev_ref, next_ref, cw_ref, cb_ref, wg_ref, bg_ref, sp_ref, o_ref,
                carry_ref, a_scr, b_scr, *, n_tiles):
    d = pl.program_id(1)
    i = pl.program_id(2)
    ti = d * (n_tiles - 1) + (1 - 2 * d) * i
    tm = x_ref.shape[1]

    @pl.when(i == 0)
    def _():
        carry_ref[...] = jnp.zeros_like(carry_ref)

    x = x_ref[0]
    prev8 = jnp.where(ti == 0, 0.0, prev_ref[0])
    next8 = jnp.where(ti == n_tiles - 1, 0.0, next_ref[0])
    r = _rows(x.shape)
    xm1 = jnp.where(r == 0, prev8[7:8], pltpu.roll(x, 1, 0))
    xm2 = jnp.where(r == 0, prev8[6:7], jnp.where(r == 1, prev8[7:8], pltpu.roll(x, 2, 0)))
    xp1 = jnp.where(r == tm - 1, next8[0:1], pltpu.roll(x, tm - 1, 0))
    cw = cw_ref[...]
    u = cw[0:1] * xm2 + cw[1:2] * xm1 + cw[2:3] * x + cw[3:4] * xp1 + cb_ref[...]

    gates = _dot(u.astype(BF16), wg_ref[0]) + bg_ref[0]
    rg = _sigmoid(gates[:, :D_LRU])
    ig = _sigmoid(gates[:, D_LRU:])
    log_a = (-LRU_C) * rg * sp_ref[0]
    a = jnp.exp(log_a)
    bv = jnp.sqrt(1.0 - jnp.exp(2.0 * log_a)) * (ig * u)

    n_lt = D_LRU // LANES
    for c in range(n_lt):
        a_scr[c] = a[:, c * LANES:(c + 1) * LANES]
        b_scr[c] = bv[:, c * LANES:(c + 1) * LANES]
    win = SUBLANES * SUBLANES

    def scan(reverse):
        r8 = _rows((SUBLANES, D_LRU))
        carry = carry_ref[...]
        n_win = tm // win
        for w in (range(n_win - 1, -1, -1) if reverse else range(n_win)):
            base = w * win
            hs = [jnp.zeros((SUBLANES, LANES), F32)] * n_lt
            accs = [jnp.ones((SUBLANES, LANES), F32)] * n_lt
            for g in (range(SUBLANES - 1, -1, -1) if reverse else range(SUBLANES)):
                rows = pl.ds(base + g, SUBLANES, stride=SUBLANES)
                for c in range(n_lt):
                    ag = a_scr[c, rows, :]
                    hs[c] = ag * hs[c] + b_scr[c, rows, :]
                    accs[c] = ag * accs[c]
                    b_scr[c, rows, :] = hs[c]
                    a_scr[c, rows, :] = accs[c]
            h = jnp.concatenate(hs, axis=1)
            acc = jnp.concatenate(accs, axis=1)
            s = 1
            while s < SUBLANES:
                keep = (r8 >= SUBLANES - s) if reverse else (r8 < s)
                sh = SUBLANES - s if reverse else s
                h = acc * jnp.where(keep, 0.0, pltpu.roll(h, sh, 0)) + h
                acc = acc * jnp.where(keep, 1.0, pltpu.roll(acc, sh, 0))
                s *= 2
            ends = h + acc * carry
            if reverse:
                enter = jnp.where(r8 == SUBLANES - 1, carry, pltpu.roll(ends, SUBLANES - 1, 0))
                carry = ends[0:1]
            else:
                enter = jnp.where(r8 == 0, carry, pltpu.roll(ends, 1, 0))
                carry = ends[SUBLANES - 1: SUBLANES]
            for r in range(SUBLANES):
                rs = slice(base + r * SUBLANES, base + (r + 1) * SUBLANES)
                for c in range(n_lt):
                    cs = slice(c * LANES, (c + 1) * LANES)
                    o_ref[0, 0, rs, cs] = b_scr[c, rs, :] + a_scr[c, rs, :] * enter[r:r + 1, cs]
        carry_ref[...] = carry

    @pl.when(d == 0)
    def _():
        scan(False)

    @pl.when(d == 1)
    def _():
        scan(True)


def _lru_call(proj_lru, conv_w, conv_b, wg, bg, sp, tm):
    bn, t, _ = proj_lru.shape
    nt = t // tm
    r8 = tm // SUBLANES
    n8 = t // SUBLANES

    def tile(d, i):
        return d * (nt - 1) + (1 - 2 * d) * i

    return pl.pallas_call(
        functools.partial(_lru_kernel, n_tiles=nt),
        grid=(bn, 2, nt),
        in_specs=[
            pl.BlockSpec((1, tm, D_LRU), lambda b, d, i: (b, tile(d, i), 0)),
            pl.BlockSpec((1, SUBLANES, D_LRU), lambda b, d, i: (b, jnp.maximum(tile(d, i) * r8 - 1, 0), 0)),
            pl.BlockSpec((1, SUBLANES, D_LRU), lambda b, d, i: (b, jnp.minimum((tile(d, i) + 1) * r8, n8 - 1), 0)),
            pl.BlockSpec((4, D_LRU), lambda b, d, i: (0, 0)),
            pl.BlockSpec((1, D_LRU), lambda b, d, i: (0, 0)),
            pl.BlockSpec((1, D_LRU, 2 * D_LRU), lambda b, d, i: (d, 0, 0)),
            pl.BlockSpec((1, 1, 2 * D_LRU), lambda b, d, i: (d, 0, 0)),
            pl.BlockSpec((1, 1, D_LRU), lambda b, d, i: (d, 0, 0)),
        ],
        out_specs=pl.BlockSpec((1, 1, tm, D_LRU), lambda b, d, i: (d, b, tile(d, i), 0)),
        out_shape=jax.ShapeDtypeStruct((2, bn, t, D_LRU), F32),
        scratch_shapes=[pltpu.VMEM((1, D_LRU), F32),
                        pltpu.VMEM((D_LRU // LANES, tm, LANES), F32),
                        pltpu.VMEM((D_LRU // LANES, tm, LANES), F32)],
        compiler_params=_cparams(("parallel", "arbitrary", "arbitrary")),
        name="rg_lru",
    )(proj_lru, proj_lru, proj_lru, conv_w, conv_b, wg, bg, sp)


def _rwprep_kernel(z_ref, prev_ref, next_ref, mup_ref, mun_ref, w0_ref, wup_ref, a0_ref, aup_ref, gup_ref,
                   kk_ref, ka_ref, rk_ref, ones_ref,
                   r_out, kkn_out, v_out, kd_out, kka_out, lw_out, bonus_out, g_out, *, n_tiles):
    i = pl.program_id(1)
    tm = z_ref.shape[1]
    zc = z_ref[0]
    prev8 = jnp.where(i == 0, 0.0, prev_ref[0])
    next8 = jnp.where(i == n_tiles - 1, 0.0, next_ref[0])
    rr = _rows(zc.shape)
    zp = jnp.where(rr == 0, prev8[7:8], pltpu.roll(zc, 1, 0))
    zn = jnp.where(rr == tm - 1, next8[0:1], pltpu.roll(zc, tm - 1, 0))
    z = zc + mup_ref[...] * (zp - zc) + mun_ref[...] * (zn - zc)

    r = z[:, 0:512]
    k = z[:, 512:1024]
    v = z[:, 1024:1536]
    wd = z[:, 1536:1664]
    ad = z[:, 1664:1792]
    gd = z[:, 1792:1920]
    ones_bd = ones_ref[...]

    kkr = k * kk_ref[...]
    ss = _dot_hilo(kkr * kkr, ones_bd)
    kkn = kkr / jnp.maximum(jnp.sqrt(ss), 1e-12)

    wlin = w0_ref[...] + _dot(jnp.tanh(wd).astype(BF16), wup_ref[...])
    lw = -jnp.exp(-_softplus(-wlin) - 0.5)
    a = _sigmoid(a0_ref[...] + _dot(ad.astype(BF16), aup_ref[...]))

    ka = ka_ref[...]
    k_sum = jnp.zeros_like(k)
    for d in range(2):
        a_d = a[:, d * D_RWKV:(d + 1) * D_RWKV]
        kd = k * (1.0 + (a_d - 1.0) * ka)
        kd_out[d, 0] = kd.astype(BF16)
        kka_out[d, 0] = (kkn * a_d).astype(BF16)
        lw_out[d, 0] = lw[:, d * D_RWKV:(d + 1) * D_RWKV]
        k_sum = k_sum + kd
    r_out[0] = r.astype(BF16)
    kkn_out[0] = kkn.astype(BF16)
    v_out[0] = v.astype(BF16)
    bonus_out[0] = (_dot_hilo(r * k_sum * rk_ref[...], ones_bd) * v).astype(BF16)
    g_out[0] = _dot(_sigmoid(gd).astype(BF16), gup_ref[...]).astype(BF16)


def _rwprep_call(proj_rw, mup, mun, w0, wup_bd, a0, aup_bd, gup, k_k, k_a, r_k, ones_bd, tm):
    bn, t, _ = proj_rw.shape
    nt = t // tm
    r8 = tm // SUBLANES
    n8 = t // SUBLANES
    c2 = lambda b, i: (0, 0)
    tok = pl.BlockSpec((1, tm, D_RWKV), lambda b, i: (b, i, 0))
    tok2 = pl.BlockSpec((2, 1, tm, D_RWKV), lambda b, i: (0, b, i, 0))
    s1 = jax.ShapeDtypeStruct((bn, t, D_RWKV), BF16)
    s2 = jax.ShapeDtypeStruct((2, bn, t, D_RWKV), BF16)
    s2f = jax.ShapeDtypeStruct((2, bn, t, D_RWKV), F32)
    return pl.pallas_call(
        functools.partial(_rwprep_kernel, n_tiles=nt),
        grid=(bn, nt),
        in_specs=[
            pl.BlockSpec((1, tm, D_RW_IN), lambda b, i: (b, i, 0)),
            pl.BlockSpec((1, SUBLANES, D_RW_IN), lambda b, i: (b, jnp.maximum(i * r8 - 1, 0), 0)),
            pl.BlockSpec((1, SUBLANES, D_RW_IN), lambda b, i: (b, jnp.minimum((i + 1) * r8, n8 - 1), 0)),
            pl.BlockSpec((1, D_RW_IN), c2),
            pl.BlockSpec((1, D_RW_IN), c2),
            pl.BlockSpec((1, 2 * D_RWKV), c2),
            pl.BlockSpec((LANES, 2 * D_RWKV), c2),
            pl.BlockSpec((1, 2 * D_RWKV), c2),
            pl.BlockSpec((LANES, 2 * D_RWKV), c2),
            pl.BlockSpec((LANES, D_RWKV), c2),
            pl.BlockSpec((1, D_RWKV), c2),
            pl.BlockSpec((1, D_RWKV), c2),
            pl.BlockSpec((1, D_RWKV), c2),
            pl.BlockSpec((D_RWKV, D_RWKV), c2),
        ],
        out_specs=[tok, tok, tok, tok2, tok2, tok2, tok, tok],
        out_shape=[s1, s1, s1, s2, s2, s2f, s1, s1],
        compiler_params=_cparams(("parallel", "parallel")),
        name="rwkv_prep",
    )(proj_rw, proj_rw, proj_rw, mup, mun, w0, wup_bd, a0, aup_bd, gup, k_k, k_a, r_k, ones_bd)


def _stack2(y):
    yb = y.astype(BF16)
    lo = _cols(yb.shape) < RWKV_HEAD
    zero = jnp.zeros_like(yb)
    return jnp.concatenate([jnp.where(lo, yb, zero), jnp.where(lo, zero, yb)], axis=0)


def _pair_mm(x, y):
    return _dot(x.astype(BF16), _stack2(y))


class _Masks:
    def __init__(self, reverse):
        shp = (CHUNK, PAIR)
        t = _rows(shp)
        s = _cols(shp) % CHUNK
        self.strict = (s > t) if reverse else (s < t)
        self.incl = (s >= t) if reverse else (s <= t)
        self.eye = s == t
        self.blk16 = (t // 16) == (s // 16)
        self.lvl32 = ((t // 32) == (s // 32)) & ((t // 16) != (s // 16))
        self.lvl64 = (t // 32) != (s // 32)
        sq = (PAIR, PAIR)
        self.bd = (_rows(sq) // RWKV_HEAD) == (_cols(sq) // RWKV_HEAD)
        self.eye_sq = _rows(sq) == _cols(sq)
        tt = _rows((CHUNK, CHUNK))
        ss = _cols((CHUNK, CHUNK))
        self.tri = jnp.where((ss >= tt) if reverse else (ss <= tt), 1.0, 0.0).astype(BF16)
        self.reverse = reverse


def _dot_hilo_l(w_bf16, x):
    hi, lo = _split(x)
    return _dot(w_bf16, hi) + _dot(w_bf16, lo)


def _col(x, p):
    return x[:, p * PAIR:(p + 1) * PAIR]


def _wkv_kernel(rf_ref, kkf_ref, vf_ref, kdf_ref, kkaf_ref, lwf_ref,
                rb_ref, kkb_ref, vb_ref, kdb_ref, kkab_ref, lwb_ref,
                yf_ref, yb_ref, h_ref, *, n_sub):
    @pl.when(pl.program_id(1) == 0)
    def _():
        h_ref[...] = jnp.zeros_like(h_ref)

    dirs = (
        (False, rf_ref, kkf_ref, vf_ref, kdf_ref, kkaf_ref, lwf_ref, yf_ref),
        (True, rb_ref, kkb_ref, vb_ref, kdb_ref, kkab_ref, lwb_ref, yb_ref),
    )
    masks = (_Masks(False), _Masks(True))
    chunks = [(di, j) for di in range(2) for j in range(n_sub)]
    units = [(di, j, p) for (di, j) in chunks for p in range(N_PAIRS)]

    rb, ab, bt, kt, bh, kh, vv, etot = {}, {}, {}, {}, {}, {}, {}, {}
    for c in chunks:
        di, j = c
        reverse, r_ref, kk_ref, v_ref, kd_ref, kka_ref, lw_ref, _ = dirs[di]
        rs = slice(j * CHUNK, (j + 1) * CHUNK)
        lw = lw_ref[0, 0, rs, :]
        kd = kd_ref[0, 0, rs, :].astype(F32)
        kka = kka_ref[0, 0, rs, :].astype(F32)
        cum = _dot_hilo_l(masks[di].tri, lw)
        tot = cum[0:1] if reverse else cum[CHUNK - 1: CHUNK]
        rb[c] = r_ref[0, rs, :].astype(F32) * jnp.exp(cum)
        ab[c] = -kk_ref[0, rs, :].astype(F32) * jnp.exp(cum - lw)
        ip = jnp.exp(-cum)
        bt[c] = kka * ip
        kt[c] = kd * ip
        ph = jnp.exp(tot - cum)
        bh[c] = kka * ph
        kh[c] = kd * ph
        vv[c] = v_ref[0, rs, :]
        etot[c] = jnp.exp(tot)

    def per_unit(fn):
        return {u: fn(u, (u[0], u[1]), u[2], masks[u[0]]) for u in units}

    s_all = per_unit(lambda u, c, p, m: _dot_nt(
        jnp.concatenate([_col(ab[c], p), _col(rb[c], p)], axis=0).astype(BF16),
        jnp.concatenate([_stack2(_col(bt[c], p)), _stack2(_col(kt[c], p))], axis=0)))
    n_ab = per_unit(lambda u, c, p, m: jnp.where(m.strict, s_all[u][:CHUNK, :PAIR], 0.0))
    a_ak = per_unit(lambda u, c, p, m: jnp.where(m.strict, s_all[u][:CHUNK, PAIR:], 0.0).astype(BF16))
    m_rb = per_unit(lambda u, c, p, m: jnp.where(m.incl, s_all[u][CHUNK:, :PAIR], 0.0).astype(BF16))
    m_rk = per_unit(lambda u, c, p, m: jnp.where(m.incl, s_all[u][CHUNK:, PAIR:], 0.0).astype(BF16))

    nd = per_unit(lambda u, c, p, m: jnp.where(m.blk16, n_ab[u], 0.0))
    t_inv = per_unit(lambda u, c, p, m: jnp.where(m.eye, 1.0, nd[u]))
    pw = per_unit(lambda u, c, p, m: _pair_mm(nd[u], nd[u]))
    for step in range(3):
        t_inv = per_unit(lambda u, c, p, m: t_inv[u] + _pair_mm(t_inv[u], pw[u]))
        if step < 2:
            pw = per_unit(lambda u, c, p, m: _pair_mm(pw[u], pw[u]))
    for lvl in ("lvl32", "lvl64"):
        tc = per_unit(lambda u, c, p, m: _pair_mm(t_inv[u], jnp.where(getattr(m, lvl), n_ab[u], 0.0)))
        t_inv = per_unit(lambda u, c, p, m: t_inv[u] + _pair_mm(tc[u], t_inv[u]))

    v2 = per_unit(lambda u, c, p, m: _stack2(_col(vv[c], p)))
    akv = per_unit(lambda u, c, p, m: _dot(a_ak[u], v2[u]))
    wu = per_unit(lambda u, c, p, m: _dot(
        t_inv[u].astype(BF16), jnp.concatenate([_stack2(_col(ab[c], p)), _stack2(akv[u])], axis=1)))
    qy = per_unit(lambda u, c, p, m: _dot(
        m_rb[u], jnp.concatenate([_stack2(wu[u][:, :PAIR]), _stack2(wu[u][:, PAIR:])], axis=1)))
    q_hat = per_unit(lambda u, c, p, m: (_col(rb[c], p) + qy[u][:, :PAIR]).astype(BF16))
    y_loc = per_unit(lambda u, c, p, m: qy[u][:, PAIR:] + _dot(m_rk[u], v2[u]))
    gd = per_unit(lambda u, c, p, m: _dot(_col(bh[c], p).T.astype(BF16), wu[u].astype(BF16)))
    kv = per_unit(lambda u, c, p, m: _dot(_col(kh[c], p).T.astype(BF16), _col(vv[c], p).astype(BF16)))
    g_m = per_unit(lambda u, c, p, m: (jnp.where(m.bd, gd[u][:, :PAIR], 0.0)
                                       + jnp.where(m.eye_sq, _col(etot[c], p), 0.0)).astype(BF16))
    d_m = per_unit(lambda u, c, p, m: jnp.where(m.bd, gd[u][:, PAIR:] + kv[u], 0.0))

    h = {(di, p): h_ref[di, p] for di in range(2) for p in range(N_PAIRS)}
    for step in range(n_sub):
        for di in range(2):
            j = n_sub - 1 - step if dirs[di][0] else step
            y_ref = dirs[di][7]
            rs = slice(j * CHUNK, (j + 1) * CHUNK)
            for p in range(N_PAIRS):
                u = (di, j, p)
                hb = h[(di, p)].astype(BF16)
                y_ref[0, rs, p * PAIR:(p + 1) * PAIR] = (_dot(q_hat[u], hb) + y_loc[u]).astype(BF16)
                h[(di, p)] = _dot(g_m[u], hb) + d_m[u]
    for di in range(2):
        for p in range(N_PAIRS):
            h_ref[di, p] = h[(di, p)]


def _wkv_call(r, kkn, v, kd, kka, lw, tm):
    bn, t, _ = r.shape
    nt = t // tm
    fwd = pl.BlockSpec((1, tm, D_RWKV), lambda b, i: (b, i, 0))
    bwd = pl.BlockSpec((1, tm, D_RWKV), lambda b, i: (b, nt - 1 - i, 0))
    fwd2 = pl.BlockSpec((1, 1, tm, D_RWKV), lambda b, i: (0, b, i, 0))
    bwd2 = pl.BlockSpec((1, 1, tm, D_RWKV), lambda b, i: (1, b, nt - 1 - i, 0))
    s1 = jax.ShapeDtypeStruct((bn, t, D_RWKV), BF16)
    return pl.pallas_call(
        functools.partial(_wkv_kernel, n_sub=tm // CHUNK),
        grid=(bn, nt),
        in_specs=[fwd, fwd, fwd, fwd2, fwd2, fwd2, bwd, bwd, bwd, bwd2, bwd2, bwd2],
        out_specs=[fwd, bwd],
        out_shape=[s1, s1],
        scratch_shapes=[pltpu.VMEM((2, N_PAIRS, PAIR, PAIR), F32)],
        compiler_params=_cparams(("parallel", "arbitrary")),
        name="wkv7_chunked",
    )(r, kkn, v, kd, kka, lw, r, kkn, v, kd, kka, lw)


def _gelu_tanh(x):
    return 0.5 * x * (1.0 + jnp.tanh(0.7978845608028654 * (x + 0.044715 * (x * x * x))))


def _route(logits_t, b_col):
    rows = [logits_t[e:e + 1] for e in range(N_EXPERTS)]
    mx = functools.reduce(jnp.maximum, rows)
    ex = [jnp.exp(x - mx) for x in rows]
    den = functools.reduce(lambda a, b: a + b, ex)
    probs = [e / den for e in ex]
    sel = [probs[e] + b_col[e:e + 1] for e in range(N_EXPERTS)]
    scores = []
    for g in range(N_GROUPS):
        a, b, c, d = sel[4 * g: 4 * g + 4]
        hi1, lo1 = jnp.maximum(a, b), jnp.minimum(a, b)
        hi2, lo2 = jnp.maximum(c, d), jnp.minimum(c, d)
        scores.append(jnp.maximum(hi1, hi2) + jnp.maximum(jnp.minimum(hi1, hi2), jnp.maximum(lo1, lo2)))
    best = scores[0]
    bg = jnp.zeros_like(best)
    for g in range(1, N_GROUPS):
        upd = scores[g] > best
        best = jnp.where(upd, scores[g], best)
        bg = jnp.where(upd, float(g), bg)

    def pick(vals):
        out = []
        for j in range(EXP_PER_GROUP):
            x = vals[j]
            for g in range(1, N_GROUPS):
                x = jnp.where(bg == float(g), vals[4 * g + j], x)
            out.append(x)
        return out

    sg = pick(sel)
    v1, i1 = sg[0], jnp.zeros_like(best)
    for j in range(1, EXP_PER_GROUP):
        upd = sg[j] > v1
        v1 = jnp.where(upd, sg[j], v1)
        i1 = jnp.where(upd, float(j), i1)
    neg = jnp.full_like(best, -jnp.inf)
    v2, i2 = neg, jnp.zeros_like(best)
    for j in range(EXP_PER_GROUP):
        cand = jnp.where(i1 == float(j), neg, sg[j])
        upd = cand > v2
        v2 = jnp.where(upd, cand, v2)
        i2 = jnp.where(upd, float(j), i2)
    return bg * float(EXP_PER_GROUP) + i1, bg * float(EXP_PER_GROUP) + i2


def _outproj_kernel(hl_ref, gate_ref, yf_ref, yb_ref, bonus_ref, g_ref, xa_ref, xb_ref, lnw_ref, lnb_ref, g1_ref,
                    sh2_ref, sc2_ref, n2_ref, wout_ref, wr_ref, br_ref, ones_ref,
                    xn_ref, h2_ref, route_ref, *, n_a):
    lru_out = (hl_ref[0, 0] + hl_ref[1, 0]) * _gelu_tanh(gate_ref[0])
    ones_bd = ones_ref[...]
    y = yf_ref[0].astype(F32) + yb_ref[0].astype(F32)
    mu = _dot_hilo(y, ones_bd) * (1.0 / RWKV_HEAD)
    yc = y - mu
    var = _dot_hilo(yc * yc, ones_bd) * (1.0 / RWKV_HEAD)
    gn = yc * lax.rsqrt(var + GN_EPS) * lnw_ref[...] + lnb_ref[...]
    rw_out = (gn + bonus_ref[0].astype(F32)) * g_ref[0].astype(F32)
    o = _dot(lru_out.astype(BF16), wout_ref[:D_LRU]) + _dot(rw_out.astype(BF16), wout_ref[D_LRU:])
    xn = _x_block(xa_ref, xb_ref, n_a) + g1_ref[0] * o
    xn_ref[0] = xn
    h2 = _norm_mod(xn, n2_ref[...], sc2_ref[0], sh2_ref[0])
    for j in range(TOK_ROWS):
        h2_ref[0, pl.ds(j, xn.shape[0], stride=TOK_ROWS), :] = h2[:, j * LANES:(j + 1) * LANES]
    logits_t = _dot3_nt(wr_ref[...], h2)
    e0, e1 = _route(logits_t, br_ref[...])
    zero = jnp.zeros_like(e0)
    route_ref[0, 0] = jnp.concatenate([e0, e1, zero, zero, zero, zero, zero, zero], axis=0)


def _outproj_call(hl, proj_lru, yf, yb, bonus, g, xa, xb, lnw, lnb, g1, sh2, sc2, n2, w_out_bf16, wr_t, br, ones_bd,
                  tm):
    n_a, t, d = xa.shape
    bn = g1.shape[0]
    nt = t // tm
    spec_a, spec_b = _x_specs(tm, d, n_a, nt)
    c2 = lambda b, i: (0, 0)
    tok = pl.BlockSpec((1, tm, D_RWKV), lambda b, i: (b, i, 0))
    tokd = pl.BlockSpec((1, tm, d), lambda b, i: (b, i, 0))
    per_b = pl.BlockSpec((1, 1, d), lambda b, i: (b, 0, 0))
    return pl.pallas_call(
        functools.partial(_outproj_kernel, n_a=n_a),
        grid=(bn, nt),
        in_specs=[
            pl.BlockSpec((2, 1, tm, D_LRU), lambda b, i: (0, b, i, 0)),
            pl.BlockSpec((1, tm, D_LRU), lambda b, i: (b, i, 1)),
            tok, tok, tok, tok, spec_a, spec_b,
            pl.BlockSpec((1, D_RWKV), c2),
            pl.BlockSpec((1, D_RWKV), c2),
            per_b, per_b, per_b,
            pl.BlockSpec((1, d), c2),
            pl.BlockSpec((d, d), c2),
            pl.BlockSpec((N_EXPERTS, d), c2),
            pl.BlockSpec((N_EXPERTS, 1), c2),
            pl.BlockSpec((D_RWKV, D_RWKV), c2),
        ],
        out_specs=[tokd, pl.BlockSpec((1, tm * TOK_ROWS, LANES), lambda b, i: (b, i, 0)),
                   pl.BlockSpec((1, 1, SUBLANES, tm), lambda b, i: (b, i, 0, 0))],
        out_shape=[
            jax.ShapeDtypeStruct((bn, t, d), F32),
            jax.ShapeDtypeStruct((bn, t * TOK_ROWS, LANES), F32),
            jax.ShapeDtypeStruct((bn, nt, SUBLANES, tm), F32),
        ],
        compiler_params=_cparams(("parallel", "parallel")),
        name="outproj_router",
    )(hl, proj_lru, yf, yb, bonus, g, xa, xb, lnw, lnb, g1, sh2, sc2, n2, w_out_bf16, wr_t, br, ones_bd)


def _row_copy(src_ref, src_row, dst_ref, dst_row, sem):
    src = src_ref.at[pl.ds(pl.multiple_of(src_row * TOK_ROWS, TOK_ROWS), TOK_ROWS)]
    dst = dst_ref.at[pl.ds(pl.multiple_of(dst_row * TOK_ROWS, TOK_ROWS), TOK_ROWS)]
    return pltpu.make_async_copy(src, dst, sem)


def _dispatch_kernel(pos_ref, h_ref, xb_in_ref, xb_ref, sem):
    del xb_in_ref
    tm = h_ref.shape[0] // TOK_ROWS

    def issue(r, carry):
        _row_copy(h_ref, r, xb_ref, pos_ref[0, 0, r], sem).start()
        return carry

    lax.fori_loop(0, tm, issue, 0)

    def drain(r, carry):
        _row_copy(h_ref, 0, xb_ref, 0, sem).wait()
        return carry

    lax.fori_loop(0, tm, drain, 0)


def _dispatch_call(h2, pos_tiles, xb_init, tm):
    n = h2.shape[0] // TOK_ROWS
    nt = n // tm
    return pl.pallas_call(
        _dispatch_kernel,
        grid=(nt,),
        in_specs=[
            pl.BlockSpec((1, 1, tm), lambda i: (i, 0, 0), memory_space=pltpu.SMEM),
            pl.BlockSpec((tm * TOK_ROWS, LANES), lambda i: (i, 0)),
            pl.BlockSpec(memory_space=pl.ANY),
        ],
        out_specs=pl.BlockSpec(memory_space=pl.ANY),
        out_shape=jax.ShapeDtypeStruct(xb_init.shape, F32),
        scratch_shapes=[pltpu.SemaphoreType.DMA(())],
        input_output_aliases={2: 0},
        compiler_params=_cparams(("arbitrary",)),
        name="moe_dispatch",
    )(pos_tiles, h2, xb_init)


def _expert_kernel(blk_a_ref, blk_b_ref, n_used_ref, x_ref, wr_ref,
                   wga_ref, wua_ref, wda_ref, wgb_ref, wub_ref, wdb_ref, o_ref):
    i = pl.program_id(0)

    @pl.when(i < n_used_ref[0])
    def _():
        x32 = jnp.concatenate([x_ref[pl.ds(j, MOE_BLOCK, stride=TOK_ROWS), :] for j in range(TOK_ROWS)], axis=1)
        x = x32.astype(BF16)
        w_diff = wr_ref[pl.ds(blk_a_ref[i], 1), :] - wr_ref[pl.ds(blk_b_ref[i], 1), :]
        l_diff = jnp.sum(x32 * w_diff, axis=-1, keepdims=True)
        g_a = 1.0 / (1.0 + jnp.exp(-l_diff))
        g_b = 1.0 / (1.0 + jnp.exp(l_diff))

        def ffn(wg_ref, wu_ref, wd_ref):
            gate = _dot(x, wg_ref[0])
            hid = gate * _sigmoid(gate) * _dot(x, wu_ref[0])
            return _dot(hid.astype(BF16), wd_ref[0])

        y = g_a * ffn(wga_ref, wua_ref, wda_ref) + g_b * ffn(wgb_ref, wub_ref, wdb_ref)
        for j in range(TOK_ROWS):
            o_ref[pl.ds(j, MOE_BLOCK, stride=TOK_ROWS), :] = y[:, j * LANES:(j + 1) * LANES]

    @pl.when(i >= n_used_ref[0])
    def _():
        o_ref[...] = jnp.zeros_like(o_ref)


def _expert_call(xb, blk_a, blk_b, n_used, wr_t, wg, wu, wd):
    n_blk = xb.shape[0] // (MOE_BLOCK * TOK_ROWS)
    d = D_MODEL
    w_in_a = pl.BlockSpec((1, d, D_EXPERT), lambda i, ba, bb, nu: (ba[i], 0, 0))
    w_in_b = pl.BlockSpec((1, d, D_EXPERT), lambda i, ba, bb, nu: (bb[i], 0, 0))
    grid_spec = pltpu.PrefetchScalarGridSpec(
        num_scalar_prefetch=3,
        grid=(n_blk,),
        in_specs=[
            pl.BlockSpec((MOE_BLOCK * TOK_ROWS, LANES), lambda i, ba, bb, nu: (i, 0)),
            pl.BlockSpec((N_EXPERTS, d), lambda i, ba, bb, nu: (0, 0)),
            w_in_a, w_in_a,
            pl.BlockSpec((1, D_EXPERT, d), lambda i, ba, bb, nu: (ba[i], 0, 0)),
            w_in_b, w_in_b,
            pl.BlockSpec((1, D_EXPERT, d), lambda i, ba, bb, nu: (bb[i], 0, 0)),
        ],
        out_specs=pl.BlockSpec((MOE_BLOCK * TOK_ROWS, LANES), lambda i, ba, bb, nu: (i, 0)),
    )
    return pl.pallas_call(
        _expert_kernel,
        grid_spec=grid_spec,
        out_shape=jax.ShapeDtypeStruct(xb.shape, F32),
        compiler_params=_cparams(("arbitrary",)),
        name="moe_experts",
    )(blk_a, blk_b, n_used, xb, wr_t, wg, wu, wd, wg, wu, wd)


def _combine_kernel(pos_ref, posn_ref, x_ref, g2_ref, nf_ref, yb_ref, *rest, n_first):
    *o_refs, buf_ref, sem = rest
    i = pl.program_id(0)
    n = pl.num_programs(0)
    tm = x_ref.shape[0]
    slot = i % 2

    def gather(p_ref, s):
        def issue(r, carry):
            _row_copy(yb_ref, p_ref[0, 0, r], buf_ref.at[s], r, sem.at[s]).start()
            return carry

        lax.fori_loop(0, tm, issue, 0)

    @pl.when(i == 0)
    def _():
        gather(pos_ref, 0)

    @pl.when(i + 1 < n)
    def _():
        gather(posn_ref, 1 - slot)

    def drain(r, carry):
        _row_copy(yb_ref, 0, buf_ref.at[slot], 0, sem.at[slot]).wait()
        return carry

    lax.fori_loop(0, tm, drain, 0)
    for j in range(TOK_ROWS):
        cs = slice(j * LANES, (j + 1) * LANES)
        y = buf_ref[slot, pl.ds(j, tm, stride=TOK_ROWS), :]
        o_refs[-1][:, cs] = x_ref[:, cs] + g2_ref[0, :, cs] * y
    if n_first is not None:
        o_first, o_second = o_refs
        xo = o_second[...]
        ms = jnp.mean(xo * xo, axis=-1, keepdims=True)
        res = xo * lax.rsqrt(ms + NORM_EPS) * nf_ref[...]
        o_second[...] = res

        @pl.when(i < n_first)
        def _():
            o_first[...] = res


def _combine_call(xn, yb, pos_tiles, g2, nf, t, tm, n_first=None):
    n, d = xn.shape
    nt = n // tm
    per_b = t // tm
    if n_first is None:
        out_specs = pl.BlockSpec((tm, d), lambda i: (i, 0))
        out_shape = jax.ShapeDtypeStruct((n, d), F32)
    else:
        out_specs = [pl.BlockSpec((tm, d), lambda i: (jnp.minimum(i, n_first - 1), 0)),
                     pl.BlockSpec((tm, d), lambda i: (jnp.maximum(i - n_first, 0), 0))]
        out_shape = [jax.ShapeDtypeStruct((n_first * tm, d), F32), jax.ShapeDtypeStruct((n - n_first * tm, d), F32)]
    return pl.pallas_call(
        functools.partial(_combine_kernel, n_first=n_first),
        grid=(nt,),
        in_specs=[
            pl.BlockSpec((1, 1, tm), lambda i: (i, 0, 0), memory_space=pltpu.SMEM),
            pl.BlockSpec((1, 1, tm), lambda i: (jnp.minimum(i + 1, nt - 1), 0, 0), memory_space=pltpu.SMEM),
            pl.BlockSpec((tm, d), lambda i: (i, 0)),
            pl.BlockSpec((1, 1, d), lambda i: (i // per_b, 0, 0)),
            pl.BlockSpec((1, d), lambda i: (0, 0)),
            pl.BlockSpec(memory_space=pl.ANY),
        ],
        out_specs=out_specs,
        out_shape=out_shape,
        scratch_shapes=[pltpu.VMEM((2, tm * TOK_ROWS, LANES), F32), pltpu.SemaphoreType.DMA((2,))],
        compiler_params=_cparams(("arbitrary",)),
        name="moe_combine",
    )(pos_tiles, pos_tiles, xn, g2, nf, yb)


def _block_diag(w):
    h, a, b = w.shape
    eye = jnp.eye(h, dtype=w.dtype)
    return jnp.einsum("hab,hg->hagb", w, eye).reshape(h * a, h * b)


def _head_ones():
    idx = np.arange(D_RWKV) // RWKV_HEAD
    return jnp.asarray((idx[:, None] == idx[None, :]).astype(np.float32), dtype=BF16)


_PAIRS = [(i, j) for i in range(EXP_PER_GROUP) for j in range(i + 1, EXP_PER_GROUP)]
N_CLASSES = N_GROUPS * len(_PAIRS)


def _routing_tables(route, n_tok, tm):
    flat = jnp.transpose(route, (2, 0, 1, 3)).reshape(SUBLANES, n_tok)
    e0 = flat[0].astype(jnp.int32)
    e1 = flat[1].astype(jnp.int32)
    ea = jnp.minimum(e0, e1)
    eb = jnp.maximum(e0, e1)
    i = ea % EXP_PER_GROUP
    j = eb % EXP_PER_GROUP
    cls = (ea // EXP_PER_GROUP) * len(_PAIRS) + (i * (2 * EXP_PER_GROUP - 1 - i)) // 2 + (j - i - 1)
    ar = jnp.arange(N_CLASSES, dtype=jnp.int32)
    oh = (cls[:, None] == ar).astype(jnp.int32)
    cs = jnp.cumsum(oh, axis=0)
    counts = cs[-1]
    padded = (counts + MOE_BLOCK - 1) // MOE_BLOCK * MOE_BLOCK
    pad_end = jnp.cumsum(padded)
    base = (cs - oh) + (pad_end - padded)[None, :]
    pos = jnp.take_along_axis(base, cls[:, None], axis=1)[:, 0]
    n_blk = (n_tok + N_CLASSES * (MOE_BLOCK - 1) + MOE_BLOCK - 1) // MOE_BLOCK
    blk_start = jnp.arange(n_blk, dtype=jnp.int32) * MOE_BLOCK
    blk_cls = jnp.minimum(jnp.sum(pad_end[None, :] <= blk_start[:, None], axis=-1), N_CLASSES - 1)
    cls_a = np.array([g * EXP_PER_GROUP + p[0] for g in range(N_GROUPS) for p in _PAIRS], np.int32)
    cls_b = np.array([g * EXP_PER_GROUP + p[1] for g in range(N_GROUPS) for p in _PAIRS], np.int32)
    blk_a = jnp.asarray(cls_a)[blk_cls]
    blk_b = jnp.asarray(cls_b)[blk_cls]
    n_used = (pad_end[-1] // MOE_BLOCK).astype(jnp.int32).reshape(1)
    pos_tiles = pos.reshape(n_tok // tm, 1, tm).astype(jnp.int32)
    return pos_tiles, blk_a, blk_b, n_used, n_blk * MOE_BLOCK


def _tiles(t):
    want = dict(inproj=512, lru=512, prep=512, wkv=256, moe=512)
    return {k: min(t, v) for k, v in want.items()}


def _trunk(x_a, x_b, c, w_mod, b_mod, norm1, norm2, w_in, w_out, conv_w, conv_b, lru_wa, lru_ba, lru_wx, lru_bx, lru_lam,
           mu_prev, mu_next, rw_w0, rw_wup, rw_a0, rw_aup, rw_gup, rw_kk, rw_ka, rw_rk, ln_x_w, ln_x_b,
           w_router, b_router, exp_gate, exp_up, exp_down, norm_f):
    n_first, t, d = x_a.shape
    bn = n_first + x_b.shape[0]
    n_tok = bn * t
    depth = w_mod.shape[0]
    ones_bd = _head_ones()
    mod = _mod_call(c, w_mod, b_mod)
    wr_t = jnp.transpose(w_router)
    br = b_router.reshape(N_EXPERTS, 1)
    tiles = _tiles(t)
    tm_moe = tiles["moe"]
    xb = None

    for l in range(depth):
        sh1, sc1, g1, sh2, sc2, g2 = [m.reshape(bn, 1, d) for m in jnp.split(mod[l], 6, axis=-1)]
        proj_lru, proj_rw = _inproj_call(x_a, x_b, sh1, sc1, norm1[l].reshape(1, d), w_in[l].astype(BF16),
                                         tiles["inproj"])

        wg = jnp.stack([jnp.concatenate([_block_diag(lru_wa[l, dd]), _block_diag(lru_wx[l, dd])], axis=1)
                        for dd in range(2)]).astype(BF16)
        bg = jnp.concatenate([lru_ba[l], lru_bx[l]], axis=1).reshape(2, 1, 2 * D_LRU)
        sp = jax.nn.softplus(-lru_lam[l]).reshape(2, 1, D_LRU)
        hl = _lru_call(proj_lru, conv_w[l], conv_b[l].reshape(1, D_LRU), wg, bg, sp, tiles["lru"])

        zeros = jnp.zeros((64, D_RWKV), F32)
        wup_bd = jnp.concatenate([jnp.concatenate([rw_wup[l, 0], zeros], axis=1),
                                  jnp.concatenate([zeros, rw_wup[l, 1]], axis=1)], axis=0).astype(BF16)
        aup_bd = jnp.concatenate([jnp.concatenate([rw_aup[l, 0], zeros], axis=1),
                                  jnp.concatenate([zeros, rw_aup[l, 1]], axis=1)], axis=0).astype(BF16)
        r, kkn, v, kd, kka, lw, bonus, g = _rwprep_call(
            proj_rw, mu_prev[l].reshape(1, -1), mu_next[l].reshape(1, -1),
            rw_w0[l].reshape(1, -1), wup_bd, rw_a0[l].reshape(1, -1), aup_bd, rw_gup[l].astype(BF16),
            rw_kk[l].reshape(1, -1), rw_ka[l].reshape(1, -1), rw_rk[l].reshape(1, -1), ones_bd, tiles["prep"])
        yf, yb = _wkv_call(r, kkn, v, kd, kka, lw, tiles["wkv"])

        xn, h2, route = _outproj_call(
            hl, proj_lru, yf, yb, bonus, g, x_a, x_b, ln_x_w[l].reshape(1, -1), ln_x_b[l].reshape(1, -1),
            g1, sh2, sc2, norm2[l].reshape(1, d), w_out[l].astype(BF16), wr_t, br, ones_bd, tm_moe)

        pos_tiles, blk_a, blk_b, n_used, n_rows = _routing_tables(route, n_tok, tm_moe)
        xb_init = jnp.zeros((n_rows * TOK_ROWS, LANES), F32) if xb is None else xb
        xb = _dispatch_call(h2.reshape(n_tok * TOK_ROWS, LANES), pos_tiles, xb_init, tm_moe)
        ybuf = _expert_call(xb, blk_a, blk_b, n_used, wr_t, exp_gate[l].astype(BF16), exp_up[l].astype(BF16),
                            exp_down[l].astype(BF16))
        last = l == depth - 1
        out = _combine_call(xn.reshape(n_tok, d), ybuf, pos_tiles, g2, norm_f.reshape(1, d), t, tm_moe,
                            n_first=n_first * (t // tm_moe) if last else None)
        if not last:
            x_a = x_b = out.reshape(bn, t, d)
    return out[0].reshape(n_first, t, d), out[1].reshape(bn - n_first, t, d)


def kernel(x_prompt, x_sample, c_prompt, c_sample, w_mod, b_mod, norm1, norm2, w_in, w_out, conv_w, conv_b, lru_wa, lru_ba, lru_wx, lru_bx, lru_lam, mu_prev, mu_next, rw_w0, rw_wup, rw_a0, rw_aup, rw_gup, rw_kk, rw_ka, rw_rk, ln_x_w, ln_x_b, w_router, b_router, exp_gate, exp_up, exp_down, norm_f):
    c = jnp.concatenate([c_prompt, c_sample], axis=0).astype(F32)
    return _trunk(x_prompt, x_sample, c, w_mod, b_mod, norm1, norm2, w_in, w_out, conv_w, conv_b, lru_wa, lru_ba,
                  lru_wx, lru_bx, lru_lam, mu_prev, mu_next, rw_w0, rw_wup, rw_a0, rw_aup, rw_gup, rw_kk, rw_ka,
                  rw_rk, ln_x_w, ln_x_b, w_router, b_router, exp_gate, exp_up, exp_down, norm_f)
```

```python
import functools

import jax
import jax.numpy as jnp
import numpy as np
from jax import lax
from jax.experimental import pallas as pl
from jax.experimental.pallas import tpu as pltpu

F32 = jnp.float32
BF16 = jnp.bfloat16

D_MODEL = 1024
D_LRU = 512
D_RWKV = 512
LRU_HEADS = 8
LRU_C = 8.0
RWKV_HEAD = 64
N_EXPERTS = 16
N_GROUPS = 4
EXP_PER_GROUP = 4
D_EXPERT = 512
MOE_BLOCK = 512
NORM_EPS = 1e-6
GN_EPS = 64e-5
D_RW_IN = 1920
D_IN = 2944

LANES = 128
SUBLANES = 8
CHUNK = 64
PAIR = 2 * RWKV_HEAD
N_PAIRS = D_RWKV // PAIR
TOK_ROWS = D_MODEL // LANES


def _cparams(sem, vmem_mb=48):
    return pltpu.CompilerParams(dimension_semantics=sem, vmem_limit_bytes=vmem_mb * 1024 * 1024)


def _dot(a, b):
    return jnp.dot(a, b, preferred_element_type=F32)


def _dot_nt(a, b):
    return lax.dot_general(a, b, (((1,), (1,)), ((), ())), preferred_element_type=F32)


def _split(x):
    hi = x.astype(BF16)
    lo = (x - hi.astype(F32)).astype(BF16)
    return hi, lo


def _dot_hilo(x, w_bf16):
    hi, lo = _split(x)
    return _dot(hi, w_bf16) + _dot(lo, w_bf16)


def _dot3(a, b):
    ah, al = _split(a)
    bh, bl = _split(b)
    return _dot(ah, bh) + (_dot(ah, bl) + _dot(al, bh))


def _dot3_nt(a, b):
    ah, al = _split(a)
    bh, bl = _split(b)
    return _dot_nt(ah, bh) + (_dot_nt(ah, bl) + _dot_nt(al, bh))


def _sigmoid(x):
    return 1.0 / (1.0 + jnp.exp(-x))


def _softplus(x):
    return jnp.maximum(x, 0.0) + jnp.log(1.0 + jnp.exp(-jnp.abs(x)))


def _rows(shape):
    return lax.broadcasted_iota(jnp.int32, shape, 0)


def _cols(shape):
    return lax.broadcasted_iota(jnp.int32, shape, 1)


def _mod_kernel(c_ref, w_ref, b_ref, o_ref):
    c = c_ref[...]
    cs = c * _sigmoid(c)
    o_ref[0] = _dot3(cs, w_ref[0]) + b_ref[0]


def _mod_call(c, w_mod, b_mod):
    nl, d, n6 = w_mod.shape
    bn = c.shape[0]
    tn = 1536
    return pl.pallas_call(
        _mod_kernel,
        grid=(nl, n6 // tn),
        in_specs=[
            pl.BlockSpec((bn, d), lambda l, j: (0, 0)),
            pl.BlockSpec((1, d, tn), lambda l, j: (l, 0, j)),
            pl.BlockSpec((1, 1, tn), lambda l, j: (l, 0, j)),
        ],
        out_specs=pl.BlockSpec((1, bn, tn), lambda l, j: (l, 0, j)),
        out_shape=jax.ShapeDtypeStruct((nl, bn, n6), F32),
        compiler_params=_cparams(("arbitrary", "arbitrary")),
        name="adaln_mod",
    )(c, w_mod, b_mod.reshape(nl, 1, n6))


def _norm_mod(x, nw, sc, sh):
    ms = jnp.mean(x * x, axis=-1, keepdims=True)
    return (x * lax.rsqrt(ms + NORM_EPS) * nw) * (1.0 + sc) + sh


def _x_specs(tm, d, n_a, nt):
    spec_a = pl.BlockSpec((1, tm, d), lambda b, i: (jnp.minimum(b, n_a - 1), jnp.where(b < n_a, i, nt - 1), 0))
    spec_b = pl.BlockSpec((1, tm, d), lambda b, i: (jnp.maximum(b - n_a, 0), jnp.where(b < n_a, 0, i), 0))
    return spec_a, spec_b


def _x_block(xa_ref, xb_ref, n_a):
    return jnp.where(pl.program_id(0) < n_a, xa_ref[0], xb_ref[0])


def _inproj_kernel(xa_ref, xb_ref, sh_ref, sc_ref, nw_ref, w_ref, lru_ref, rw_ref, *, n_a):
    h = _norm_mod(_x_block(xa_ref, xb_ref, n_a), nw_ref[...], sc_ref[0], sh_ref[0]).astype(BF16)
    lru_ref[0] = _dot(h, w_ref[:, : 2 * D_LRU])
    rw_ref[0] = _dot(h, w_ref[:, 2 * D_LRU:])


def _inproj_call(xa, xb, sh, sc, nw, w_in_bf16, tm):
    n_a, t, d = xa.shape
    bn = sh.shape[0]
    nt = t // tm
    spec_a, spec_b = _x_specs(tm, d, n_a, nt)
    return pl.pallas_call(
        functools.partial(_inproj_kernel, n_a=n_a),
        grid=(bn, nt),
        in_specs=[
            spec_a, spec_b,
            pl.BlockSpec((1, 1, d), lambda b, i: (b, 0, 0)),
            pl.BlockSpec((1, 1, d), lambda b, i: (b, 0, 0)),
            pl.BlockSpec((1, d), lambda b, i: (0, 0)),
            pl.BlockSpec((d, D_IN), lambda b, i: (0, 0)),
        ],
        out_specs=[
            pl.BlockSpec((1, tm, 2 * D_LRU), lambda b, i: (b, i, 0)),
            pl.BlockSpec((1, tm, D_RW_IN), lambda b, i: (b, i, 0)),
        ],
        out_shape=[
            jax.ShapeDtypeStruct((bn, t, 2 * D_LRU), F32),
            jax.ShapeDtypeStruct((bn, t, D_RW_IN), F32),
        ],
        compiler_params=_cparams(("parallel", "parallel")),
        name="norm_inproj",
    )(xa, xb, sh, sc, nw, w_in_bf16)


def _lru_kernel(x_ref, prev_ref, next_ref, cw_ref, cb_ref, wg_ref, bg_ref, sp_ref, o_ref,
                carry_ref, a_scr, b_scr, *, n_tiles):
    d = pl.program_id(1)
    i = pl.program_id(2)
    ti = d * (n_tiles - 1) + (1 - 2 * d) * i
    tm = x_ref.shape[1]

    @pl.when(i == 0)
    def _():
        carry_ref[...] = jnp.zeros_like(carry_ref)

    x = x_ref[0]
    prev8 = jnp.where(ti == 0, 0.0, prev_ref[0])
    next8 = jnp.where(ti == n_tiles - 1, 0.0, next_ref[0])
    r = _rows(x.shape)
    xm1 = jnp.where(r == 0, prev8[7:8], pltpu.roll(x, 1, 0))
    xm2 = jnp.where(r == 0, prev8[6:7], jnp.where(r == 1, prev8[7:8], pltpu.roll(x, 2, 0)))
    xp1 = jnp.where(r == tm - 1, next8[0:1], pltpu.roll(x, tm - 1, 0))
    cw = cw_ref[...]
    u = cw[0:1] * xm2 + cw[1:2] * xm1 + cw[2:3] * x + cw[3:4] * xp1 + cb_ref[...]

    gates = _dot(u.astype(BF16), wg_ref[0]) + bg_ref[0]
    rg = _sigmoid(gates[:, :D_LRU])
    ig = _sigmoid(gates[:, D_LRU:])
    log_a = (-LRU_C) * rg * sp_ref[0]
    a = jnp.exp(log_a)
    bv = jnp.sqrt(1.0 - jnp.exp(2.0 * log_a)) * (ig * u)

    n_lt = D_LRU // LANES
    for c in range(n_lt):
        a_scr[c] = a[:, c * LANES:(c + 1) * LANES]
        b_scr[c] = bv[:, c * LANES:(c + 1) * LANES]
    win = SUBLANES * SUBLANES

    def scan(reverse):
        r8 = _rows((SUBLANES, D_LRU))
        carry = carry_ref[...]
        n_win = tm // win
        for w in (range(n_win - 1, -1, -1) if reverse else range(n_win)):
            base = w * win
            hs = [jnp.zeros((SUBLANES, LANES), F32)] * n_lt
            accs = [jnp.ones((SUBLANES, LANES), F32)] * n_lt
            for g in (range(SUBLANES - 1, -1, -1) if reverse else range(SUBLANES)):
                rows = pl.ds(base + g, SUBLANES, stride=SUBLANES)
                for c in range(n_lt):
                    ag = a_scr[c, rows, :]
                    hs[c] = ag * hs[c] + b_scr[c, rows, :]
                    accs[c] = ag * accs[c]
                    b_scr[c, rows, :] = hs[c]
                    a_scr[c, rows, :] = accs[c]
            h = jnp.concatenate(hs, axis=1)
            acc = jnp.concatenate(accs, axis=1)
            s = 1
            while s < SUBLANES:
                keep = (r8 >= SUBLANES - s) if reverse else (r8 < s)
                sh = SUBLANES - s if reverse else s
                h = acc * jnp.where(keep, 0.0, pltpu.roll(h, sh, 0)) + h
                acc = acc * jnp.where(keep, 1.0, pltpu.roll(acc, sh, 0))
                s *= 2
            ends = h + acc * carry
            if reverse:
                enter = jnp.where(r8 == SUBLANES - 1, carry, pltpu.roll(ends, SUBLANES - 1, 0))
                carry = ends[0:1]
            else:
                enter = jnp.where(r8 == 0, carry, pltpu.roll(ends, 1, 0))
                carry = ends[SUBLANES - 1: SUBLANES]
            for r in range(SUBLANES):
                rs = slice(base + r * SUBLANES, base + (r + 1) * SUBLANES)
                for c in range(n_lt):
                    cs = slice(c * LANES, (c + 1) * LANES)
                    o_ref[0, 0, rs, cs] = b_scr[c, rs, :] + a_scr[c, rs, :] * enter[r:r + 1, cs]
        carry_ref[...] = carry

    @pl.when(d == 0)
    def _():
        scan(False)

    @pl.when(d == 1)
    def _():
        scan(True)


def _lru_call(proj_lru, conv_w, conv_b, wg, bg, sp, tm):
    bn, t, _ = proj_lru.shape
    nt = t // tm
    r8 = tm // SUBLANES
    n8 = t // SUBLANES

    def tile(d, i):
        return d * (nt - 1) + (1 - 2 * d) * i

    return pl.pallas_call(
        functools.partial(_lru_kernel, n_tiles=nt),
        grid=(bn, 2, nt),
        in_specs=[
            pl.BlockSpec((1, tm, D_LRU), lambda b, d, i: (b, tile(d, i), 0)),
            pl.BlockSpec((1, SUBLANES, D_LRU), lambda b, d, i: (b, jnp.maximum(tile(d, i) * r8 - 1, 0), 0)),
            pl.BlockSpec((1, SUBLANES, D_LRU), lambda b, d, i: (b, jnp.minimum((tile(d, i) + 1) * r8, n8 - 1), 0)),
            pl.BlockSpec((4, D_LRU), lambda b, d, i: (0, 0)),
            pl.BlockSpec((1, D_LRU), lambda b, d, i: (0, 0)),
            pl.BlockSpec((1, D_LRU, 2 * D_LRU), lambda b, d, i: (d, 0, 0)),
            pl.BlockSpec((1, 1, 2 * D_LRU), lambda b, d, i: (d, 0, 0)),
            pl.BlockSpec((1, 1, D_LRU), lambda b, d, i: (d, 0, 0)),
        ],
        out_specs=pl.BlockSpec((1, 1, tm, D_LRU), lambda b, d, i: (d, b, tile(d, i), 0)),
        out_shape=jax.ShapeDtypeStruct((2, bn, t, D_LRU), F32),
        scratch_shapes=[pltpu.VMEM((1, D_LRU), F32),
                        pltpu.VMEM((D_LRU // LANES, tm, LANES), F32),
                        pltpu.VMEM((D_LRU // LANES, tm, LANES), F32)],
        compiler_params=_cparams(("parallel", "arbitrary", "arbitrary")),
        name="rg_lru",
    )(proj_lru, proj_lru, proj_lru, conv_w, conv_b, wg, bg, sp)


def _rwprep_kernel(z_ref, prev_ref, next_ref, mup_ref, mun_ref, w0_ref, wup_ref, a0_ref, aup_ref, gup_ref,
                   kk_ref, ka_ref, rk_ref, ones_ref,
                   r_out, kkn_out, v_out, kd_out, kka_out, lw_out, bonus_out, g_out, *, n_tiles):
    i = pl.program_id(1)
    tm = z_ref.shape[1]
    zc = z_ref[0]
    prev8 = jnp.where(i == 0, 0.0, prev_ref[0])
    next8 = jnp.where(i == n_tiles - 1, 0.0, next_ref[0])
    rr = _rows(zc.shape)
    zp = jnp.where(rr == 0, prev8[7:8], pltpu.roll(zc, 1, 0))
    zn = jnp.where(rr == tm - 1, next8[0:1], pltpu.roll(zc, tm - 1, 0))
    z = zc + mup_ref[...] * (zp - zc) + mun_ref[...] * (zn - zc)

    r = z[:, 0:512]
    k = z[:, 512:1024]
    v = z[:, 1024:1536]
    wd = z[:, 1536:1664]
    ad = z[:, 1664:1792]
    gd = z[:, 1792:1920]
    ones_bd = ones_ref[...]

    kkr = k * kk_ref[...]
    ss = _dot_hilo(kkr * kkr, ones_bd)
    kkn = kkr / jnp.maximum(jnp.sqrt(ss), 1e-12)

    wlin = w0_ref[...] + _dot(jnp.tanh(wd).astype(BF16), wup_ref[...])
    lw = -jnp.exp(-_softplus(-wlin) - 0.5)
    a = _sigmoid(a0_ref[...] + _dot(ad.astype(BF16), aup_ref[...]))

    ka = ka_ref[...]
    k_sum = jnp.zeros_like(k)
    for d in range(2):
        a_d = a[:, d * D_RWKV:(d + 1) * D_RWKV]
        kd = k * (1.0 + (a_d - 1.0) * ka)
        kd_out[d, 0] = kd.astype(BF16)
        kka_out[d, 0] = (kkn * a_d).astype(BF16)
        lw_out[d, 0] = lw[:, d * D_RWKV:(d + 1) * D_RWKV]
        k_sum = k_sum + kd
    r_out[0] = r.astype(BF16)
    kkn_out[0] = kkn.astype(BF16)
    v_out[0] = v.astype(BF16)
    bonus_out[0] = (_dot_hilo(r * k_sum * rk_ref[...], ones_bd) * v).astype(BF16)
    g_out[0] = _dot(_sigmoid(gd).astype(BF16), gup_ref[...]).astype(BF16)


def _rwprep_call(proj_rw, mup, mun, w0, wup_bd, a0, aup_bd, gup, k_k, k_a, r_k, ones_bd, tm):
    bn, t, _ = proj_rw.shape
    nt = t // tm
    r8 = tm // SUBLANES
    n8 = t // SUBLANES
    c2 = lambda b, i: (0, 0)
    tok = pl.BlockSpec((1, tm, D_RWKV), lambda b, i: (b, i, 0))
    tok2 = pl.BlockSpec((2, 1, tm, D_RWKV), lambda b, i: (0, b, i, 0))
    s1 = jax.ShapeDtypeStruct((bn, t, D_RWKV), BF16)
    s2 = jax.ShapeDtypeStruct((2, bn, t, D_RWKV), BF16)
    s2f = jax.ShapeDtypeStruct((2, bn, t, D_RWKV), F32)
    return pl.pallas_call(
        functools.partial(_rwprep_kernel, n_tiles=nt),
        grid=(bn, nt),
        in_specs=[
            pl.BlockSpec((1, tm, D_RW_IN), lambda b, i: (b, i, 0)),
            pl.BlockSpec((1, SUBLANES, D_RW_IN), lambda b, i: (b, jnp.maximum(i * r8 - 1, 0), 0)),
            pl.BlockSpec((1, SUBLANES, D_RW_IN), lambda b, i: (b, jnp.minimum((i + 1) * r8, n8 - 1), 0)),
            pl.BlockSpec((1, D_RW_IN), c2),
            pl.BlockSpec((1, D_RW_IN), c2),
            pl.BlockSpec((1, 2 * D_RWKV), c2),
            pl.BlockSpec((LANES, 2 * D_RWKV), c2),
            pl.BlockSpec((1, 2 * D_RWKV), c2),
            pl.BlockSpec((LANES, 2 * D_RWKV), c2),
            pl.BlockSpec((LANES, D_RWKV), c2),
            pl.BlockSpec((1, D_RWKV), c2),
            pl.BlockSpec((1, D_RWKV), c2),
            pl.BlockSpec((1, D_RWKV), c2),
            pl.BlockSpec((D_RWKV, D_RWKV), c2),
        ],
        out_specs=[tok, tok, tok, tok2, tok2, tok2, tok, tok],
        out_shape=[s1, s1, s1, s2, s2, s2f, s1, s1],
        compiler_params=_cparams(("parallel", "parallel")),
        name="rwkv_prep",
    )(proj_rw, proj_rw, proj_rw, mup, mun, w0, wup_bd, a0, aup_bd, gup, k_k, k_a, r_k, ones_bd)


def _stack2(y):
    yb = y.astype(BF16)
    lo = _cols(yb.shape) < RWKV_HEAD
    zero = jnp.zeros_like(yb)
    return jnp.concatenate([jnp.where(lo, yb, zero), jnp.where(lo, zero, yb)], axis=0)


def _pair_mm(x, y):
    return _dot(x.astype(BF16), _stack2(y))


class _Masks:
    def __init__(self, reverse):
        shp = (CHUNK, PAIR)
        t = _rows(shp)
        s = _cols(shp) % CHUNK
        self.strict = (s > t) if reverse else (s < t)
        self.incl = (s >= t) if reverse else (s <= t)
        self.eye = s == t
        self.blk16 = (t // 16) == (s // 16)
        self.lvl32 = ((t // 32) == (s // 32)) & ((t // 16) != (s // 16))
        self.lvl64 = (t // 32) != (s // 32)
        sq = (PAIR, PAIR)
        self.bd = (_rows(sq) // RWKV_HEAD) == (_cols(sq) // RWKV_HEAD)
        self.eye_sq = _rows(sq) == _cols(sq)
        tt = _rows((CHUNK, CHUNK))
        ss = _cols((CHUNK, CHUNK))
        self.tri = jnp.where((ss >= tt) if reverse else (ss <= tt), 1.0, 0.0).astype(BF16)
        self.reverse = reverse


def _dot_hilo_l(w_bf16, x):
    hi, lo = _split(x)
    return _dot(w_bf16, hi) + _dot(w_bf16, lo)


def _col(x, p):
    return x[:, p * PAIR:(p + 1) * PAIR]


def _wkv_kernel(rf_ref, kkf_ref, vf_ref, kdf_ref, kkaf_ref, lwf_ref,
                rb_ref, kkb_ref, vb_ref, kdb_ref, kkab_ref, lwb_ref,
                yf_ref, yb_ref, h_ref, *, n_sub):
    @pl.when(pl.program_id(1) == 0)
    def _():
        h_ref[...] = jnp.zeros_like(h_ref)

    dirs = (
        (False, rf_ref, kkf_ref, vf_ref, kdf_ref, kkaf_ref, lwf_ref, yf_ref),
        (True, rb_ref, kkb_ref, vb_ref, kdb_ref, kkab_ref, lwb_ref, yb_ref),
    )
    masks = (_Masks(False), _Masks(True))
    chunks = [(di, j) for di in range(2) for j in range(n_sub)]
    units = [(di, j, p) for (di, j) in chunks for p in range(N_PAIRS)]

    rb, ab, bt, kt, bh, kh, vv, etot = {}, {}, {}, {}, {}, {}, {}, {}
    for c in chunks:
        di, j = c
        reverse, r_ref, kk_ref, v_ref, kd_ref, kka_ref, lw_ref, _ = dirs[di]
        rs = slice(j * CHUNK, (j + 1) * CHUNK)
        lw = lw_ref[0, 0, rs, :]
        kd = kd_ref[0, 0, rs, :].astype(F32)
        kka = kka_ref[0, 0, rs, :].astype(F32)
        cum = _dot_hilo_l(masks[di].tri, lw)
        tot = cum[0:1] if reverse else cum[CHUNK - 1: CHUNK]
        rb[c] = r_ref[0, rs, :].astype(F32) * jnp.exp(cum)
        ab[c] = -kk_ref[0, rs, :].astype(F32) * jnp.exp(cum - lw)
        ip = jnp.exp(-cum)
        bt[c] = kka * ip
        kt[c] = kd * ip
        ph = jnp.exp(tot - cum)
        bh[c] = kka * ph
        kh[c] = kd * ph
        vv[c] = v_ref[0, rs, :]
        etot[c] = jnp.exp(tot)

    def per_unit(fn):
        return {u: fn(u, (u[0], u[1]), u[2], masks[u[0]]) for u in units}

    s_all = per_unit(lambda u, c, p, m: _dot_nt(
        jnp.concatenate([_col(ab[c], p), _col(rb[c], p)], axis=0).astype(BF16),
        jnp.concatenate([_stack2(_col(bt[c], p)), _stack2(_col(kt[c], p))], axis=0)))
    n_ab = per_unit(lambda u, c, p, m: jnp.where(m.strict, s_all[u][:CHUNK, :PAIR], 0.0))
    a_ak = per_unit(lambda u, c, p, m: jnp.where(m.strict, s_all[u][:CHUNK, PAIR:], 0.0).astype(BF16))
    m_rb = per_unit(lambda u, c, p, m: jnp.where(m.incl, s_all[u][CHUNK:, :PAIR], 0.0).astype(BF16))
    m_rk = per_unit(lambda u, c, p, m: jnp.where(m.incl, s_all[u][CHUNK:, PAIR:], 0.0).astype(BF16))

    nd = per_unit(lambda u, c, p, m: jnp.where(m.blk16, n_ab[u], 0.0))
    t_inv = per_unit(lambda u, c, p, m: jnp.where(m.eye, 1.0, nd[u]))
    pw = per_unit(lambda u, c, p, m: _pair_mm(nd[u], nd[u]))
    for step in range(3):
        t_inv = per_unit(lambda u, c, p, m: t_inv[u] + _pair_mm(t_inv[u], pw[u]))
        if step < 2:
            pw = per_unit(lambda u, c, p, m: _pair_mm(pw[u], pw[u]))
    for lvl in ("lvl32", "lvl64"):
        tc = per_unit(lambda u, c, p, m: _pair_mm(t_inv[u], jnp.where(getattr(m, lvl), n_ab[u], 0.0)))
        t_inv = per_unit(lambda u, c, p, m: t_inv[u] + _pair_mm(tc[u], t_inv[u]))

    v2 = per_unit(lambda u, c, p, m: _stack2(_col(vv[c], p)))
    akv = per_unit(lambda u, c, p, m: _dot(a_ak[u], v2[u]))
    wu = per_unit(lambda u, c, p, m: _dot(
        t_inv[u].astype(BF16), jnp.concatenate([_stack2(_col(ab[c], p)), _stack2(akv[u])], axis=1)))
    qy = per_unit(lambda u, c, p, m: _dot(
        m_rb[u], jnp.concatenate([_stack2(wu[u][:, :PAIR]), _stack2(wu[u][:, PAIR:])], axis=1)))
    q_hat = per_unit(lambda u, c, p, m: (_col(rb[c], p) + qy[u][:, :PAIR]).astype(BF16))
    y_loc = per_unit(lambda u, c, p, m: qy[u][:, PAIR:] + _dot(m_rk[u], v2[u]))
    gd = per_unit(lambda u, c, p, m: _dot(_col(bh[c], p).T.astype(BF16), wu[u].astype(BF16)))
    kv = per_unit(lambda u, c, p, m: _dot(_col(kh[c], p).T.astype(BF16), _col(vv[c], p).astype(BF16)))
    g_m = per_unit(lambda u, c, p, m: (jnp.where(m.bd, gd[u][:, :PAIR], 0.0)
                                       + jnp.where(m.eye_sq, _col(etot[c], p), 0.0)).astype(BF16))
    d_m = per_unit(lambda u, c, p, m: jnp.where(m.bd, gd[u][:, PAIR:] + kv[u], 0.0))

    h = {(di, p): h_ref[di, p] for di in range(2) for p in range(N_PAIRS)}
    for step in range(n_sub):
        for di in range(2):
            j = n_sub - 1 - step if dirs[di][0] else step
            y_ref = dirs[di][7]
            rs = slice(j * CHUNK, (j + 1) * CHUNK)
            for p in range(N_PAIRS):
                u = (di, j, p)
                hb = h[(di, p)].astype(BF16)
                y_ref[0, rs, p * PAIR:(p + 1) * PAIR] = (_dot(q_hat[u], hb) + y_loc[u]).astype(BF16)
                h[(di, p)] = _dot(g_m[u], hb) + d_m[u]
    for di in range(2):
        for p in range(N_PAIRS):
            h_ref[di, p] = h[(di, p)]


def _wkv_call(r, kkn, v, kd, kka, lw, tm):
    bn, t, _ = r.shape
    nt = t // tm
    fwd = pl.BlockSpec((1, tm, D_RWKV), lambda b, i: (b, i, 0))
    bwd = pl.BlockSpec((1, tm, D_RWKV), lambda b, i: (b, nt - 1 - i, 0))
    fwd2 = pl.BlockSpec((1, 1, tm, D_RWKV), lambda b, i: (0, b, i, 0))
    bwd2 = pl.BlockSpec((1, 1, tm, D_RWKV), lambda b, i: (1, b, nt - 1 - i, 0))
    s1 = jax.ShapeDtypeStruct((bn, t, D_RWKV), BF16)
    return pl.pallas_call(
        functools.partial(_wkv_kernel, n_sub=tm // CHUNK),
        grid=(bn, nt),
        in_specs=[fwd, fwd, fwd, fwd2, fwd2, fwd2, bwd, bwd, bwd, bwd2, bwd2, bwd2],
        out_specs=[fwd, bwd],
        out_shape=[s1, s1],
        scratch_shapes=[pltpu.VMEM((2, N_PAIRS, PAIR, PAIR), F32)],
        compiler_params=_cparams(("parallel", "arbitrary")),
        name="wkv7_chunked",
    )(r, kkn, v, kd, kka, lw, r, kkn, v, kd, kka, lw)


def _gelu_tanh(x):
    return 0.5 * x * (1.0 + jnp.tanh(0.7978845608028654 * (x + 0.044715 * (x * x * x))))


def _route(logits_t, b_col):
    rows = [logits_t[e:e + 1] for e in range(N_EXPERTS)]
    mx = functools.reduce(jnp.maximum, rows)
    ex = [jnp.exp(x - mx) for x in rows]
    den = functools.reduce(lambda a, b: a + b, ex)
    probs = [e / den for e in ex]
    sel = [probs[e] + b_col[e:e + 1] for e in range(N_EXPERTS)]
    scores = []
    for g in range(N_GROUPS):
        a, b, c, d = sel[4 * g: 4 * g + 4]
        hi1, lo1 = jnp.maximum(a, b), jnp.minimum(a, b)
        hi2, lo2 = jnp.maximum(c, d), jnp.minimum(c, d)
        scores.append(jnp.maximum(hi1, hi2) + jnp.maximum(jnp.minimum(hi1, hi2), jnp.maximum(lo1, lo2)))
    best = scores[0]
    bg = jnp.zeros_like(best)
    for g in range(1, N_GROUPS):
        upd = scores[g] > best
        best = jnp.where(upd, scores[g], best)
        bg = jnp.where(upd, float(g), bg)

    def pick(vals):
        out = []
        for j in range(EXP_PER_GROUP):
            x = vals[j]
            for g in range(1, N_GROUPS):
                x = jnp.where(bg == float(g), vals[4 * g + j], x)
            out.append(x)
        return out

    sg = pick(sel)
    v1, i1 = sg[0], jnp.zeros_like(best)
    for j in range(1, EXP_PER_GROUP):
        upd = sg[j] > v1
        v1 = jnp.where(upd, sg[j], v1)
        i1 = jnp.where(upd, float(j), i1)
    neg = jnp.full_like(best, -jnp.inf)
    v2, i2 = neg, jnp.zeros_like(best)
    for j in range(EXP_PER_GROUP):
        cand = jnp.where(i1 == float(j), neg, sg[j])
        upd = cand > v2
        v2 = jnp.where(upd, cand, v2)
        i2 = jnp.where(upd, float(j), i2)
    return bg * float(EXP_PER_GROUP) + i1, bg * float(EXP_PER_GROUP) + i2


def _outproj_kernel(hl_ref, gate_ref, yf_ref, yb_ref, bonus_ref, g_ref, xa_ref, xb_ref, lnw_ref, lnb_ref, g1_ref,
                    sh2_ref, sc2_ref, n2_ref, wout_ref, wr_ref, br_ref, ones_ref,
                    xn_ref, h2_ref, route_ref, *, n_a):
    lru_out = (hl_ref[0, 0] + hl_ref[1, 0]) * _gelu_tanh(gate_ref[0])
    ones_bd = ones_ref[...]
    y = yf_ref[0].astype(F32) + yb_ref[0].astype(F32)
    mu = _dot_hilo(y, ones_bd) * (1.0 / RWKV_HEAD)
    yc = y - mu
    var = _dot_hilo(yc * yc, ones_bd) * (1.0 / RWKV_HEAD)
    gn = yc * lax.rsqrt(var + GN_EPS) * lnw_ref[...] + lnb_ref[...]
    rw_out = (gn + bonus_ref[0].astype(F32)) * g_ref[0].astype(F32)
    o = _dot(lru_out.astype(BF16), wout_ref[:D_LRU]) + _dot(rw_out.astype(BF16), wout_ref[D_LRU:])
    xn = _x_block(xa_ref, xb_ref, n_a) + g1_ref[0] * o
    xn_ref[0] = xn
    h2 = _norm_mod(xn, n2_ref[...], sc2_ref[0], sh2_ref[0])
    for j in range(TOK_ROWS):
        h2_ref[0, pl.ds(j, xn.shape[0], stride=TOK_ROWS), :] = h2[:, j * LANES:(j + 1) * LANES]
    logits_t = _dot3_nt(wr_ref[...], h2)
    e0, e1 = _route(logits_t, br_ref[...])
    zero = jnp.zeros_like(e0)
    route_ref[0, 0] = jnp.concatenate([e0, e1, zero, zero, zero, zero, zero, zero], axis=0)


def _outproj_call(hl, proj_lru, yf, yb, bonus, g, xa, xb, lnw, lnb, g1, sh2, sc2, n2, w_out_bf16, wr_t, br, ones_bd,
                  tm):
    n_a, t, d = xa.shape
    bn = g1.shape[0]
    nt = t // tm
    spec_a, spec_b = _x_specs(tm, d, n_a, nt)
    c2 = lambda b, i: (0, 0)
    tok = pl.BlockSpec((1, tm, D_RWKV), lambda b, i: (b, i, 0))
    tokd = pl.BlockSpec((1, tm, d), lambda b, i: (b, i, 0))
    per_b = pl.BlockSpec((1, 1, d), lambda b, i: (b, 0, 0))
    return pl.pallas_call(
        functools.partial(_outproj_kernel, n_a=n_a),
        grid=(bn, nt),
        in_specs=[
            pl.BlockSpec((2, 1, tm, D_LRU), lambda b, i: (0, b, i, 0)),
            pl.BlockSpec((1, tm, D_LRU), lambda b, i: (b, i, 1)),
            tok, tok, tok, tok, spec_a, spec_b,
            pl.BlockSpec((1, D_RWKV), c2),
            pl.BlockSpec((1, D_RWKV), c2),
            per_b, per_b, per_b,
            pl.BlockSpec((1, d), c2),
            pl.BlockSpec((d, d), c2),
            pl.BlockSpec((N_EXPERTS, d), c2),
            pl.BlockSpec((N_EXPERTS, 1), c2),
            pl.BlockSpec((D_RWKV, D_RWKV), c2),
        ],
        out_specs=[tokd, pl.BlockSpec((1, tm * TOK_ROWS, LANES), lambda b, i: (b, i, 0)),
                   pl.BlockSpec((1, 1, SUBLANES, tm), lambda b, i: (b, i, 0, 0))],
        out_shape=[
            jax.ShapeDtypeStruct((bn, t, d), F32),
            jax.ShapeDtypeStruct((bn, t * TOK_ROWS, LANES), F32),
            jax.ShapeDtypeStruct((bn, nt, SUBLANES, tm), F32),
        ],
        compiler_params=_cparams(("parallel", "parallel")),
        name="outproj_router",
    )(hl, proj_lru, yf, yb, bonus, g, xa, xb, lnw, lnb, g1, sh2, sc2, n2, w_out_bf16, wr_t, br, ones_bd)


def _row_copy(src_ref, src_row, dst_ref, dst_row, sem):
    src = src_ref.at[pl.ds(pl.multiple_of(src_row * TOK_ROWS, TOK_ROWS), TOK_ROWS)]
    dst = dst_ref.at[pl.ds(pl.multiple_of(dst_row * TOK_ROWS, TOK_ROWS), TOK_ROWS)]
    return pltpu.make_async_copy(src, dst, sem)


def _dispatch_kernel(pos_ref, h_ref, xb_in_ref, xb_ref, sem):
    del xb_in_ref
    tm = h_ref.shape[0] // TOK_ROWS

    def issue(r, carry):
        _row_copy(h_ref, r, xb_ref, pos_ref[0, 0, r], sem).start()
        return carry

    lax.fori_loop(0, tm, issue, 0, unroll=8)

    def drain(r, carry):
        _row_copy(h_ref, 0, xb_ref, 0, sem).wait()
        return carry

    lax.fori_loop(0, tm, drain, 0, unroll=True)


def _dispatch_call(h2, pos_tiles, xb_init, tm):
    n = h2.shape[0] // TOK_ROWS
    nt = n // tm
    return pl.pallas_call(
        _dispatch_kernel,
        grid=(nt,),
        in_specs=[
            pl.BlockSpec((1, 1, tm), lambda i: (i, 0, 0), memory_space=pltpu.SMEM),
            pl.BlockSpec((tm * TOK_ROWS, LANES), lambda i: (i, 0)),
            pl.BlockSpec(memory_space=pl.ANY),
        ],
        out_specs=pl.BlockSpec(memory_space=pl.ANY),
        out_shape=jax.ShapeDtypeStruct(xb_init.shape, F32),
        scratch_shapes=[pltpu.SemaphoreType.DMA(())],
        input_output_aliases={2: 0},
        compiler_params=_cparams(("arbitrary",)),
        name="moe_dispatch",
    )(pos_tiles, h2, xb_init)


def _expert_kernel(blk_a_ref, blk_b_ref, n_used_ref, x_ref, wr_ref,
                   wga_ref, wua_ref, wda_ref, wgb_ref, wub_ref, wdb_ref, o_ref):
    i = pl.program_id(0)

    @pl.when(i < n_used_ref[0])
    def _():
        x32 = jnp.concatenate([x_ref[pl.ds(j, MOE_BLOCK, stride=TOK_ROWS), :] for j in range(TOK_ROWS)], axis=1)
        x = x32.astype(BF16)
        w_diff = wr_ref[pl.ds(blk_a_ref[i], 1), :] - wr_ref[pl.ds(blk_b_ref[i], 1), :]
        l_diff = jnp.sum(x32 * w_diff, axis=-1, keepdims=True)
        g_a = 1.0 / (1.0 + jnp.exp(-l_diff))
        g_b = 1.0 / (1.0 + jnp.exp(l_diff))

        def ffn(wg_ref, wu_ref, wd_ref):
            gate = _dot(x, wg_ref[0])
            hid = gate * _sigmoid(gate) * _dot(x, wu_ref[0])
            return _dot(hid.astype(BF16), wd_ref[0])

        y = g_a * ffn(wga_ref, wua_ref, wda_ref) + g_b * ffn(wgb_ref, wub_ref, wdb_ref)
        for j in range(TOK_ROWS):
            o_ref[pl.ds(j, MOE_BLOCK, stride=TOK_ROWS), :] = y[:, j * LANES:(j + 1) * LANES]

    @pl.when(i >= n_used_ref[0])
    def _():
        o_ref[...] = jnp.zeros_like(o_ref)


def _expert_call(xb, blk_a, blk_b, n_used, wr_t, wg, wu, wd):
    n_blk = xb.shape[0] // (MOE_BLOCK * TOK_ROWS)
    d = D_MODEL
    w_in_a = pl.BlockSpec((1, d, D_EXPERT), lambda i, ba, bb, nu: (ba[i], 0, 0))
    w_in_b = pl.BlockSpec((1, d, D_EXPERT), lambda i, ba, bb, nu: (bb[i], 0, 0))
    grid_spec = pltpu.PrefetchScalarGridSpec(
        num_scalar_prefetch=3,
        grid=(n_blk,),
        in_specs=[
            pl.BlockSpec((MOE_BLOCK * TOK_ROWS, LANES), lambda i, ba, bb, nu: (i, 0)),
            pl.BlockSpec((N_EXPERTS, d), lambda i, ba, bb, nu: (0, 0)),
            w_in_a, w_in_a,
            pl.BlockSpec((1, D_EXPERT, d), lambda i, ba, bb, nu: (ba[i], 0, 0)),
            w_in_b, w_in_b,
            pl.BlockSpec((1, D_EXPERT, d), lambda i, ba, bb, nu: (bb[i], 0, 0)),
        ],
        out_specs=pl.BlockSpec((MOE_BLOCK * TOK_ROWS, LANES), lambda i, ba, bb, nu: (i, 0)),
    )
    return pl.pallas_call(
        _expert_kernel,
        grid_spec=grid_spec,
        out_shape=jax.ShapeDtypeStruct(xb.shape, F32),
        compiler_params=_cparams(("arbitrary",)),
        name="moe_experts",
    )(blk_a, blk_b, n_used, xb, wr_t, wg, wu, wd, wg, wu, wd)


def _combine_kernel(pos_ref, posn_ref, x_ref, g2_ref, nf_ref, yb_ref, *rest, n_first):
    *o_refs, buf_ref, sem = rest
    i = pl.program_id(0)
    n = pl.num_programs(0)
    tm = x_ref.shape[0]
    slot = i % 2

    def gather(p_ref, s):
        def issue(r, carry):
            _row_copy(yb_ref, p_ref[0, 0, r], buf_ref.at[s], r, sem.at[s]).start()
            return carry

        lax.fori_loop(0, tm, issue, 0, unroll=8)

    @pl.when(i == 0)
    def _():
        gather(pos_ref, 0)

    @pl.when(i + 1 < n)
    def _():
        gather(posn_ref, 1 - slot)

    def drain(r, carry):
        _row_copy(yb_ref, 0, buf_ref.at[slot], 0, sem.at[slot]).wait()
        return carry

    lax.fori_loop(0, tm, drain, 0, unroll=True)
    for j in range(TOK_ROWS):
        cs = slice(j * LANES, (j + 1) * LANES)
        y = buf_ref[slot, pl.ds(j, tm, stride=TOK_ROWS), :]
        o_refs[-1][:, cs] = x_ref[:, cs] + g2_ref[0, :, cs] * y
    if n_first is not None:
        o_first, o_second = o_refs
        xo = o_second[...]
        ms = jnp.mean(xo * xo, axis=-1, keepdims=True)
        res = xo * lax.rsqrt(ms + NORM_EPS) * nf_ref[...]
        o_second[...] = res

        @pl.when(i < n_first)
        def _():
            o_first[...] = res


def _combine_call(xn, yb, pos_tiles, g2, nf, t, tm, n_first=None):
    n, d = xn.shape
    nt = n // tm
    per_b = t // tm
    if n_first is None:
        out_specs = pl.BlockSpec((tm, d), lambda i: (i, 0))
        out_shape = jax.ShapeDtypeStruct((n, d), F32)
    else:
        out_specs = [pl.BlockSpec((tm, d), lambda i: (jnp.minimum(i, n_first - 1), 0)),
                     pl.BlockSpec((tm, d), lambda i: (jnp.maximum(i - n_first, 0), 0))]
        out_shape = [jax.ShapeDtypeStruct((n_first * tm, d), F32), jax.ShapeDtypeStruct((n - n_first * tm, d), F32)]
    return pl.pallas_call(
        functools.partial(_combine_kernel, n_first=n_first),
        grid=(nt,),
        in_specs=[
            pl.BlockSpec((1, 1, tm), lambda i: (i, 0, 0), memory_space=pltpu.SMEM),
            pl.BlockSpec((1, 1, tm), lambda i: (jnp.minimum(i + 1, nt - 1), 0, 0), memory_space=pltpu.SMEM),
            pl.BlockSpec((tm, d), lambda i: (i, 0)),
            pl.BlockSpec((1, 1, d), lambda i: (i // per_b, 0, 0)),
            pl.BlockSpec((1, d), lambda i: (0, 0)),
            pl.BlockSpec(memory_space=pl.ANY),
        ],
        out_specs=out_specs,
        out_shape=out_shape,
        scratch_shapes=[pltpu.VMEM((2, tm * TOK_ROWS, LANES), F32), pltpu.SemaphoreType.DMA((2,))],
        compiler_params=_cparams(("arbitrary",)),
        name="moe_combine",
    )(pos_tiles, pos_tiles, xn, g2, nf, yb)


def _block_diag(w):
    h, a, b = w.shape
    eye = jnp.eye(h, dtype=w.dtype)
    return jnp.einsum("hab,hg->hagb", w, eye).reshape(h * a, h * b)


def _head_ones():
    idx = np.arange(D_RWKV) // RWKV_HEAD
    return jnp.asarray((idx[:, None] == idx[None, :]).astype(np.float32), dtype=BF16)


_PAIRS = [(i, j) for i in range(EXP_PER_GROUP) for j in range(i + 1, EXP_PER_GROUP)]
N_CLASSES = N_GROUPS * len(_PAIRS)


def _routing_tables(route, n_tok, tm):
    flat = jnp.transpose(route, (2, 0, 1, 3)).reshape(SUBLANES, n_tok)
    e0 = flat[0].astype(jnp.int32)
    e1 = flat[1].astype(jnp.int32)
    ea = jnp.minimum(e0, e1)
    eb = jnp.maximum(e0, e1)
    i = ea % EXP_PER_GROUP
    j = eb % EXP_PER_GROUP
    cls = (ea // EXP_PER_GROUP) * len(_PAIRS) + (i * (2 * EXP_PER_GROUP - 1 - i)) // 2 + (j - i - 1)
    ar = jnp.arange(N_CLASSES, dtype=jnp.int32)
    oh = (cls[:, None] == ar).astype(jnp.int32)
    cs = jnp.cumsum(oh, axis=0)
    counts = cs[-1]
    padded = (counts + MOE_BLOCK - 1) // MOE_BLOCK * MOE_BLOCK
    pad_end = jnp.cumsum(padded)
    base = (cs - oh) + (pad_end - padded)[None, :]
    pos = jnp.take_along_axis(base, cls[:, None], axis=1)[:, 0]
    n_blk = (n_tok + N_CLASSES * (MOE_BLOCK - 1) + MOE_BLOCK - 1) // MOE_BLOCK
    blk_start = jnp.arange(n_blk, dtype=jnp.int32) * MOE_BLOCK
    blk_cls = jnp.minimum(jnp.sum(pad_end[None, :] <= blk_start[:, None], axis=-1), N_CLASSES - 1)
    cls_a = np.array([g * EXP_PER_GROUP + p[0] for g in range(N_GROUPS) for p in _PAIRS], np.int32)
    cls_b = np.array([g * EXP_PER_GROUP + p[1] for g in range(N_GROUPS) for p in _PAIRS], np.int32)
    blk_a = jnp.asarray(cls_a)[blk_cls]
    blk_b = jnp.asarray(cls_b)[blk_cls]
    n_used = (pad_end[-1] // MOE_BLOCK).astype(jnp.int32).reshape(1)
    pos_tiles = pos.reshape(n_tok // tm, 1, tm).astype(jnp.int32)
    return pos_tiles, blk_a, blk_b, n_used, n_blk * MOE_BLOCK


def _tiles(t):
    want = dict(inproj=512, lru=512, prep=512, wkv=256, moe=512)
    return {k: min(t, v) for k, v in want.items()}


def _trunk(x_a, x_b, c, w_mod, b_mod, norm1, norm2, w_in, w_out, conv_w, conv_b, lru_wa, lru_ba, lru_wx, lru_bx, lru_lam,
           mu_prev, mu_next, rw_w0, rw_wup, rw_a0, rw_aup, rw_gup, rw_kk, rw_ka, rw_rk, ln_x_w, ln_x_b,
           w_router, b_router, exp_gate, exp_up, exp_down, norm_f):
    n_first, t, d = x_a.shape
    bn = n_first + x_b.shape[0]
    n_tok = bn * t
    depth = w_mod.shape[0]
    ones_bd = _head_ones()
    mod = _mod_call(c, w_mod, b_mod)
    wr_t = jnp.transpose(w_router)
    br = b_router.reshape(N_EXPERTS, 1)
    tiles = _tiles(t)
    tm_moe = tiles["moe"]
    xb = None

    for l in range(depth):
        sh1, sc1, g1, sh2, sc2, g2 = [m.reshape(bn, 1, d) for m in jnp.split(mod[l], 6, axis=-1)]
        proj_lru, proj_rw = _inproj_call(x_a, x_b, sh1, sc1, norm1[l].reshape(1, d), w_in[l].astype(BF16),
                                         tiles["inproj"])

        wg = jnp.stack([jnp.concatenate([_block_diag(lru_wa[l, dd]), _block_diag(lru_wx[l, dd])], axis=1)
                        for dd in range(2)]).astype(BF16)
        bg = jnp.concatenate([lru_ba[l], lru_bx[l]], axis=1).reshape(2, 1, 2 * D_LRU)
        sp = jax.nn.softplus(-lru_lam[l]).reshape(2, 1, D_LRU)
        hl = _lru_call(proj_lru, conv_w[l], conv_b[l].reshape(1, D_LRU), wg, bg, sp, tiles["lru"])

        zeros = jnp.zeros((64, D_RWKV), F32)
        wup_bd = jnp.concatenate([jnp.concatenate([rw_wup[l, 0], zeros], axis=1),
                                  jnp.concatenate([zeros, rw_wup[l, 1]], axis=1)], axis=0).astype(BF16)
        aup_bd = jnp.concatenate([jnp.concatenate([rw_aup[l, 0], zeros], axis=1),
                                  jnp.concatenate([zeros, rw_aup[l, 1]], axis=1)], axis=0).astype(BF16)
        r, kkn, v, kd, kka, lw, bonus, g = _rwprep_call(
            proj_rw, mu_prev[l].reshape(1, -1), mu_next[l].reshape(1, -1),
            rw_w0[l].reshape(1, -1), wup_bd, rw_a0[l].reshape(1, -1), aup_bd, rw_gup[l].astype(BF16),
            rw_kk[l].reshape(1, -1), rw_ka[l].reshape(1, -1), rw_rk[l].reshape(1, -1), ones_bd, tiles["prep"])
        yf, yb = _wkv_call(r, kkn, v, kd, kka, lw, tiles["wkv"])

        xn, h2, route = _outproj_call(
            hl, proj_lru, yf, yb, bonus, g, x_a, x_b, ln_x_w[l].reshape(1, -1), ln_x_b[l].reshape(1, -1),
            g1, sh2, sc2, norm2[l].reshape(1, d), w_out[l].astype(BF16), wr_t, br, ones_bd, tm_moe)

        pos_tiles, blk_a, blk_b, n_used, n_rows = _routing_tables(route, n_tok, tm_moe)
        xb_init = jnp.zeros((n_rows * TOK_ROWS, LANES), F32) if xb is None else xb
        xb = _dispatch_call(h2.reshape(n_tok * TOK_ROWS, LANES), pos_tiles, xb_init, tm_moe)
        ybuf = _expert_call(xb, blk_a, blk_b, n_used, wr_t, exp_gate[l].astype(BF16), exp_up[l].astype(BF16),
                            exp_down[l].astype(BF16))
        last = l == depth - 1
        out = _combine_call(xn.reshape(n_tok, d), ybuf, pos_tiles, g2, norm_f.reshape(1, d), t, tm_moe,
                            n_first=n_first * (t // tm_moe) if last else None)
        if not last:
            x_a = x_b = out.reshape(bn, t, d)
    return out[0].reshape(n_first, t, d), out[1].reshape(bn - n_first, t, d)


def kernel(x_prompt, x_sample, c_prompt, c_sample, w_mod, b_mod, norm1, norm2, w_in, w_out, conv_w, conv_b, lru_wa, lru_ba, lru_wx, lru_bx, lru_lam, mu_prev, mu_next, rw_w0, rw_wup, rw_a0, rw_aup, rw_gup, rw_kk, rw_ka, rw_rk, ln_x_w, ln_x_b, w_router, b_router, exp_gate, exp_up, exp_down, norm_f):
    c = jnp.concatenate([c_prompt, c_sample], axis=0).astype(F32)
    return _trunk(x_prompt, x_sample, c, w_mod, b_mod, norm1, norm2, w_in, w_out, conv_w, conv_b, lru_wa, lru_ba,
                  lru_wx, lru_bx, lru_lam, mu_prev, mu_next, rw_w0, rw_wup, rw_a0, rw_aup, rw_gup, rw_kk, rw_ka,
                  rw_rk, ln_x_w, ln_x_b, w_router, b_router, exp_gate, exp_up, exp_down, norm_f)
```

```python
import functools

import jax
import jax.numpy as jnp
import numpy as np
from jax import lax
from jax.experimental import pallas as pl
from jax.experimental.pallas import tpu as pltpu

F32 = jnp.float32
BF16 = jnp.bfloat16

D_MODEL = 1024
D_LRU = 512
D_RWKV = 512
LRU_HEADS = 8
LRU_C = 8.0
RWKV_HEAD = 64
N_EXPERTS = 16
N_GROUPS = 4
EXP_PER_GROUP = 4
D_EXPERT = 512
MOE_BLOCK = 512
NORM_EPS = 1e-6
GN_EPS = 64e-5
D_RW_IN = 1920
D_IN = 2944

LANES = 128
SUBLANES = 8
CHUNK = 64
PAIR = 2 * RWKV_HEAD
N_PAIRS = D_RWKV // PAIR
TOK_ROWS = D_MODEL // LANES


def _cparams(sem, vmem_mb=48):
    return pltpu.CompilerParams(dimension_semantics=sem, vmem_limit_bytes=vmem_mb * 1024 * 1024)


def _dot(a, b):
    return jnp.dot(a, b, preferred_element_type=F32)


def _dot_nt(a, b):
    return lax.dot_general(a, b, (((1,), (1,)), ((), ())), preferred_element_type=F32)


def _split(x):
    hi = x.astype(BF16)
    lo = (x - hi.astype(F32)).astype(BF16)
    return hi, lo


def _dot_hilo(x, w_bf16):
    hi, lo = _split(x)
    return _dot(hi, w_bf16) + _dot(lo, w_bf16)


def _dot3(a, b):
    ah, al = _split(a)
    bh, bl = _split(b)
    return _dot(ah, bh) + (_dot(ah, bl) + _dot(al, bh))


def _dot3_nt(a, b):
    ah, al = _split(a)
    bh, bl = _split(b)
    return _dot_nt(ah, bh) + (_dot_nt(ah, bl) + _dot_nt(al, bh))


def _sigmoid(x):
    return 1.0 / (1.0 + jnp.exp(-x))


def _softplus(x):
    return jnp.maximum(x, 0.0) + jnp.log(1.0 + jnp.exp(-jnp.abs(x)))


def _rows(shape):
    return lax.broadcasted_iota(jnp.int32, shape, 0)


def _cols(shape):
    return lax.broadcasted_iota(jnp.int32, shape, 1)


def _mod_kernel(c_ref, w_ref, b_ref, o_ref):
    c = c_ref[...]
    cs = c * _sigmoid(c)
    o_ref[0] = _dot3(cs, w_ref[0]) + b_ref[0]


def _mod_call(c, w_mod, b_mod):
    nl, d, n6 = w_mod.shape
    bn = c.shape[0]
    tn = 1536
    return pl.pallas_call(
        _mod_kernel,
        grid=(nl, n6 // tn),
        in_specs=[
            pl.BlockSpec((bn, d), lambda l, j: (0, 0)),
            pl.BlockSpec((1, d, tn), lambda l, j: (l, 0, j)),
            pl.BlockSpec((1, 1, tn), lambda l, j: (l, 0, j)),
        ],
        out_specs=pl.BlockSpec((1, bn, tn), lambda l, j: (l, 0, j)),
        out_shape=jax.ShapeDtypeStruct((nl, bn, n6), F32),
        compiler_params=_cparams(("arbitrary", "arbitrary")),
        name="adaln_mod",
    )(c, w_mod, b_mod.reshape(nl, 1, n6))


def _norm_mod(x, nw, sc, sh):
    ms = jnp.mean(x * x, axis=-1, keepdims=True)
    return (x * lax.rsqrt(ms + NORM_EPS) * nw) * (1.0 + sc) + sh


def _x_specs(tm, d, n_a, nt):
    spec_a = pl.BlockSpec((1, tm, d), lambda b, i: (jnp.minimum(b, n_a - 1), jnp.where(b < n_a, i, nt - 1), 0))
    spec_b = pl.BlockSpec((1, tm, d), lambda b, i: (jnp.maximum(b - n_a, 0), jnp.where(b < n_a, 0, i), 0))
    return spec_a, spec_b


def _x_block(xa_ref, xb_ref, n_a):
    return jnp.where(pl.program_id(0) < n_a, xa_ref[0], xb_ref[0])


def _inproj_kernel(xa_ref, xb_ref, sh_ref, sc_ref, nw_ref, w_ref, lru_ref, rw_ref, *, n_a):
    h = _norm_mod(_x_block(xa_ref, xb_ref, n_a), nw_ref[...], sc_ref[0], sh_ref[0]).astype(BF16)
    lru_ref[0] = _dot(h, w_ref[:, : 2 * D_LRU])
    rw_ref[0] = _dot(h, w_ref[:, 2 * D_LRU:])


def _inproj_call(xa, xb, sh, sc, nw, w_in_bf16, tm):
    n_a, t, d = xa.shape
    bn = sh.shape[0]
    nt = t // tm
    spec_a, spec_b = _x_specs(tm, d, n_a, nt)
    return pl.pallas_call(
        functools.partial(_inproj_kernel, n_a=n_a),
        grid=(bn, nt),
        in_specs=[
            spec_a, spec_b,
            pl.BlockSpec((1, 1, d), lambda b, i: (b, 0, 0)),
            pl.BlockSpec((1, 1, d), lambda b, i: (b, 0, 0)),
            pl.BlockSpec((1, d), lambda b, i: (0, 0)),
            pl.BlockSpec((d, D_IN), lambda b, i: (0, 0)),
        ],
        out_specs=[
            pl.BlockSpec((1, tm, 2 * D_LRU), lambda b, i: (b, i, 0)),
            pl.BlockSpec((1, tm, D_RW_IN), lambda b, i: (b, i, 0)),
        ],
        out_shape=[
            jax.ShapeDtypeStruct((bn, t, 2 * D_LRU), F32),
            jax.ShapeDtypeStruct((bn, t, D_RW_IN), F32),
        ],
        compiler_params=_cparams(("parallel", "parallel")),
        name="norm_inproj",
    )(xa, xb, sh, sc, nw, w_in_bf16)


def _lru_kernel(x_ref, prev_ref, next_ref, cw_ref, cb_ref, wg_ref, bg_ref, sp_ref, o_ref,
                carry_ref, a_scr, b_scr, *, n_tiles):
    d = pl.program_id(1)
    i = pl.program_id(2)
    ti = d * (n_tiles - 1) + (1 - 2 * d) * i
    tm = x_ref.shape[1]

    @pl.when(i == 0)
    def _():
        carry_ref[...] = jnp.zeros_like(carry_ref)

    x = x_ref[0]
    prev8 = jnp.where(ti == 0, 0.0, prev_ref[0])
    next8 = jnp.where(ti == n_tiles - 1, 0.0, next_ref[0])
    r = _rows(x.shape)
    xm1 = jnp.where(r == 0, prev8[7:8], pltpu.roll(x, 1, 0))
    xm2 = jnp.where(r == 0, prev8[6:7], jnp.where(r == 1, prev8[7:8], pltpu.roll(x, 2, 0)))
    xp1 = jnp.where(r == tm - 1, next8[0:1], pltpu.roll(x, tm - 1, 0))
    cw = cw_ref[...]
    u = cw[0:1] * xm2 + cw[1:2] * xm1 + cw[2:3] * x + cw[3:4] * xp1 + cb_ref[...]

    gates = _dot(u.astype(BF16), wg_ref[0]) + bg_ref[0]
    rg = _sigmoid(gates[:, :D_LRU])
    ig = _sigmoid(gates[:, D_LRU:])
    log_a = (-LRU_C) * rg * sp_ref[0]
    a = jnp.exp(log_a)
    bv = jnp.sqrt(1.0 - a * a) * (ig * u)

    n_lt = D_LRU // LANES
    for c in range(n_lt):
        a_scr[c] = a[:, c * LANES:(c + 1) * LANES]
        b_scr[c] = bv[:, c * LANES:(c + 1) * LANES]
    win = SUBLANES * SUBLANES

    def scan(reverse):
        r8 = _rows((SUBLANES, D_LRU))
        carry = carry_ref[...]
        n_win = tm // win
        for w in (range(n_win - 1, -1, -1) if reverse else range(n_win)):
            base = w * win
            hs = [jnp.zeros((SUBLANES, LANES), F32)] * n_lt
            accs = [jnp.ones((SUBLANES, LANES), F32)] * n_lt
            for g in (range(SUBLANES - 1, -1, -1) if reverse else range(SUBLANES)):
                rows = pl.ds(base + g, SUBLANES, stride=SUBLANES)
                for c in range(n_lt):
                    ag = a_scr[c, rows, :]
                    hs[c] = ag * hs[c] + b_scr[c, rows, :]
                    accs[c] = ag * accs[c]
                    b_scr[c, rows, :] = hs[c]
                    a_scr[c, rows, :] = accs[c]
            h = jnp.concatenate(hs, axis=1)
            acc = jnp.concatenate(accs, axis=1)
            s = 1
            while s < SUBLANES:
                keep = (r8 >= SUBLANES - s) if reverse else (r8 < s)
                sh = SUBLANES - s if reverse else s
                h = acc * jnp.where(keep, 0.0, pltpu.roll(h, sh, 0)) + h
                acc = acc * jnp.where(keep, 1.0, pltpu.roll(acc, sh, 0))
                s *= 2
            ends = h + acc * carry
            if reverse:
                enter = jnp.where(r8 == SUBLANES - 1, carry, pltpu.roll(ends, SUBLANES - 1, 0))
                carry = ends[0:1]
            else:
                enter = jnp.where(r8 == 0, carry, pltpu.roll(ends, 1, 0))
                carry = ends[SUBLANES - 1: SUBLANES]
            for r in range(SUBLANES):
                rs = slice(base + r * SUBLANES, base + (r + 1) * SUBLANES)
                for c in range(n_lt):
                    cs = slice(c * LANES, (c + 1) * LANES)
                    o_ref[0, 0, rs, cs] = b_scr[c, rs, :] + a_scr[c, rs, :] * enter[r:r + 1, cs]
        carry_ref[...] = carry

    @pl.when(d == 0)
    def _():
        scan(False)

    @pl.when(d == 1)
    def _():
        scan(True)


def _lru_call(proj_lru, conv_w, conv_b, wg, bg, sp, tm):
    bn, t, _ = proj_lru.shape
    nt = t // tm
    r8 = tm // SUBLANES
    n8 = t // SUBLANES

    def tile(d, i):
        return d * (nt - 1) + (1 - 2 * d) * i

    return pl.pallas_call(
        functools.partial(_lru_kernel, n_tiles=nt),
        grid=(bn, 2, nt),
        in_specs=[
            pl.BlockSpec((1, tm, D_LRU), lambda b, d, i: (b, tile(d, i), 0)),
            pl.BlockSpec((1, SUBLANES, D_LRU), lambda b, d, i: (b, jnp.maximum(tile(d, i) * r8 - 1, 0), 0)),
            pl.BlockSpec((1, SUBLANES, D_LRU), lambda b, d, i: (b, jnp.minimum((tile(d, i) + 1) * r8, n8 - 1), 0)),
            pl.BlockSpec((4, D_LRU), lambda b, d, i: (0, 0)),
            pl.BlockSpec((1, D_LRU), lambda b, d, i: (0, 0)),
            pl.BlockSpec((1, D_LRU, 2 * D_LRU), lambda b, d, i: (d, 0, 0)),
            pl.BlockSpec((1, 1, 2 * D_LRU), lambda b, d, i: (d, 0, 0)),
            pl.BlockSpec((1, 1, D_LRU), lambda b, d, i: (d, 0, 0)),
        ],
        out_specs=pl.BlockSpec((1, 1, tm, D_LRU), lambda b, d, i: (d, b, tile(d, i), 0)),
        out_shape=jax.ShapeDtypeStruct((2, bn, t, D_LRU), F32),
        scratch_shapes=[pltpu.VMEM((1, D_LRU), F32),
                        pltpu.VMEM((D_LRU // LANES, tm, LANES), F32),
                        pltpu.VMEM((D_LRU // LANES, tm, LANES), F32)],
        compiler_params=_cparams(("parallel", "arbitrary", "arbitrary")),
        name="rg_lru",
    )(proj_lru, proj_lru, proj_lru, conv_w, conv_b, wg, bg, sp)


def _rwprep_kernel(z_ref, prev_ref, next_ref, mup_ref, mun_ref, w0_ref, wup_ref, a0_ref, aup_ref, gup_ref,
                   kk_ref, ka_ref, rk_ref, ones_ref,
                   r_out, kkn_out, v_out, kd_out, kka_out, lw_out, bonus_out, g_out, *, n_tiles):
    i = pl.program_id(1)
    tm = z_ref.shape[1]
    zc = z_ref[0]
    prev8 = jnp.where(i == 0, 0.0, prev_ref[0])
    next8 = jnp.where(i == n_tiles - 1, 0.0, next_ref[0])
    rr = _rows(zc.shape)
    zp = jnp.where(rr == 0, prev8[7:8], pltpu.roll(zc, 1, 0))
    zn = jnp.where(rr == tm - 1, next8[0:1], pltpu.roll(zc, tm - 1, 0))
    z = zc + mup_ref[...] * (zp - zc) + mun_ref[...] * (zn - zc)

    r = z[:, 0:512]
    k = z[:, 512:1024]
    v = z[:, 1024:1536]
    wd = z[:, 1536:1664]
    ad = z[:, 1664:1792]
    gd = z[:, 1792:1920]
    ones_bd = ones_ref[...]

    kkr = k * kk_ref[...]
    ss = _dot_hilo(kkr * kkr, ones_bd)
    kkn = kkr / jnp.maximum(jnp.sqrt(ss), 1e-12)

    wlin = w0_ref[...] + _dot(jnp.tanh(wd).astype(BF16), wup_ref[...])
    lw = -jnp.exp(-_softplus(-wlin) - 0.5)
    a = _sigmoid(a0_ref[...] + _dot(ad.astype(BF16), aup_ref[...]))

    ka = ka_ref[...]
    k_sum = jnp.zeros_like(k)
    for d in range(2):
        a_d = a[:, d * D_RWKV:(d + 1) * D_RWKV]
        kd = k * (1.0 + (a_d - 1.0) * ka)
        kd_out[d, 0] = kd.astype(BF16)
        kka_out[d, 0] = (kkn * a_d).astype(BF16)
        lw_out[d, 0] = lw[:, d * D_RWKV:(d + 1) * D_RWKV]
        k_sum = k_sum + kd
    r_out[0] = r.astype(BF16)
    kkn_out[0] = kkn.astype(BF16)
    v_out[0] = v.astype(BF16)
    bonus_out[0] = (_dot_hilo(r * k_sum * rk_ref[...], ones_bd) * v).astype(BF16)
    g_out[0] = _dot(_sigmoid(gd).astype(BF16), gup_ref[...]).astype(BF16)


def _rwprep_call(proj_rw, mup, mun, w0, wup_bd, a0, aup_bd, gup, k_k, k_a, r_k, ones_bd, tm):
    bn, t, _ = proj_rw.shape
    nt = t // tm
    r8 = tm // SUBLANES
    n8 = t // SUBLANES
    c2 = lambda b, i: (0, 0)
    tok = pl.BlockSpec((1, tm, D_RWKV), lambda b, i: (b, i, 0))
    tok2 = pl.BlockSpec((2, 1, tm, D_RWKV), lambda b, i: (0, b, i, 0))
    s1 = jax.ShapeDtypeStruct((bn, t, D_RWKV), BF16)
    s2 = jax.ShapeDtypeStruct((2, bn, t, D_RWKV), BF16)
    s2f = jax.ShapeDtypeStruct((2, bn, t, D_RWKV), F32)
    return pl.pallas_call(
        functools.partial(_rwprep_kernel, n_tiles=nt),
        grid=(bn, nt),
        in_specs=[
            pl.BlockSpec((1, tm, D_RW_IN), lambda b, i: (b, i, 0)),
            pl.BlockSpec((1, SUBLANES, D_RW_IN), lambda b, i: (b, jnp.maximum(i * r8 - 1, 0), 0)),
            pl.BlockSpec((1, SUBLANES, D_RW_IN), lambda b, i: (b, jnp.minimum((i + 1) * r8, n8 - 1), 0)),
            pl.BlockSpec((1, D_RW_IN), c2),
            pl.BlockSpec((1, D_RW_IN), c2),
            pl.BlockSpec((1, 2 * D_RWKV), c2),
            pl.BlockSpec((LANES, 2 * D_RWKV), c2),
            pl.BlockSpec((1, 2 * D_RWKV), c2),
            pl.BlockSpec((LANES, 2 * D_RWKV), c2),
            pl.BlockSpec((LANES, D_RWKV), c2),
            pl.BlockSpec((1, D_RWKV), c2),
            pl.BlockSpec((1, D_RWKV), c2),
            pl.BlockSpec((1, D_RWKV), c2),
            pl.BlockSpec((D_RWKV, D_RWKV), c2),
        ],
        out_specs=[tok, tok, tok, tok2, tok2, tok2, tok, tok],
        out_shape=[s1, s1, s1, s2, s2, s2f, s1, s1],
        compiler_params=_cparams(("parallel", "parallel")),
        name="rwkv_prep",
    )(proj_rw, proj_rw, proj_rw, mup, mun, w0, wup_bd, a0, aup_bd, gup, k_k, k_a, r_k, ones_bd)


def _stack2(y):
    yb = y.astype(BF16)
    lo = _cols(yb.shape) < RWKV_HEAD
    zero = jnp.zeros_like(yb)
    return jnp.concatenate([jnp.where(lo, yb, zero), jnp.where(lo, zero, yb)], axis=0)


def _pair_mm(x, y):
    return _dot(x.astype(BF16), _stack2(y))


class _Masks:
    def __init__(self, reverse):
        shp = (CHUNK, PAIR)
        t = _rows(shp)
        s = _cols(shp) % CHUNK
        self.strict = (s > t) if reverse else (s < t)
        self.incl = (s >= t) if reverse else (s <= t)
        self.eye = s == t
        self.blk16 = (t // 16) == (s // 16)
        self.lvl32 = ((t // 32) == (s // 32)) & ((t // 16) != (s // 16))
        self.lvl64 = (t // 32) != (s // 32)
        sq = (PAIR, PAIR)
        self.bd = (_rows(sq) // RWKV_HEAD) == (_cols(sq) // RWKV_HEAD)
        self.eye_sq = _rows(sq) == _cols(sq)
        tt = _rows((CHUNK, CHUNK))
        ss = _cols((CHUNK, CHUNK))
        self.tri = jnp.where((ss >= tt) if reverse else (ss <= tt), 1.0, 0.0).astype(BF16)
        self.reverse = reverse


def _dot_hilo_l(w_bf16, x):
    hi, lo = _split(x)
    return _dot(w_bf16, hi) + _dot(w_bf16, lo)


def _col(x, p):
    return x[:, p * PAIR:(p + 1) * PAIR]


def _wkv_kernel(rf_ref, kkf_ref, vf_ref, kdf_ref, kkaf_ref, lwf_ref,
                rb_ref, kkb_ref, vb_ref, kdb_ref, kkab_ref, lwb_ref,
                yf_ref, yb_ref, h_ref, *, n_sub):
    @pl.when(pl.program_id(1) == 0)
    def _():
        h_ref[...] = jnp.zeros_like(h_ref)

    dirs = (
        (False, rf_ref, kkf_ref, vf_ref, kdf_ref, kkaf_ref, lwf_ref, yf_ref),
        (True, rb_ref, kkb_ref, vb_ref, kdb_ref, kkab_ref, lwb_ref, yb_ref),
    )
    masks = (_Masks(False), _Masks(True))
    chunks = [(di, j) for di in range(2) for j in range(n_sub)]
    units = [(di, j, p) for (di, j) in chunks for p in range(N_PAIRS)]

    rb, ab, bt, kt, bh, kh, vv, etot = {}, {}, {}, {}, {}, {}, {}, {}
    for c in chunks:
        di, j = c
        reverse, r_ref, kk_ref, v_ref, kd_ref, kka_ref, lw_ref, _ = dirs[di]
        rs = slice(j * CHUNK, (j + 1) * CHUNK)
        lw = lw_ref[0, 0, rs, :]
        kd = kd_ref[0, 0, rs, :].astype(F32)
        kka = kka_ref[0, 0, rs, :].astype(F32)
        cum = _dot_hilo_l(masks[di].tri, lw)
        tot = cum[0:1] if reverse else cum[CHUNK - 1: CHUNK]
        rb[c] = r_ref[0, rs, :].astype(F32) * jnp.exp(cum)
        ab[c] = -kk_ref[0, rs, :].astype(F32) * jnp.exp(cum - lw)
        ip = jnp.exp(-cum)
        bt[c] = kka * ip
        kt[c] = kd * ip
        ph = jnp.exp(tot - cum)
        bh[c] = kka * ph
        kh[c] = kd * ph
        vv[c] = v_ref[0, rs, :]
        etot[c] = jnp.exp(tot)

    def per_unit(fn):
        return {u: fn(u, (u[0], u[1]), u[2], masks[u[0]]) for u in units}

    s_all = per_unit(lambda u, c, p, m: _dot_nt(
        jnp.concatenate([_col(ab[c], p), _col(rb[c], p)], axis=0).astype(BF16),
        jnp.concatenate([_stack2(_col(bt[c], p)), _stack2(_col(kt[c], p))], axis=0)))
    n_ab = per_unit(lambda u, c, p, m: jnp.where(m.strict, s_all[u][:CHUNK, :PAIR], 0.0))
    a_ak = per_unit(lambda u, c, p, m: jnp.where(m.strict, s_all[u][:CHUNK, PAIR:], 0.0).astype(BF16))
    m_rb = per_unit(lambda u, c, p, m: jnp.where(m.incl, s_all[u][CHUNK:, :PAIR], 0.0).astype(BF16))
    m_rk = per_unit(lambda u, c, p, m: jnp.where(m.incl, s_all[u][CHUNK:, PAIR:], 0.0).astype(BF16))

    nd = per_unit(lambda u, c, p, m: jnp.where(m.blk16, n_ab[u], 0.0))
    t_inv = per_unit(lambda u, c, p, m: jnp.where(m.eye, 1.0, nd[u]))
    pw = per_unit(lambda u, c, p, m: _pair_mm(nd[u], nd[u]))
    for step in range(2):
        tp = per_unit(lambda u, c, p, m: _pair_mm(jnp.concatenate([t_inv[u], pw[u]], axis=0), pw[u]))
        t_inv = per_unit(lambda u, c, p, m: t_inv[u] + tp[u][:CHUNK])
        pw = per_unit(lambda u, c, p, m: tp[u][CHUNK:])
    t_inv = per_unit(lambda u, c, p, m: t_inv[u] + _pair_mm(t_inv[u], pw[u]))
    for lvl in ("lvl32", "lvl64"):
        tc = per_unit(lambda u, c, p, m: _pair_mm(t_inv[u], jnp.where(getattr(m, lvl), n_ab[u], 0.0)))
        t_inv = per_unit(lambda u, c, p, m: t_inv[u] + _pair_mm(tc[u], t_inv[u]))

    v2 = per_unit(lambda u, c, p, m: _stack2(_col(vv[c], p)))
    av = per_unit(lambda u, c, p, m: _dot(jnp.concatenate([a_ak[u], m_rk[u]], axis=0), v2[u]))
    akv = per_unit(lambda u, c, p, m: av[u][:CHUNK])
    wu = per_unit(lambda u, c, p, m: _dot(
        t_inv[u].astype(BF16), jnp.concatenate([_stack2(_col(ab[c], p)), _stack2(akv[u])], axis=1)))
    qy = per_unit(lambda u, c, p, m: _dot(
        m_rb[u], jnp.concatenate([_stack2(wu[u][:, :PAIR]), _stack2(wu[u][:, PAIR:])], axis=1)))
    q_hat = per_unit(lambda u, c, p, m: (_col(rb[c], p) + qy[u][:, :PAIR]).astype(BF16))
    y_loc = per_unit(lambda u, c, p, m: qy[u][:, PAIR:] + av[u][CHUNK:])
    gd = per_unit(lambda u, c, p, m: _dot(_col(bh[c], p).T.astype(BF16), wu[u].astype(BF16)))
    kv = per_unit(lambda u, c, p, m: _dot(_col(kh[c], p).T.astype(BF16), _col(vv[c], p).astype(BF16)))
    g_m = per_unit(lambda u, c, p, m: (jnp.where(m.bd, gd[u][:, :PAIR], 0.0)
                                       + jnp.where(m.eye_sq, _col(etot[c], p), 0.0)).astype(BF16))
    d_m = per_unit(lambda u, c, p, m: jnp.where(m.bd, gd[u][:, PAIR:] + kv[u], 0.0))

    h = {(di, p): h_ref[di, p] for di in range(2) for p in range(N_PAIRS)}
    for step in range(n_sub):
        for di in range(2):
            j = n_sub - 1 - step if dirs[di][0] else step
            y_ref = dirs[di][7]
            rs = slice(j * CHUNK, (j + 1) * CHUNK)
            for p in range(N_PAIRS):
                u = (di, j, p)
                hb = h[(di, p)].astype(BF16)
                qh = _dot(jnp.concatenate([q_hat[u], g_m[u]], axis=0), hb)
                y_ref[0, rs, p * PAIR:(p + 1) * PAIR] = (qh[:CHUNK] + y_loc[u]).astype(BF16)
                h[(di, p)] = qh[CHUNK:] + d_m[u]
    for di in range(2):
        for p in range(N_PAIRS):
            h_ref[di, p] = h[(di, p)]


def _wkv_call(r, kkn, v, kd, kka, lw, tm):
    bn, t, _ = r.shape
    nt = t // tm
    fwd = pl.BlockSpec((1, tm, D_RWKV), lambda b, i: (b, i, 0))
    bwd = pl.BlockSpec((1, tm, D_RWKV), lambda b, i: (b, nt - 1 - i, 0))
    fwd2 = pl.BlockSpec((1, 1, tm, D_RWKV), lambda b, i: (0, b, i, 0))
    bwd2 = pl.BlockSpec((1, 1, tm, D_RWKV), lambda b, i: (1, b, nt - 1 - i, 0))
    s1 = jax.ShapeDtypeStruct((bn, t, D_RWKV), BF16)
    return pl.pallas_call(
        functools.partial(_wkv_kernel, n_sub=tm // CHUNK),
        grid=(bn, nt),
        in_specs=[fwd, fwd, fwd, fwd2, fwd2, fwd2, bwd, bwd, bwd, bwd2, bwd2, bwd2],
        out_specs=[fwd, bwd],
        out_shape=[s1, s1],
        scratch_shapes=[pltpu.VMEM((2, N_PAIRS, PAIR, PAIR), F32)],
        compiler_params=_cparams(("parallel", "arbitrary")),
        name="wkv7_chunked",
    )(r, kkn, v, kd, kka, lw, r, kkn, v, kd, kka, lw)


def _gelu_tanh(x):
    return 0.5 * x * (1.0 + jnp.tanh(0.7978845608028654 * (x + 0.044715 * (x * x * x))))


def _route(logits_t, b_col):
    rows = [logits_t[e:e + 1] for e in range(N_EXPERTS)]
    mx = functools.reduce(jnp.maximum, rows)
    ex = [jnp.exp(x - mx) for x in rows]
    den = functools.reduce(lambda a, b: a + b, ex)
    probs = [e / den for e in ex]
    sel = [probs[e] + b_col[e:e + 1] for e in range(N_EXPERTS)]
    scores = []
    for g in range(N_GROUPS):
        a, b, c, d = sel[4 * g: 4 * g + 4]
        hi1, lo1 = jnp.maximum(a, b), jnp.minimum(a, b)
        hi2, lo2 = jnp.maximum(c, d), jnp.minimum(c, d)
        scores.append(jnp.maximum(hi1, hi2) + jnp.maximum(jnp.minimum(hi1, hi2), jnp.maximum(lo1, lo2)))
    best = scores[0]
    bg = jnp.zeros_like(best)
    for g in range(1, N_GROUPS):
        upd = scores[g] > best
        best = jnp.where(upd, scores[g], best)
        bg = jnp.where(upd, float(g), bg)

    def pick(vals):
        out = []
        for j in range(EXP_PER_GROUP):
            x = vals[j]
            for g in range(1, N_GROUPS):
                x = jnp.where(bg == float(g), vals[4 * g + j], x)
            out.append(x)
        return out

    sg = pick(sel)
    v1, i1 = sg[0], jnp.zeros_like(best)
    for j in range(1, EXP_PER_GROUP):
        upd = sg[j] > v1
        v1 = jnp.where(upd, sg[j], v1)
        i1 = jnp.where(upd, float(j), i1)
    neg = jnp.full_like(best, -jnp.inf)
    v2, i2 = neg, jnp.zeros_like(best)
    for j in range(EXP_PER_GROUP):
        cand = jnp.where(i1 == float(j), neg, sg[j])
        upd = cand > v2
        v2 = jnp.where(upd, cand, v2)
        i2 = jnp.where(upd, float(j), i2)
    return bg * float(EXP_PER_GROUP) + i1, bg * float(EXP_PER_GROUP) + i2


def _outproj_kernel(hl_ref, gate_ref, yf_ref, yb_ref, bonus_ref, g_ref, xa_ref, xb_ref, lnw_ref, lnb_ref, g1_ref,
                    sh2_ref, sc2_ref, n2_ref, wout_ref, wr_ref, br_ref, ones_ref,
                    xn_ref, h2_ref, route_ref, *, n_a):
    lru_out = (hl_ref[0, 0] + hl_ref[1, 0]) * _gelu_tanh(gate_ref[0])
    ones_bd = ones_ref[...]
    y = yf_ref[0].astype(F32) + yb_ref[0].astype(F32)
    mu = _dot_hilo(y, ones_bd) * (1.0 / RWKV_HEAD)
    yc = y - mu
    var = _dot_hilo(yc * yc, ones_bd) * (1.0 / RWKV_HEAD)
    gn = yc * lax.rsqrt(var + GN_EPS) * lnw_ref[...] + lnb_ref[...]
    rw_out = (gn + bonus_ref[0].astype(F32)) * g_ref[0].astype(F32)
    o = _dot(lru_out.astype(BF16), wout_ref[:D_LRU]) + _dot(rw_out.astype(BF16), wout_ref[D_LRU:])
    xn = _x_block(xa_ref, xb_ref, n_a) + g1_ref[0] * o
    xn_ref[0] = xn
    h2 = _norm_mod(xn, n2_ref[...], sc2_ref[0], sh2_ref[0])
    for j in range(TOK_ROWS):
        h2_ref[0, pl.ds(j, xn.shape[0], stride=TOK_ROWS), :] = h2[:, j * LANES:(j + 1) * LANES]
    logits_t = _dot3_nt(wr_ref[...], h2)
    e0, e1 = _route(logits_t, br_ref[...])
    zero = jnp.zeros_like(e0)
    route_ref[0, 0] = jnp.concatenate([e0, e1, zero, zero, zero, zero, zero, zero], axis=0)


def _outproj_call(hl, proj_lru, yf, yb, bonus, g, xa, xb, lnw, lnb, g1, sh2, sc2, n2, w_out_bf16, wr_t, br, ones_bd,
                  tm):
    n_a, t, d = xa.shape
    bn = g1.shape[0]
    nt = t // tm
    spec_a, spec_b = _x_specs(tm, d, n_a, nt)
    c2 = lambda b, i: (0, 0)
    tok = pl.BlockSpec((1, tm, D_RWKV), lambda b, i: (b, i, 0))
    tokd = pl.BlockSpec((1, tm, d), lambda b, i: (b, i, 0))
    per_b = pl.BlockSpec((1, 1, d), lambda b, i: (b, 0, 0))
    return pl.pallas_call(
        functools.partial(_outproj_kernel, n_a=n_a),
        grid=(bn, nt),
        in_specs=[
            pl.BlockSpec((2, 1, tm, D_LRU), lambda b, i: (0, b, i, 0)),
            pl.BlockSpec((1, tm, D_LRU), lambda b, i: (b, i, 1)),
            tok, tok, tok, tok, spec_a, spec_b,
            pl.BlockSpec((1, D_RWKV), c2),
            pl.BlockSpec((1, D_RWKV), c2),
            per_b, per_b, per_b,
            pl.BlockSpec((1, d), c2),
            pl.BlockSpec((d, d), c2),
            pl.BlockSpec((N_EXPERTS, d), c2),
            pl.BlockSpec((N_EXPERTS, 1), c2),
            pl.BlockSpec((D_RWKV, D_RWKV), c2),
        ],
        out_specs=[tokd, pl.BlockSpec((1, tm * TOK_ROWS, LANES), lambda b, i: (b, i, 0)),
                   pl.BlockSpec((1, 1, SUBLANES, tm), lambda b, i: (b, i, 0, 0))],
        out_shape=[
            jax.ShapeDtypeStruct((bn, t, d), F32),
            jax.ShapeDtypeStruct((bn, t * TOK_ROWS, LANES), F32),
            jax.ShapeDtypeStruct((bn, nt, SUBLANES, tm), F32),
        ],
        compiler_params=_cparams(("parallel", "parallel")),
        name="outproj_router",
    )(hl, proj_lru, yf, yb, bonus, g, xa, xb, lnw, lnb, g1, sh2, sc2, n2, w_out_bf16, wr_t, br, ones_bd)


def _row_copy(src_ref, src_row, dst_ref, dst_row, sem):
    src = src_ref.at[pl.ds(pl.multiple_of(src_row * TOK_ROWS, TOK_ROWS), TOK_ROWS)]
    dst = dst_ref.at[pl.ds(pl.multiple_of(dst_row * TOK_ROWS, TOK_ROWS), TOK_ROWS)]
    return pltpu.make_async_copy(src, dst, sem)


def _dispatch_kernel(pos_ref, h_ref, xb_in_ref, xb_ref, sem):
    del xb_in_ref
    tm = h_ref.shape[0] // TOK_ROWS

    def issue(r, carry):
        _row_copy(h_ref, r, xb_ref, pos_ref[0, 0, r], sem).start()
        return carry

    lax.fori_loop(0, tm, issue, 0, unroll=8)

    def drain(r, carry):
        _row_copy(h_ref, 0, xb_ref, 0, sem).wait()
        return carry

    lax.fori_loop(0, tm, drain, 0, unroll=True)


def _dispatch_call(h2, pos_tiles, xb_init, tm):
    n = h2.shape[0] // TOK_ROWS
    nt = n // tm
    return pl.pallas_call(
        _dispatch_kernel,
        grid=(nt,),
        in_specs=[
            pl.BlockSpec((1, 1, tm), lambda i: (i, 0, 0), memory_space=pltpu.SMEM),
            pl.BlockSpec((tm * TOK_ROWS, LANES), lambda i: (i, 0)),
            pl.BlockSpec(memory_space=pl.ANY),
        ],
        out_specs=pl.BlockSpec(memory_space=pl.ANY),
        out_shape=jax.ShapeDtypeStruct(xb_init.shape, F32),
        scratch_shapes=[pltpu.SemaphoreType.DMA(())],
        input_output_aliases={2: 0},
        compiler_params=_cparams(("arbitrary",)),
        name="moe_dispatch",
    )(pos_tiles, h2, xb_init)


def _expert_kernel(blk_a_ref, blk_b_ref, n_used_ref, x_ref, wr_ref,
                   wga_ref, wua_ref, wda_ref, wgb_ref, wub_ref, wdb_ref, o_ref):
    i = pl.program_id(0)

    @pl.when(i < n_used_ref[0])
    def _():
        x32 = jnp.concatenate([x_ref[pl.ds(j, MOE_BLOCK, stride=TOK_ROWS), :] for j in range(TOK_ROWS)], axis=1)
        x = x32.astype(BF16)
        w_diff = wr_ref[pl.ds(blk_a_ref[i], 1), :] - wr_ref[pl.ds(blk_b_ref[i], 1), :]
        l_diff = jnp.sum(x32 * w_diff, axis=-1, keepdims=True)
        g_a = 1.0 / (1.0 + jnp.exp(-l_diff))
        g_b = 1.0 / (1.0 + jnp.exp(l_diff))

        def ffn(wg_ref, wu_ref, wd_ref):
            gate = _dot(x, wg_ref[0])
            hid = gate * _sigmoid(gate) * _dot(x, wu_ref[0])
            return _dot(hid.astype(BF16), wd_ref[0])

        y = g_a * ffn(wga_ref, wua_ref, wda_ref) + g_b * ffn(wgb_ref, wub_ref, wdb_ref)
        for j in range(TOK_ROWS):
            o_ref[pl.ds(j, MOE_BLOCK, stride=TOK_ROWS), :] = y[:, j * LANES:(j + 1) * LANES]

    @pl.when(i >= n_used_ref[0])
    def _():
        o_ref[...] = jnp.zeros_like(o_ref)


def _expert_call(xb, blk_a, blk_b, n_used, wr_t, wg, wu, wd):
    n_blk = xb.shape[0] // (MOE_BLOCK * TOK_ROWS)
    d = D_MODEL
    w_in_a = pl.BlockSpec((1, d, D_EXPERT), lambda i, ba, bb, nu: (ba[i], 0, 0))
    w_in_b = pl.BlockSpec((1, d, D_EXPERT), lambda i, ba, bb, nu: (bb[i], 0, 0))
    grid_spec = pltpu.PrefetchScalarGridSpec(
        num_scalar_prefetch=3,
        grid=(n_blk,),
        in_specs=[
            pl.BlockSpec((MOE_BLOCK * TOK_ROWS, LANES), lambda i, ba, bb, nu: (i, 0)),
            pl.BlockSpec((N_EXPERTS, d), lambda i, ba, bb, nu: (0, 0)),
            w_in_a, w_in_a,
            pl.BlockSpec((1, D_EXPERT, d), lambda i, ba, bb, nu: (ba[i], 0, 0)),
            w_in_b, w_in_b,
            pl.BlockSpec((1, D_EXPERT, d), lambda i, ba, bb, nu: (bb[i], 0, 0)),
        ],
        out_specs=pl.BlockSpec((MOE_BLOCK * TOK_ROWS, LANES), lambda i, ba, bb, nu: (i, 0)),
    )
    return pl.pallas_call(
        _expert_kernel,
        grid_spec=grid_spec,
        out_shape=jax.ShapeDtypeStruct(xb.shape, F32),
        compiler_params=_cparams(("arbitrary",)),
        name="moe_experts",
    )(blk_a, blk_b, n_used, xb, wr_t, wg, wu, wd, wg, wu, wd)


def _combine_kernel(pos_ref, posn_ref, x_ref, g2_ref, nf_ref, yb_ref, *rest, n_first):
    *o_refs, buf_ref, sem = rest
    i = pl.program_id(0)
    n = pl.num_programs(0)
    tm = x_ref.shape[0]
    slot = i % 2

    def gather(p_ref, s):
        def issue(r, carry):
            _row_copy(yb_ref, p_ref[0, 0, r], buf_ref.at[s], r, sem.at[s]).start()
            return carry

        lax.fori_loop(0, tm, issue, 0, unroll=8)

    @pl.when(i == 0)
    def _():
        gather(pos_ref, 0)

    @pl.when(i + 1 < n)
    def _():
        gather(posn_ref, 1 - slot)

    def drain(r, carry):
        _row_copy(yb_ref, 0, buf_ref.at[slot], 0, sem.at[slot]).wait()
        return carry

    lax.fori_loop(0, tm, drain, 0, unroll=True)
    for j in range(TOK_ROWS):
        cs = slice(j * LANES, (j + 1) * LANES)
        y = buf_ref[slot, pl.ds(j, tm, stride=TOK_ROWS), :]
        o_refs[-1][:, cs] = x_ref[:, cs] + g2_ref[0, :, cs] * y
    if n_first is not None:
        o_first, o_second = o_refs
        xo = o_second[...]
        ms = jnp.mean(xo * xo, axis=-1, keepdims=True)
        res = xo * lax.rsqrt(ms + NORM_EPS) * nf_ref[...]
        o_second[...] = res

        @pl.when(i < n_first)
        def _():
            o_first[...] = res


def _combine_call(xn, yb, pos_tiles, g2, nf, t, tm, n_first=None):
    n, d = xn.shape
    nt = n // tm
    per_b = t // tm
    if n_first is None:
        out_specs = pl.BlockSpec((tm, d), lambda i: (i, 0))
        out_shape = jax.ShapeDtypeStruct((n, d), F32)
    else:
        out_specs = [pl.BlockSpec((tm, d), lambda i: (jnp.minimum(i, n_first - 1), 0)),
                     pl.BlockSpec((tm, d), lambda i: (jnp.maximum(i - n_first, 0), 0))]
        out_shape = [jax.ShapeDtypeStruct((n_first * tm, d), F32), jax.ShapeDtypeStruct((n - n_first * tm, d), F32)]
    return pl.pallas_call(
        functools.partial(_combine_kernel, n_first=n_first),
        grid=(nt,),
        in_specs=[
            pl.BlockSpec((1, 1, tm), lambda i: (i, 0, 0), memory_space=pltpu.SMEM),
            pl.BlockSpec((1, 1, tm), lambda i: (jnp.minimum(i + 1, nt - 1), 0, 0), memory_space=pltpu.SMEM),
            pl.BlockSpec((tm, d), lambda i: (i, 0)),
            pl.BlockSpec((1, 1, d), lambda i: (i // per_b, 0, 0)),
            pl.BlockSpec((1, d), lambda i: (0, 0)),
            pl.BlockSpec(memory_space=pl.ANY),
        ],
        out_specs=out_specs,
        out_shape=out_shape,
        scratch_shapes=[pltpu.VMEM((2, tm * TOK_ROWS, LANES), F32), pltpu.SemaphoreType.DMA((2,))],
        compiler_params=_cparams(("arbitrary",)),
        name="moe_combine",
    )(pos_tiles, pos_tiles, xn, g2, nf, yb)


def _block_diag(w):
    h, a, b = w.shape
    eye = jnp.eye(h, dtype=w.dtype)
    return jnp.einsum("hab,hg->hagb", w, eye).reshape(h * a, h * b)


def _head_ones():
    idx = np.arange(D_RWKV) // RWKV_HEAD
    return jnp.asarray((idx[:, None] == idx[None, :]).astype(np.float32), dtype=BF16)


_PAIRS = [(i, j) for i in range(EXP_PER_GROUP) for j in range(i + 1, EXP_PER_GROUP)]
N_CLASSES = N_GROUPS * len(_PAIRS)


def _routing_tables(route, n_tok, tm):
    flat = jnp.transpose(route, (2, 0, 1, 3)).reshape(SUBLANES, n_tok)
    e0 = flat[0].astype(jnp.int32)
    e1 = flat[1].astype(jnp.int32)
    ea = jnp.minimum(e0, e1)
    eb = jnp.maximum(e0, e1)
    i = ea % EXP_PER_GROUP
    j = eb % EXP_PER_GROUP
    cls = (ea // EXP_PER_GROUP) * len(_PAIRS) + (i * (2 * EXP_PER_GROUP - 1 - i)) // 2 + (j - i - 1)
    ar = jnp.arange(N_CLASSES, dtype=jnp.int32)
    oh = (cls[:, None] == ar).astype(jnp.int32)
    cs = jnp.cumsum(oh, axis=0)
    counts = cs[-1]
    padded = (counts + MOE_BLOCK - 1) // MOE_BLOCK * MOE_BLOCK
    pad_end = jnp.cumsum(padded)
    base = (cs - oh) + (pad_end - padded)[None, :]
    pos = jnp.take_along_axis(base, cls[:, None], axis=1)[:, 0]
    n_blk = (n_tok + N_CLASSES * (MOE_BLOCK - 1) + MOE_BLOCK - 1) // MOE_BLOCK
    blk_start = jnp.arange(n_blk, dtype=jnp.int32) * MOE_BLOCK
    blk_cls = jnp.minimum(jnp.sum(pad_end[None, :] <= blk_start[:, None], axis=-1), N_CLASSES - 1)
    cls_a = np.array([g * EXP_PER_GROUP + p[0] for g in range(N_GROUPS) for p in _PAIRS], np.int32)
    cls_b = np.array([g * EXP_PER_GROUP + p[1] for g in range(N_GROUPS) for p in _PAIRS], np.int32)
    blk_a = jnp.asarray(cls_a)[blk_cls]
    blk_b = jnp.asarray(cls_b)[blk_cls]
    n_used = (pad_end[-1] // MOE_BLOCK).astype(jnp.int32).reshape(1)
    pos_tiles = pos.reshape(n_tok // tm, 1, tm).astype(jnp.int32)
    return pos_tiles, blk_a, blk_b, n_used, n_blk * MOE_BLOCK


def _tiles(t):
    want = dict(inproj=512, lru=512, prep=512, wkv=512, moe=512)
    return {k: min(t, v) for k, v in want.items()}


def _trunk(x_a, x_b, c, w_mod, b_mod, norm1, norm2, w_in, w_out, conv_w, conv_b, lru_wa, lru_ba, lru_wx, lru_bx, lru_lam,
           mu_prev, mu_next, rw_w0, rw_wup, rw_a0, rw_aup, rw_gup, rw_kk, rw_ka, rw_rk, ln_x_w, ln_x_b,
           w_router, b_router, exp_gate, exp_up, exp_down, norm_f):
    n_first, t, d = x_a.shape
    bn = n_first + x_b.shape[0]
    n_tok = bn * t
    depth = w_mod.shape[0]
    ones_bd = _head_ones()
    mod = _mod_call(c, w_mod, b_mod)
    wr_t = jnp.transpose(w_router)
    br = b_router.reshape(N_EXPERTS, 1)
    tiles = _tiles(t)
    tm_moe = tiles["moe"]
    xb = None

    for l in range(depth):
        sh1, sc1, g1, sh2, sc2, g2 = [m.reshape(bn, 1, d) for m in jnp.split(mod[l], 6, axis=-1)]
        proj_lru, proj_rw = _inproj_call(x_a, x_b, sh1, sc1, norm1[l].reshape(1, d), w_in[l].astype(BF16),
                                         tiles["inproj"])

        wg = jnp.stack([jnp.concatenate([_block_diag(lru_wa[l, dd]), _block_diag(lru_wx[l, dd])], axis=1)
                        for dd in range(2)]).astype(BF16)
        bg = jnp.concatenate([lru_ba[l], lru_bx[l]], axis=1).reshape(2, 1, 2 * D_LRU)
        sp = jax.nn.softplus(-lru_lam[l]).reshape(2, 1, D_LRU)
        hl = _lru_call(proj_lru, conv_w[l], conv_b[l].reshape(1, D_LRU), wg, bg, sp, tiles["lru"])

        zeros = jnp.zeros((64, D_RWKV), F32)
        wup_bd = jnp.concatenate([jnp.concatenate([rw_wup[l, 0], zeros], axis=1),
                                  jnp.concatenate([zeros, rw_wup[l, 1]], axis=1)], axis=0).astype(BF16)
        aup_bd = jnp.concatenate([jnp.concatenate([rw_aup[l, 0], zeros], axis=1),
                                  jnp.concatenate([zeros, rw_aup[l, 1]], axis=1)], axis=0).astype(BF16)
        r, kkn, v, kd, kka, lw, bonus, g = _rwprep_call(
            proj_rw, mu_prev[l].reshape(1, -1), mu_next[l].reshape(1, -1),
            rw_w0[l].reshape(1, -1), wup_bd, rw_a0[l].reshape(1, -1), aup_bd, rw_gup[l].astype(BF16),
            rw_kk[l].reshape(1, -1), rw_ka[l].reshape(1, -1), rw_rk[l].reshape(1, -1), ones_bd, tiles["prep"])
        yf, yb = _wkv_call(r, kkn, v, kd, kka, lw, tiles["wkv"])

        xn, h2, route = _outproj_call(
            hl, proj_lru, yf, yb, bonus, g, x_a, x_b, ln_x_w[l].reshape(1, -1), ln_x_b[l].reshape(1, -1),
            g1, sh2, sc2, norm2[l].reshape(1, d), w_out[l].astype(BF16), wr_t, br, ones_bd, tm_moe)

        pos_tiles, blk_a, blk_b, n_used, n_rows = _routing_tables(route, n_tok, tm_moe)
        xb_init = jnp.zeros((n_rows * TOK_ROWS, LANES), F32) if xb is None else xb
        xb = _dispatch_call(h2.reshape(n_tok * TOK_ROWS, LANES), pos_tiles, xb_init, tm_moe)
        ybuf = _expert_call(xb, blk_a, blk_b, n_used, wr_t, exp_gate[l].astype(BF16), exp_up[l].astype(BF16),
                            exp_down[l].astype(BF16))
        last = l == depth - 1
        out = _combine_call(xn.reshape(n_tok, d), ybuf, pos_tiles, g2, norm_f.reshape(1, d), t, tm_moe,
                            n_first=n_first * (t // tm_moe) if last else None)
        if not last:
            x_a = x_b = out.reshape(bn, t, d)
    return out[0].reshape(n_first, t, d), out[1].reshape(bn - n_first, t, d)


def kernel(x_prompt, x_sample, c_prompt, c_sample, w_mod, b_mod, norm1, norm2, w_in, w_out, conv_w, conv_b, lru_wa, lru_ba, lru_wx, lru_bx, lru_lam, mu_prev, mu_next, rw_w0, rw_wup, rw_a0, rw_aup, rw_gup, rw_kk, rw_ka, rw_rk, ln_x_w, ln_x_b, w_router, b_router, exp_gate, exp_up, exp_down, norm_f):
    c = jnp.concatenate([c_prompt, c_sample], axis=0).astype(F32)
    return _trunk(x_prompt, x_sample, c, w_mod, b_mod, norm1, norm2, w_in, w_out, conv_w, conv_b, lru_wa, lru_ba,
                  lru_wx, lru_bx, lru_lam, mu_prev, mu_next, rw_w0, rw_wup, rw_a0, rw_aup, rw_gup, rw_kk, rw_ka,
                  rw_rk, ln_x_w, ln_x_b, w_router, b_router, exp_gate, exp_up, exp_down, norm_f)
```

```python
import functools

import jax
import jax.numpy as jnp
import numpy as np
from jax import lax
from jax.experimental import pallas as pl
from jax.experimental.pallas import tpu as pltpu

F32 = jnp.float32
BF16 = jnp.bfloat16

D_MODEL = 1024
D_LRU = 512
D_RWKV = 512
LRU_HEADS = 8
LRU_C = 8.0
RWKV_HEAD = 64
N_EXPERTS = 16
N_GROUPS = 4
EXP_PER_GROUP = 4
D_EXPERT = 512
MOE_BLOCK = 512
NORM_EPS = 1e-6
GN_EPS = 64e-5
D_RW_IN = 1920
D_IN = 2944

LANES = 128
SUBLANES = 8
CHUNK = 64
PAIR = 2 * RWKV_HEAD
N_PAIRS = D_RWKV // PAIR
TOK_ROWS = D_MODEL // LANES


def _cparams(sem, vmem_mb=48):
    return pltpu.CompilerParams(dimension_semantics=sem, vmem_limit_bytes=vmem_mb * 1024 * 1024)


def _dot(a, b):
    return jnp.dot(a, b, preferred_element_type=F32)


def _dot_nt(a, b):
    return lax.dot_general(a, b, (((1,), (1,)), ((), ())), preferred_element_type=F32)


def _split(x):
    hi = x.astype(BF16)
    lo = (x - hi.astype(F32)).astype(BF16)
    return hi, lo


def _dot_hilo(x, w_bf16):
    hi, lo = _split(x)
    return _dot(hi, w_bf16) + _dot(lo, w_bf16)


def _dot3(a, b):
    ah, al = _split(a)
    bh, bl = _split(b)
    return _dot(ah, bh) + (_dot(ah, bl) + _dot(al, bh))


def _dot3_nt(a, b):
    ah, al = _split(a)
    bh, bl = _split(b)
    return _dot_nt(ah, bh) + (_dot_nt(ah, bl) + _dot_nt(al, bh))


def _sigmoid(x):
    return 1.0 / (1.0 + jnp.exp(-x))


def _rows_before(x, halo8, k):
    rolled = pltpu.roll(x, k, 0)
    head = rolled[0:SUBLANES]
    r8 = _rows(head.shape)
    for j in range(k):
        head = jnp.where(r8 == j, halo8[SUBLANES - k + j: SUBLANES - k + j + 1], head)
    return jnp.concatenate([head, rolled[SUBLANES:]], axis=0)


def _rows_after(x, halo8, k):
    n = x.shape[0]
    rolled = pltpu.roll(x, n - k, 0)
    tail = rolled[n - SUBLANES:]
    r8 = _rows(tail.shape)
    for j in range(k):
        tail = jnp.where(r8 == SUBLANES - k + j, halo8[j: j + 1], tail)
    return jnp.concatenate([rolled[: n - SUBLANES], tail], axis=0)


def _rows(shape):
    return lax.broadcasted_iota(jnp.int32, shape, 0)


def _cols(shape):
    return lax.broadcasted_iota(jnp.int32, shape, 1)


def _mod_kernel(c_ref, w_ref, b_ref, o_ref):
    c = c_ref[...]
    cs = c * _sigmoid(c)
    o_ref[0] = _dot3(cs, w_ref[0]) + b_ref[0]


def _mod_call(c, w_mod, b_mod):
    nl, d, n6 = w_mod.shape
    bn = c.shape[0]
    tn = 1536
    return pl.pallas_call(
        _mod_kernel,
        grid=(nl, n6 // tn),
        in_specs=[
            pl.BlockSpec((bn, d), lambda l, j: (0, 0)),
            pl.BlockSpec((1, d, tn), lambda l, j: (l, 0, j)),
            pl.BlockSpec((1, 1, tn), lambda l, j: (l, 0, j)),
        ],
        out_specs=pl.BlockSpec((1, bn, tn), lambda l, j: (l, 0, j)),
        out_shape=jax.ShapeDtypeStruct((nl, bn, n6), F32),
        compiler_params=_cparams(("arbitrary", "arbitrary")),
        name="adaln_mod",
    )(c, w_mod, b_mod.reshape(nl, 1, n6))


def _norm_mod(x, nw, sc, sh):
    ms = jnp.mean(x * x, axis=-1, keepdims=True)
    return (x * lax.rsqrt(ms + NORM_EPS) * nw) * (1.0 + sc) + sh


def _x_specs(tm, d, n_a, nt):
    spec_a = pl.BlockSpec((1, tm, d), lambda b, i: (jnp.minimum(b, n_a - 1), jnp.where(b < n_a, i, nt - 1), 0))
    spec_b = pl.BlockSpec((1, tm, d), lambda b, i: (jnp.maximum(b - n_a, 0), jnp.where(b < n_a, 0, i), 0))
    return spec_a, spec_b


def _x_block(xa_ref, xb_ref, n_a):
    return jnp.where(pl.program_id(0) < n_a, xa_ref[0], xb_ref[0])


def _inproj_kernel(xa_ref, xb_ref, sh_ref, sc_ref, nw_ref, w_ref, lru_ref, rw_ref, *, n_a):
    h = _norm_mod(_x_block(xa_ref, xb_ref, n_a), nw_ref[...], sc_ref[0], sh_ref[0]).astype(BF16)
    lru_ref[0] = _dot(h, w_ref[:, : 2 * D_LRU])
    rw_ref[0] = _dot(h, w_ref[:, 2 * D_LRU:])


def _inproj_call(xa, xb, sh, sc, nw, w_in_bf16, tm):
    n_a, t, d = xa.shape
    bn = sh.shape[0]
    nt = t // tm
    spec_a, spec_b = _x_specs(tm, d, n_a, nt)
    return pl.pallas_call(
        functools.partial(_inproj_kernel, n_a=n_a),
        grid=(bn, nt),
        in_specs=[
            spec_a, spec_b,
            pl.BlockSpec((1, 1, d), lambda b, i: (b, 0, 0)),
            pl.BlockSpec((1, 1, d), lambda b, i: (b, 0, 0)),
            pl.BlockSpec((1, d), lambda b, i: (0, 0)),
            pl.BlockSpec((d, D_IN), lambda b, i: (0, 0)),
        ],
        out_specs=[
            pl.BlockSpec((1, tm, 2 * D_LRU), lambda b, i: (b, i, 0)),
            pl.BlockSpec((1, tm, D_RW_IN), lambda b, i: (b, i, 0)),
        ],
        out_shape=[
            jax.ShapeDtypeStruct((bn, t, 2 * D_LRU), F32),
            jax.ShapeDtypeStruct((bn, t, D_RW_IN), F32),
        ],
        compiler_params=_cparams(("parallel", "parallel")),
        name="norm_inproj",
    )(xa, xb, sh, sc, nw, w_in_bf16)


def _lru_kernel(x_ref, prev_ref, next_ref, cw_ref, cb_ref, wg_ref, bg_ref, sp_ref, o_ref,
                carry_ref, a_scr, b_scr, *, n_tiles):
    d = pl.program_id(1)
    i = pl.program_id(2)
    ti = d * (n_tiles - 1) + (1 - 2 * d) * i
    tm = x_ref.shape[1]

    @pl.when(i == 0)
    def _():
        carry_ref[...] = jnp.zeros_like(carry_ref)

    x = x_ref[0]
    prev8 = jnp.where(ti == 0, 0.0, prev_ref[0])
    next8 = jnp.where(ti == n_tiles - 1, 0.0, next_ref[0])
    xm1 = _rows_before(x, prev8, 1)
    xm2 = _rows_before(x, prev8, 2)
    xp1 = _rows_after(x, next8, 1)
    cw = cw_ref[...]
    u = cw[0:1] * xm2 + cw[1:2] * xm1 + cw[2:3] * x + cw[3:4] * xp1 + cb_ref[...]

    gates = _dot(u.astype(BF16), wg_ref[0]) + bg_ref[0]
    rg = _sigmoid(gates[:, :D_LRU])
    ig = _sigmoid(gates[:, D_LRU:])
    log_a = (-LRU_C) * rg * sp_ref[0]
    a = jnp.exp(log_a)
    bv = jnp.sqrt(1.0 - a * a) * (ig * u)

    n_lt = D_LRU // LANES
    for c in range(n_lt):
        a_scr[c] = a[:, c * LANES:(c + 1) * LANES]
        b_scr[c] = bv[:, c * LANES:(c + 1) * LANES]
    win = SUBLANES * SUBLANES

    def scan(reverse):
        r8 = _rows((SUBLANES, D_LRU))
        carry = carry_ref[...]
        n_win = tm // win
        for w in (range(n_win - 1, -1, -1) if reverse else range(n_win)):
            base = w * win
            hs = [jnp.zeros((SUBLANES, LANES), F32)] * n_lt
            accs = [jnp.ones((SUBLANES, LANES), F32)] * n_lt
            for g in (range(SUBLANES - 1, -1, -1) if reverse else range(SUBLANES)):
                rows = pl.ds(base + g, SUBLANES, stride=SUBLANES)
                for c in range(n_lt):
                    ag = a_scr[c, rows, :]
                    hs[c] = ag * hs[c] + b_scr[c, rows, :]
                    accs[c] = ag * accs[c]
                    b_scr[c, rows, :] = hs[c]
                    a_scr[c, rows, :] = accs[c]
            h = jnp.concatenate(hs, axis=1)
            acc = jnp.concatenate(accs, axis=1)
            s = 1
            while s < SUBLANES:
                keep = (r8 >= SUBLANES - s) if reverse else (r8 < s)
                sh = SUBLANES - s if reverse else s
                h = acc * jnp.where(keep, 0.0, pltpu.roll(h, sh, 0)) + h
                acc = acc * jnp.where(keep, 1.0, pltpu.roll(acc, sh, 0))
                s *= 2
            ends = h + acc * carry
            if reverse:
                enter = jnp.where(r8 == SUBLANES - 1, carry, pltpu.roll(ends, SUBLANES - 1, 0))
                carry = ends[0:1]
            else:
                enter = jnp.where(r8 == 0, carry, pltpu.roll(ends, 1, 0))
                carry = ends[SUBLANES - 1: SUBLANES]
            for r in range(SUBLANES):
                rs = slice(base + r * SUBLANES, base + (r + 1) * SUBLANES)
                for c in range(n_lt):
                    cs = slice(c * LANES, (c + 1) * LANES)
                    o_ref[0, 0, rs, cs] = b_scr[c, rs, :] + a_scr[c, rs, :] * enter[r:r + 1, cs]
        carry_ref[...] = carry

    @pl.when(d == 0)
    def _():
        scan(False)

    @pl.when(d == 1)
    def _():
        scan(True)


def _lru_call(proj_lru, conv_w, conv_b, wg, bg, sp, tm):
    bn, t, _ = proj_lru.shape
    nt = t // tm
    r8 = tm // SUBLANES
    n8 = t // SUBLANES

    def tile(d, i):
        return d * (nt - 1) + (1 - 2 * d) * i

    return pl.pallas_call(
        functools.partial(_lru_kernel, n_tiles=nt),
        grid=(bn, 2, nt),
        in_specs=[
            pl.BlockSpec((1, tm, D_LRU), lambda b, d, i: (b, tile(d, i), 0)),
            pl.BlockSpec((1, SUBLANES, D_LRU), lambda b, d, i: (b, jnp.maximum(tile(d, i) * r8 - 1, 0), 0)),
            pl.BlockSpec((1, SUBLANES, D_LRU), lambda b, d, i: (b, jnp.minimum((tile(d, i) + 1) * r8, n8 - 1), 0)),
            pl.BlockSpec((4, D_LRU), lambda b, d, i: (0, 0)),
            pl.BlockSpec((1, D_LRU), lambda b, d, i: (0, 0)),
            pl.BlockSpec((1, D_LRU, 2 * D_LRU), lambda b, d, i: (d, 0, 0)),
            pl.BlockSpec((1, 1, 2 * D_LRU), lambda b, d, i: (d, 0, 0)),
            pl.BlockSpec((1, 1, D_LRU), lambda b, d, i: (d, 0, 0)),
        ],
        out_specs=pl.BlockSpec((1, 1, tm, D_LRU), lambda b, d, i: (d, b, tile(d, i), 0)),
        out_shape=jax.ShapeDtypeStruct((2, bn, t, D_LRU), F32),
        scratch_shapes=[pltpu.VMEM((1, D_LRU), F32),
                        pltpu.VMEM((D_LRU // LANES, tm, LANES), F32),
                        pltpu.VMEM((D_LRU // LANES, tm, LANES), F32)],
        compiler_params=_cparams(("parallel", "arbitrary", "arbitrary")),
        name="rg_lru",
    )(proj_lru, proj_lru, proj_lru, conv_w, conv_b, wg, bg, sp)


def _rwprep_kernel(z_ref, prev_ref, next_ref, mup_ref, mun_ref, w0_ref, wup_ref, a0_ref, aup_ref, gup_ref,
                   kk_ref, ka_ref, rk_ref, ones_ref,
                   r_out, kkn_out, v_out, kd_out, kka_out, lw_out, bonus_out, g_out, *, n_tiles):
    i = pl.program_id(1)
    tm = z_ref.shape[1]
    zc = z_ref[0]
    prev8 = jnp.where(i == 0, 0.0, prev_ref[0])
    next8 = jnp.where(i == n_tiles - 1, 0.0, next_ref[0])
    zp = _rows_before(zc, prev8, 1)
    zn = _rows_after(zc, next8, 1)
    z = zc + mup_ref[...] * (zp - zc) + mun_ref[...] * (zn - zc)

    r = z[:, 0:512]
    k = z[:, 512:1024]
    v = z[:, 1024:1536]
    wd = z[:, 1536:1664]
    ad = z[:, 1664:1792]
    gd = z[:, 1792:1920]
    ones_bd = ones_ref[...]

    kkr = k * kk_ref[...]
    ss = _dot_hilo(kkr * kkr, ones_bd)
    kkn = kkr / jnp.maximum(jnp.sqrt(ss), 1e-12)

    wlin = w0_ref[...] + _dot(jnp.tanh(wd).astype(BF16), wup_ref[...])
    lw = (-np.exp(-0.5).astype(np.float32)) * _sigmoid(wlin)
    a = _sigmoid(a0_ref[...] + _dot(ad.astype(BF16), aup_ref[...]))

    ka = ka_ref[...]
    k_sum = jnp.zeros_like(k)
    for d in range(2):
        a_d = a[:, d * D_RWKV:(d + 1) * D_RWKV]
        kd = k * (1.0 + (a_d - 1.0) * ka)
        kd_out[d, 0] = kd.astype(BF16)
        kka_out[d, 0] = (kkn * a_d).astype(BF16)
        lw_out[d, 0] = lw[:, d * D_RWKV:(d + 1) * D_RWKV]
        k_sum = k_sum + kd
    r_out[0] = r.astype(BF16)
    kkn_out[0] = kkn.astype(BF16)
    v_out[0] = v.astype(BF16)
    bonus_out[0] = (_dot_hilo(r * k_sum * rk_ref[...], ones_bd) * v).astype(BF16)
    g_out[0] = _dot(_sigmoid(gd).astype(BF16), gup_ref[...]).astype(BF16)


def _rwprep_call(proj_rw, mup, mun, w0, wup_bd, a0, aup_bd, gup, k_k, k_a, r_k, ones_bd, tm):
    bn, t, _ = proj_rw.shape
    nt = t // tm
    r8 = tm // SUBLANES
    n8 = t // SUBLANES
    c2 = lambda b, i: (0, 0)
    tok = pl.BlockSpec((1, tm, D_RWKV), lambda b, i: (b, i, 0))
    tok2 = pl.BlockSpec((2, 1, tm, D_RWKV), lambda b, i: (0, b, i, 0))
    s1 = jax.ShapeDtypeStruct((bn, t, D_RWKV), BF16)
    s2 = jax.ShapeDtypeStruct((2, bn, t, D_RWKV), BF16)
    s2f = jax.ShapeDtypeStruct((2, bn, t, D_RWKV), F32)
    return pl.pallas_call(
        functools.partial(_rwprep_kernel, n_tiles=nt),
        grid=(bn, nt),
        in_specs=[
            pl.BlockSpec((1, tm, D_RW_IN), lambda b, i: (b, i, 0)),
            pl.BlockSpec((1, SUBLANES, D_RW_IN), lambda b, i: (b, jnp.maximum(i * r8 - 1, 0), 0)),
            pl.BlockSpec((1, SUBLANES, D_RW_IN), lambda b, i: (b, jnp.minimum((i + 1) * r8, n8 - 1), 0)),
            pl.BlockSpec((1, D_RW_IN), c2),
            pl.BlockSpec((1, D_RW_IN), c2),
            pl.BlockSpec((1, 2 * D_RWKV), c2),
            pl.BlockSpec((LANES, 2 * D_RWKV), c2),
            pl.BlockSpec((1, 2 * D_RWKV), c2),
            pl.BlockSpec((LANES, 2 * D_RWKV), c2),
            pl.BlockSpec((LANES, D_RWKV), c2),
            pl.BlockSpec((1, D_RWKV), c2),
            pl.BlockSpec((1, D_RWKV), c2),
            pl.BlockSpec((1, D_RWKV), c2),
            pl.BlockSpec((D_RWKV, D_RWKV), c2),
        ],
        out_specs=[tok, tok, tok, tok2, tok2, tok2, tok, tok],
        out_shape=[s1, s1, s1, s2, s2, s2f, s1, s1],
        compiler_params=_cparams(("parallel", "parallel")),
        name="rwkv_prep",
    )(proj_rw, proj_rw, proj_rw, mup, mun, w0, wup_bd, a0, aup_bd, gup, k_k, k_a, r_k, ones_bd)


def _stack2(y):
    yb = y.astype(BF16)
    lo = _cols(yb.shape) < RWKV_HEAD
    zero = jnp.zeros_like(yb)
    return jnp.concatenate([jnp.where(lo, yb, zero), jnp.where(lo, zero, yb)], axis=0)


def _pair_mm(x, y):
    return _dot(x.astype(BF16), _stack2(y))


class _Masks:
    def __init__(self, reverse):
        shp = (CHUNK, PAIR)
        t = _rows(shp)
        s = _cols(shp) % CHUNK
        self.strict = (s > t) if reverse else (s < t)
        self.incl = (s >= t) if reverse else (s <= t)
        self.eye = s == t
        self.blk16 = (t // 16) == (s // 16)
        self.lvl32 = ((t // 32) == (s // 32)) & ((t // 16) != (s // 16))
        self.lvl64 = (t // 32) != (s // 32)
        sq = (PAIR, PAIR)
        self.bd = (_rows(sq) // RWKV_HEAD) == (_cols(sq) // RWKV_HEAD)
        self.eye_sq = _rows(sq) == _cols(sq)
        tt = _rows((CHUNK, CHUNK))
        ss = _cols((CHUNK, CHUNK))
        self.tri = jnp.where((ss >= tt) if reverse else (ss <= tt), 1.0, 0.0).astype(BF16)
        self.reverse = reverse


def _dot_hilo_l(w_bf16, x):
    hi, lo = _split(x)
    return _dot(w_bf16, hi) + _dot(w_bf16, lo)


def _col(x, p):
    return x[:, p * PAIR:(p + 1) * PAIR]


def _wkv_kernel(rf_ref, kkf_ref, vf_ref, kdf_ref, kkaf_ref, lwf_ref,
                rb_ref, kkb_ref, vb_ref, kdb_ref, kkab_ref, lwb_ref,
                yf_ref, yb_ref, h_ref, *, n_sub):
    @pl.when(pl.program_id(1) == 0)
    def _():
        h_ref[...] = jnp.zeros_like(h_ref)

    dirs = (
        (False, rf_ref, kkf_ref, vf_ref, kdf_ref, kkaf_ref, lwf_ref, yf_ref),
        (True, rb_ref, kkb_ref, vb_ref, kdb_ref, kkab_ref, lwb_ref, yb_ref),
    )
    masks = (_Masks(False), _Masks(True))
    chunks = [(di, j) for di in range(2) for j in range(n_sub)]
    units = [(di, j, p) for (di, j) in chunks for p in range(N_PAIRS)]

    rb, ab, bt, kt, bh, kh, vv, etot = {}, {}, {}, {}, {}, {}, {}, {}
    for c in chunks:
        di, j = c
        reverse, r_ref, kk_ref, v_ref, kd_ref, kka_ref, lw_ref, _ = dirs[di]
        rs = slice(j * CHUNK, (j + 1) * CHUNK)
        lw = lw_ref[0, 0, rs, :]
        kd = kd_ref[0, 0, rs, :].astype(F32)
        kka = kka_ref[0, 0, rs, :].astype(F32)
        cum = _dot_hilo_l(masks[di].tri, lw)
        tot = cum[0:1] if reverse else cum[CHUNK - 1: CHUNK]
        rb[c] = r_ref[0, rs, :].astype(F32) * jnp.exp(cum)
        ab[c] = -kk_ref[0, rs, :].astype(F32) * jnp.exp(cum - lw)
        ip = jnp.exp(-cum)
        bt[c] = kka * ip
        kt[c] = kd * ip
        ph = jnp.exp(tot - cum)
        bh[c] = kka * ph
        kh[c] = kd * ph
        vv[c] = v_ref[0, rs, :]
        etot[c] = jnp.exp(tot)

    def per_unit(fn):
        return {u: fn(u, (u[0], u[1]), u[2], masks[u[0]]) for u in units}

    s_all = per_unit(lambda u, c, p, m: _dot_nt(
        jnp.concatenate([_col(ab[c], p), _col(rb[c], p)], axis=0).astype(BF16),
        jnp.concatenate([_stack2(_col(bt[c], p)), _stack2(_col(kt[c], p))], axis=0)))
    n_ab = per_unit(lambda u, c, p, m: jnp.where(m.strict, s_all[u][:CHUNK, :PAIR], 0.0))
    a_ak = per_unit(lambda u, c, p, m: jnp.where(m.strict, s_all[u][:CHUNK, PAIR:], 0.0).astype(BF16))
    m_rb = per_unit(lambda u, c, p, m: jnp.where(m.incl, s_all[u][CHUNK:, :PAIR], 0.0).astype(BF16))
    m_rk = per_unit(lambda u, c, p, m: jnp.where(m.incl, s_all[u][CHUNK:, PAIR:], 0.0).astype(BF16))

    nd = per_unit(lambda u, c, p, m: jnp.where(m.blk16, n_ab[u], 0.0))
    t_inv = per_unit(lambda u, c, p, m: jnp.where(m.eye, 1.0, nd[u]))
    pw = per_unit(lambda u, c, p, m: _pair_mm(nd[u], nd[u]))
    for step in range(2):
        tp = per_unit(lambda u, c, p, m: _pair_mm(jnp.concatenate([t_inv[u], pw[u]], axis=0), pw[u]))
        t_inv = per_unit(lambda u, c, p, m: t_inv[u] + tp[u][:CHUNK])
        pw = per_unit(lambda u, c, p, m: tp[u][CHUNK:])
    t_inv = per_unit(lambda u, c, p, m: t_inv[u] + _pair_mm(t_inv[u], pw[u]))
    for lvl in ("lvl32", "lvl64"):
        tc = per_unit(lambda u, c, p, m: _pair_mm(t_inv[u], jnp.where(getattr(m, lvl), n_ab[u], 0.0)))
        t_inv = per_unit(lambda u, c, p, m: t_inv[u] + _pair_mm(tc[u], t_inv[u]))

    v2 = per_unit(lambda u, c, p, m: _stack2(_col(vv[c], p)))
    av = per_unit(lambda u, c, p, m: _dot(jnp.concatenate([a_ak[u], m_rk[u]], axis=0), v2[u]))
    akv = per_unit(lambda u, c, p, m: av[u][:CHUNK])
    wu = per_unit(lambda u, c, p, m: _dot(
        t_inv[u].astype(BF16), jnp.concatenate([_stack2(_col(ab[c], p)), _stack2(akv[u])], axis=1)))
    qy = per_unit(lambda u, c, p, m: _dot(
        m_rb[u], jnp.concatenate([_stack2(wu[u][:, :PAIR]), _stack2(wu[u][:, PAIR:])], axis=1)))
    q_hat = per_unit(lambda u, c, p, m: (_col(rb[c], p) + qy[u][:, :PAIR]).astype(BF16))
    y_loc = per_unit(lambda u, c, p, m: qy[u][:, PAIR:] + av[u][CHUNK:])
    gd = per_unit(lambda u, c, p, m: _dot(_col(bh[c], p).T.astype(BF16), wu[u].astype(BF16)))
    kv = per_unit(lambda u, c, p, m: _dot(_col(kh[c], p).T.astype(BF16), _col(vv[c], p).astype(BF16)))
    g_m = per_unit(lambda u, c, p, m: (jnp.where(m.bd, gd[u][:, :PAIR], 0.0)
                                       + jnp.where(m.eye_sq, _col(etot[c], p), 0.0)).astype(BF16))
    d_m = per_unit(lambda u, c, p, m: jnp.where(m.bd, gd[u][:, PAIR:] + kv[u], 0.0))

    h = {(di, p): h_ref[di, p] for di in range(2) for p in range(N_PAIRS)}
    for step in range(n_sub):
        for di in range(2):
            j = n_sub - 1 - step if dirs[di][0] else step
            y_ref = dirs[di][7]
            rs = slice(j * CHUNK, (j + 1) * CHUNK)
            for p in range(N_PAIRS):
                u = (di, j, p)
                hb = h[(di, p)].astype(BF16)
                qh = _dot(jnp.concatenate([q_hat[u], g_m[u]], axis=0), hb)
                y_ref[0, rs, p * PAIR:(p + 1) * PAIR] = (qh[:CHUNK] + y_loc[u]).astype(BF16)
                h[(di, p)] = qh[CHUNK:] + d_m[u]
    for di in range(2):
        for p in range(N_PAIRS):
            h_ref[di, p] = h[(di, p)]


def _wkv_call(r, kkn, v, kd, kka, lw, tm):
    bn, t, _ = r.shape
    nt = t // tm
    fwd = pl.BlockSpec((1, tm, D_RWKV), lambda b, i: (b, i, 0))
    bwd = pl.BlockSpec((1, tm, D_RWKV), lambda b, i: (b, nt - 1 - i, 0))
    fwd2 = pl.BlockSpec((1, 1, tm, D_RWKV), lambda b, i: (0, b, i, 0))
    bwd2 = pl.BlockSpec((1, 1, tm, D_RWKV), lambda b, i: (1, b, nt - 1 - i, 0))
    s1 = jax.ShapeDtypeStruct((bn, t, D_RWKV), BF16)
    return pl.pallas_call(
        functools.partial(_wkv_kernel, n_sub=tm // CHUNK),
        grid=(bn, nt),
        in_specs=[fwd, fwd, fwd, fwd2, fwd2, fwd2, bwd, bwd, bwd, bwd2, bwd2, bwd2],
        out_specs=[fwd, bwd],
        out_shape=[s1, s1],
        scratch_shapes=[pltpu.VMEM((2, N_PAIRS, PAIR, PAIR), F32)],
        compiler_params=_cparams(("parallel", "arbitrary")),
        name="wkv7_chunked",
    )(r, kkn, v, kd, kka, lw, r, kkn, v, kd, kka, lw)


def _gelu_tanh(x):
    return 0.5 * x * (1.0 + jnp.tanh(0.7978845608028654 * (x + 0.044715 * (x * x * x))))


def _route(logits_t, b_col):
    rows = [logits_t[e:e + 1] for e in range(N_EXPERTS)]
    mx = functools.reduce(jnp.maximum, rows)
    ex = [jnp.exp(x - mx) for x in rows]
    den = functools.reduce(lambda a, b: a + b, ex)
    probs = [e / den for e in ex]
    sel = [probs[e] + b_col[e:e + 1] for e in range(N_EXPERTS)]
    scores = []
    for g in range(N_GROUPS):
        a, b, c, d = sel[4 * g: 4 * g + 4]
        hi1, lo1 = jnp.maximum(a, b), jnp.minimum(a, b)
        hi2, lo2 = jnp.maximum(c, d), jnp.minimum(c, d)
        scores.append(jnp.maximum(hi1, hi2) + jnp.maximum(jnp.minimum(hi1, hi2), jnp.maximum(lo1, lo2)))
    best = scores[0]
    bg = jnp.zeros_like(best)
    for g in range(1, N_GROUPS):
        upd = scores[g] > best
        best = jnp.where(upd, scores[g], best)
        bg = jnp.where(upd, float(g), bg)

    def pick(vals):
        out = []
        for j in range(EXP_PER_GROUP):
            x = vals[j]
            for g in range(1, N_GROUPS):
                x = jnp.where(bg == float(g), vals[4 * g + j], x)
            out.append(x)
        return out

    sg = pick(sel)
    v1, i1 = sg[0], jnp.zeros_like(best)
    for j in range(1, EXP_PER_GROUP):
        upd = sg[j] > v1
        v1 = jnp.where(upd, sg[j], v1)
        i1 = jnp.where(upd, float(j), i1)
    neg = jnp.full_like(best, -jnp.inf)
    v2, i2 = neg, jnp.zeros_like(best)
    for j in range(EXP_PER_GROUP):
        cand = jnp.where(i1 == float(j), neg, sg[j])
        upd = cand > v2
        v2 = jnp.where(upd, cand, v2)
        i2 = jnp.where(upd, float(j), i2)
    return bg * float(EXP_PER_GROUP) + i1, bg * float(EXP_PER_GROUP) + i2


def _outproj_kernel(hl_ref, gate_ref, yf_ref, yb_ref, bonus_ref, g_ref, xa_ref, xb_ref, lnw_ref, lnb_ref, g1_ref,
                    sh2_ref, sc2_ref, n2_ref, wout_ref, wr_ref, br_ref, ones_ref,
                    xn_ref, h2_ref, route_ref, *, n_a):
    lru_out = (hl_ref[0, 0] + hl_ref[1, 0]) * _gelu_tanh(gate_ref[0])
    ones_bd = ones_ref[...]
    y = yf_ref[0].astype(F32) + yb_ref[0].astype(F32)
    mu = _dot_hilo(y, ones_bd) * (1.0 / RWKV_HEAD)
    yc = y - mu
    var = _dot_hilo(yc * yc, ones_bd) * (1.0 / RWKV_HEAD)
    gn = yc * lax.rsqrt(var + GN_EPS) * lnw_ref[...] + lnb_ref[...]
    rw_out = (gn + bonus_ref[0].astype(F32)) * g_ref[0].astype(F32)
    o = _dot(lru_out.astype(BF16), wout_ref[:D_LRU]) + _dot(rw_out.astype(BF16), wout_ref[D_LRU:])
    xn = _x_block(xa_ref, xb_ref, n_a) + g1_ref[0] * o
    xn_ref[0] = xn
    h2 = _norm_mod(xn, n2_ref[...], sc2_ref[0], sh2_ref[0])
    for j in range(TOK_ROWS):
        h2_ref[0, pl.ds(j, xn.shape[0], stride=TOK_ROWS), :] = h2[:, j * LANES:(j + 1) * LANES]
    logits_t = _dot3_nt(wr_ref[...], h2)
    e0, e1 = _route(logits_t, br_ref[...])
    zero = jnp.zeros_like(e0)
    route_ref[0, 0] = jnp.concatenate([e0, e1, zero, zero, zero, zero, zero, zero], axis=0)


def _outproj_call(hl, proj_lru, yf, yb, bonus, g, xa, xb, lnw, lnb, g1, sh2, sc2, n2, w_out_bf16, wr_t, br, ones_bd,
                  tm):
    n_a, t, d = xa.shape
    bn = g1.shape[0]
    nt = t // tm
    spec_a, spec_b = _x_specs(tm, d, n_a, nt)
    c2 = lambda b, i: (0, 0)
    tok = pl.BlockSpec((1, tm, D_RWKV), lambda b, i: (b, i, 0))
    tokd = pl.BlockSpec((1, tm, d), lambda b, i: (b, i, 0))
    per_b = pl.BlockSpec((1, 1, d), lambda b, i: (b, 0, 0))
    return pl.pallas_call(
        functools.partial(_outproj_kernel, n_a=n_a),
        grid=(bn, nt),
        in_specs=[
            pl.BlockSpec((2, 1, tm, D_LRU), lambda b, i: (0, b, i, 0)),
            pl.BlockSpec((1, tm, D_LRU), lambda b, i: (b, i, 1)),
            tok, tok, tok, tok, spec_a, spec_b,
            pl.BlockSpec((1, D_RWKV), c2),
            pl.BlockSpec((1, D_RWKV), c2),
            per_b, per_b, per_b,
            pl.BlockSpec((1, d), c2),
            pl.BlockSpec((d, d), c2),
            pl.BlockSpec((N_EXPERTS, d), c2),
            pl.BlockSpec((N_EXPERTS, 1), c2),
            pl.BlockSpec((D_RWKV, D_RWKV), c2),
        ],
        out_specs=[tokd, pl.BlockSpec((1, tm * TOK_ROWS, LANES), lambda b, i: (b, i, 0)),
                   pl.BlockSpec((1, 1, SUBLANES, tm), lambda b, i: (b, i, 0, 0))],
        out_shape=[
            jax.ShapeDtypeStruct((bn, t, d), F32),
            jax.ShapeDtypeStruct((bn, t * TOK_ROWS, LANES), F32),
            jax.ShapeDtypeStruct((bn, nt, SUBLANES, tm), F32),
        ],
        compiler_params=_cparams(("parallel", "parallel")),
        name="outproj_router",
    )(hl, proj_lru, yf, yb, bonus, g, xa, xb, lnw, lnb, g1, sh2, sc2, n2, w_out_bf16, wr_t, br, ones_bd)


def _row_copy(src_ref, src_row, dst_ref, dst_row, sem):
    src = src_ref.at[pl.ds(pl.multiple_of(src_row * TOK_ROWS, TOK_ROWS), TOK_ROWS)]
    dst = dst_ref.at[pl.ds(pl.multiple_of(dst_row * TOK_ROWS, TOK_ROWS), TOK_ROWS)]
    return pltpu.make_async_copy(src, dst, sem)


def _dispatch_kernel(pos_ref, h_ref, xb_in_ref, xb_ref, sem):
    del xb_in_ref
    tm = h_ref.shape[0] // TOK_ROWS

    def issue(r, carry):
        _row_copy(h_ref, r, xb_ref, pos_ref[0, 0, r], sem).start()
        return carry

    lax.fori_loop(0, tm, issue, 0, unroll=8)

    def drain(r, carry):
        _row_copy(h_ref, 0, xb_ref, 0, sem).wait()
        return carry

    lax.fori_loop(0, tm, drain, 0, unroll=True)


def _dispatch_call(h2, pos_tiles, xb_init, tm):
    n = h2.shape[0] // TOK_ROWS
    nt = n // tm
    return pl.pallas_call(
        _dispatch_kernel,
        grid=(nt,),
        in_specs=[
            pl.BlockSpec((1, 1, tm), lambda i: (i, 0, 0), memory_space=pltpu.SMEM),
            pl.BlockSpec((tm * TOK_ROWS, LANES), lambda i: (i, 0)),
            pl.BlockSpec(memory_space=pl.ANY),
        ],
        out_specs=pl.BlockSpec(memory_space=pl.ANY),
        out_shape=jax.ShapeDtypeStruct(xb_init.shape, F32),
        scratch_shapes=[pltpu.SemaphoreType.DMA(())],
        input_output_aliases={2: 0},
        compiler_params=_cparams(("arbitrary",)),
        name="moe_dispatch",
    )(pos_tiles, h2, xb_init)


def _expert_kernel(blk_a_ref, blk_b_ref, n_used_ref, x_ref, wr_ref,
                   wga_ref, wua_ref, wda_ref, wgb_ref, wub_ref, wdb_ref, o_ref):
    i = pl.program_id(0)

    @pl.when(i < n_used_ref[0])
    def _():
        x32 = jnp.concatenate([x_ref[pl.ds(j, MOE_BLOCK, stride=TOK_ROWS), :] for j in range(TOK_ROWS)], axis=1)
        x = x32.astype(BF16)
        w_diff = wr_ref[pl.ds(blk_a_ref[i], 1), :] - wr_ref[pl.ds(blk_b_ref[i], 1), :]
        l_diff = jnp.sum(x32 * w_diff, axis=-1, keepdims=True)
        g_a = 1.0 / (1.0 + jnp.exp(-l_diff))
        g_b = 1.0 / (1.0 + jnp.exp(l_diff))

        def ffn(wg_ref, wu_ref, wd_ref):
            gate = _dot(x, wg_ref[0])
            hid = gate * _sigmoid(gate) * _dot(x, wu_ref[0])
            return _dot(hid.astype(BF16), wd_ref[0])

        y = g_a * ffn(wga_ref, wua_ref, wda_ref) + g_b * ffn(wgb_ref, wub_ref, wdb_ref)
        for j in range(TOK_ROWS):
            o_ref[pl.ds(j, MOE_BLOCK, stride=TOK_ROWS), :] = y[:, j * LANES:(j + 1) * LANES]

    @pl.when(i >= n_used_ref[0])
    def _():
        o_ref[...] = jnp.zeros_like(o_ref)


def _expert_call(xb, blk_a, blk_b, n_used, wr_t, wg, wu, wd):
    n_blk = xb.shape[0] // (MOE_BLOCK * TOK_ROWS)
    d = D_MODEL
    w_in_a = pl.BlockSpec((1, d, D_EXPERT), lambda i, ba, bb, nu: (ba[i], 0, 0))
    w_in_b = pl.BlockSpec((1, d, D_EXPERT), lambda i, ba, bb, nu: (bb[i], 0, 0))
    grid_spec = pltpu.PrefetchScalarGridSpec(
        num_scalar_prefetch=3,
        grid=(n_blk,),
        in_specs=[
            pl.BlockSpec((MOE_BLOCK * TOK_ROWS, LANES), lambda i, ba, bb, nu: (i, 0)),
            pl.BlockSpec((N_EXPERTS, d), lambda i, ba, bb, nu: (0, 0)),
            w_in_a, w_in_a,
            pl.BlockSpec((1, D_EXPERT, d), lambda i, ba, bb, nu: (ba[i], 0, 0)),
            w_in_b, w_in_b,
            pl.BlockSpec((1, D_EXPERT, d), lambda i, ba, bb, nu: (bb[i], 0, 0)),
        ],
        out_specs=pl.BlockSpec((MOE_BLOCK * TOK_ROWS, LANES), lambda i, ba, bb, nu: (i, 0)),
    )
    return pl.pallas_call(
        _expert_kernel,
        grid_spec=grid_spec,
        out_shape=jax.ShapeDtypeStruct(xb.shape, F32),
        compiler_params=_cparams(("arbitrary",)),
        name="moe_experts",
    )(blk_a, blk_b, n_used, xb, wr_t, wg, wu, wd, wg, wu, wd)


def _combine_kernel(pos_ref, posn_ref, x_ref, g2_ref, nf_ref, yb_ref, *rest, n_first):
    *o_refs, buf_ref, sem = rest
    i = pl.program_id(0)
    n = pl.num_programs(0)
    tm = x_ref.shape[0]
    slot = i % 2

    def gather(p_ref, s):
        def issue(r, carry):
            _row_copy(yb_ref, p_ref[0, 0, r], buf_ref.at[s], r, sem.at[s]).start()
            return carry

        lax.fori_loop(0, tm, issue, 0, unroll=8)

    @pl.when(i == 0)
    def _():
        gather(pos_ref, 0)

    @pl.when(i + 1 < n)
    def _():
        gather(posn_ref, 1 - slot)

    def drain(r, carry):
        _row_copy(yb_ref, 0, buf_ref.at[slot], 0, sem.at[slot]).wait()
        return carry

    lax.fori_loop(0, tm, drain, 0, unroll=True)
    for j in range(TOK_ROWS):
        cs = slice(j * LANES, (j + 1) * LANES)
        y = buf_ref[slot, pl.ds(j, tm, stride=TOK_ROWS), :]
        o_refs[-1][:, cs] = x_ref[:, cs] + g2_ref[0, :, cs] * y
    if n_first is not None:
        o_first, o_second = o_refs
        xo = o_second[...]
        ms = jnp.mean(xo * xo, axis=-1, keepdims=True)
        res = xo * lax.rsqrt(ms + NORM_EPS) * nf_ref[...]
        o_second[...] = res

        @pl.when(i < n_first)
        def _():
            o_first[...] = res


def _combine_call(xn, yb, pos_tiles, g2, nf, t, tm, n_first=None):
    n, d = xn.shape
    nt = n // tm
    per_b = t // tm
    if n_first is None:
        out_specs = pl.BlockSpec((tm, d), lambda i: (i, 0))
        out_shape = jax.ShapeDtypeStruct((n, d), F32)
    else:
        out_specs = [pl.BlockSpec((tm, d), lambda i: (jnp.minimum(i, n_first - 1), 0)),
                     pl.BlockSpec((tm, d), lambda i: (jnp.maximum(i - n_first, 0), 0))]
        out_shape = [jax.ShapeDtypeStruct((n_first * tm, d), F32), jax.ShapeDtypeStruct((n - n_first * tm, d), F32)]
    return pl.pallas_call(
        functools.partial(_combine_kernel, n_first=n_first),
        grid=(nt,),
        in_specs=[
            pl.BlockSpec((1, 1, tm), lambda i: (i, 0, 0), memory_space=pltpu.SMEM),
            pl.BlockSpec((1, 1, tm), lambda i: (jnp.minimum(i + 1, nt - 1), 0, 0), memory_space=pltpu.SMEM),
            pl.BlockSpec((tm, d), lambda i: (i, 0)),
            pl.BlockSpec((1, 1, d), lambda i: (i // per_b, 0, 0)),
            pl.BlockSpec((1, d), lambda i: (0, 0)),
            pl.BlockSpec(memory_space=pl.ANY),
        ],
        out_specs=out_specs,
        out_shape=out_shape,
        scratch_shapes=[pltpu.VMEM((2, tm * TOK_ROWS, LANES), F32), pltpu.SemaphoreType.DMA((2,))],
        compiler_params=_cparams(("arbitrary",)),
        name="moe_combine",
    )(pos_tiles, pos_tiles, xn, g2, nf, yb)


def _block_diag(w):
    h, a, b = w.shape
    eye = jnp.eye(h, dtype=w.dtype)
    return jnp.einsum("hab,hg->hagb", w, eye).reshape(h * a, h * b)


def _head_ones():
    idx = np.arange(D_RWKV) // RWKV_HEAD
    return jnp.asarray((idx[:, None] == idx[None, :]).astype(np.float32), dtype=BF16)


_PAIRS = [(i, j) for i in range(EXP_PER_GROUP) for j in range(i + 1, EXP_PER_GROUP)]
N_CLASSES = N_GROUPS * len(_PAIRS)


def _routing_tables(route, n_tok, tm):
    flat = jnp.transpose(route, (2, 0, 1, 3)).reshape(SUBLANES, n_tok)
    e0 = flat[0].astype(jnp.int32)
    e1 = flat[1].astype(jnp.int32)
    ea = jnp.minimum(e0, e1)
    eb = jnp.maximum(e0, e1)
    i = ea % EXP_PER_GROUP
    j = eb % EXP_PER_GROUP
    cls = (ea // EXP_PER_GROUP) * len(_PAIRS) + (i * (2 * EXP_PER_GROUP - 1 - i)) // 2 + (j - i - 1)
    ar = jnp.arange(N_CLASSES, dtype=jnp.int32)
    oh = (cls[:, None] == ar).astype(jnp.int32)
    cs = jnp.cumsum(oh, axis=0)
    counts = cs[-1]
    padded = (counts + MOE_BLOCK - 1) // MOE_BLOCK * MOE_BLOCK
    pad_end = jnp.cumsum(padded)
    base = (cs - oh) + (pad_end - padded)[None, :]
    pos = jnp.take_along_axis(base, cls[:, None], axis=1)[:, 0]
    n_blk = (n_tok + N_CLASSES * (MOE_BLOCK - 1) + MOE_BLOCK - 1) // MOE_BLOCK
    blk_start = jnp.arange(n_blk, dtype=jnp.int32) * MOE_BLOCK
    blk_cls = jnp.minimum(jnp.sum(pad_end[None, :] <= blk_start[:, None], axis=-1), N_CLASSES - 1)
    cls_a = np.array([g * EXP_PER_GROUP + p[0] for g in range(N_GROUPS) for p in _PAIRS], np.int32)
    cls_b = np.array([g * EXP_PER_GROUP + p[1] for g in range(N_GROUPS) for p in _PAIRS], np.int32)
    blk_a = jnp.asarray(cls_a)[blk_cls]
    blk_b = jnp.asarray(cls_b)[blk_cls]
    n_used = (pad_end[-1] // MOE_BLOCK).astype(jnp.int32).reshape(1)
    pos_tiles = pos.reshape(n_tok // tm, 1, tm).astype(jnp.int32)
    return pos_tiles, blk_a, blk_b, n_used, n_blk * MOE_BLOCK


def _tiles(t):
    want = dict(inproj=512, lru=512, prep=512, wkv=512, moe=512)
    return {k: min(t, v) for k, v in want.items()}


def _trunk(x_a, x_b, c, w_mod, b_mod, norm1, norm2, w_in, w_out, conv_w, conv_b, lru_wa, lru_ba, lru_wx, lru_bx, lru_lam,
           mu_prev, mu_next, rw_w0, rw_wup, rw_a0, rw_aup, rw_gup, rw_kk, rw_ka, rw_rk, ln_x_w, ln_x_b,
           w_router, b_router, exp_gate, exp_up, exp_down, norm_f):
    n_first, t, d = x_a.shape
    bn = n_first + x_b.shape[0]
    n_tok = bn * t
    depth = w_mod.shape[0]
    ones_bd = _head_ones()
    mod = _mod_call(c, w_mod, b_mod)
    wr_t = jnp.transpose(w_router)
    br = b_router.reshape(N_EXPERTS, 1)
    tiles = _tiles(t)
    tm_moe = tiles["moe"]
    xb = None

    for l in range(depth):
        sh1, sc1, g1, sh2, sc2, g2 = [m.reshape(bn, 1, d) for m in jnp.split(mod[l], 6, axis=-1)]
        proj_lru, proj_rw = _inproj_call(x_a, x_b, sh1, sc1, norm1[l].reshape(1, d), w_in[l].astype(BF16),
                                         tiles["inproj"])

        wg = jnp.stack([jnp.concatenate([_block_diag(lru_wa[l, dd]), _block_diag(lru_wx[l, dd])], axis=1)
                        for dd in range(2)]).astype(BF16)
        bg = jnp.concatenate([lru_ba[l], lru_bx[l]], axis=1).reshape(2, 1, 2 * D_LRU)
        sp = jax.nn.softplus(-lru_lam[l]).reshape(2, 1, D_LRU)
        hl = _lru_call(proj_lru, conv_w[l], conv_b[l].reshape(1, D_LRU), wg, bg, sp, tiles["lru"])

        zeros = jnp.zeros((64, D_RWKV), F32)
        wup_bd = jnp.concatenate([jnp.concatenate([rw_wup[l, 0], zeros], axis=1),
                                  jnp.concatenate([zeros, rw_wup[l, 1]], axis=1)], axis=0).astype(BF16)
        aup_bd = jnp.concatenate([jnp.concatenate([rw_aup[l, 0], zeros], axis=1),
                                  jnp.concatenate([zeros, rw_aup[l, 1]], axis=1)], axis=0).astype(BF16)
        r, kkn, v, kd, kka, lw, bonus, g = _rwprep_call(
            proj_rw, mu_prev[l].reshape(1, -1), mu_next[l].reshape(1, -1),
            rw_w0[l].reshape(1, -1), wup_bd, rw_a0[l].reshape(1, -1), aup_bd, rw_gup[l].astype(BF16),
            rw_kk[l].reshape(1, -1), rw_ka[l].reshape(1, -1), rw_rk[l].reshape(1, -1), ones_bd, tiles["prep"])
        yf, yb = _wkv_call(r, kkn, v, kd, kka, lw, tiles["wkv"])

        xn, h2, route = _outproj_call(
            hl, proj_lru, yf, yb, bonus, g, x_a, x_b, ln_x_w[l].reshape(1, -1), ln_x_b[l].reshape(1, -1),
            g1, sh2, sc2, norm2[l].reshape(1, d), w_out[l].astype(BF16), wr_t, br, ones_bd, tm_moe)

        pos_tiles, blk_a, blk_b, n_used, n_rows = _routing_tables(route, n_tok, tm_moe)
        xb_init = jnp.zeros((n_rows * TOK_ROWS, LANES), F32) if xb is None else xb
        xb = _dispatch_call(h2.reshape(n_tok * TOK_ROWS, LANES), pos_tiles, xb_init, tm_moe)
        ybuf = _expert_call(xb, blk_a, blk_b, n_used, wr_t, exp_gate[l].astype(BF16), exp_up[l].astype(BF16),
                            exp_down[l].astype(BF16))
        last = l == depth - 1
        out = _combine_call(xn.reshape(n_tok, d), ybuf, pos_tiles, g2, norm_f.reshape(1, d), t, tm_moe,
                            n_first=n_first * (t // tm_moe) if last else None)
        if not last:
            x_a = x_b = out.reshape(bn, t, d)
    return out[0].reshape(n_first, t, d), out[1].reshape(bn - n_first, t, d)


def kernel(x_prompt, x_sample, c_prompt, c_sample, w_mod, b_mod, norm1, norm2, w_in, w_out, conv_w, conv_b, lru_wa, lru_ba, lru_wx, lru_bx, lru_lam, mu_prev, mu_next, rw_w0, rw_wup, rw_a0, rw_aup, rw_gup, rw_kk, rw_ka, rw_rk, ln_x_w, ln_x_b, w_router, b_router, exp_gate, exp_up, exp_down, norm_f):
    c = jnp.concatenate([c_prompt, c_sample], axis=0).astype(F32)
    return _trunk(x_prompt, x_sample, c, w_mod, b_mod, norm1, norm2, w_in, w_out, conv_w, conv_b, lru_wa, lru_ba,
                  lru_wx, lru_bx, lru_lam, mu_prev, mu_next, rw_w0, rw_wup, rw_a0, rw_aup, rw_gup, rw_kk, rw_ka,
                  rw_rk, ln_x_w, ln_x_b, w_router, b_router, exp_gate, exp_up, exp_down, norm_f)
```

```python
import functools

import jax
import jax.numpy as jnp
import numpy as np
from jax import lax
from jax.experimental import pallas as pl
from jax.experimental.pallas import tpu as pltpu

F32 = jnp.float32
BF16 = jnp.bfloat16

D_MODEL = 1024
D_LRU = 512
D_RWKV = 512
LRU_HEADS = 8
LRU_C = 8.0
RWKV_HEAD = 64
N_EXPERTS = 16
N_GROUPS = 4
EXP_PER_GROUP = 4
D_EXPERT = 512
MOE_BLOCK = 512
NORM_EPS = 1e-6
GN_EPS = 64e-5
W_LORA = 64
A_LORA = 64
G_LORA = 128
RW_OFFSETS = tuple(int(o) for o in np.cumsum([0, D_RWKV, D_RWKV, D_RWKV, 2 * W_LORA, 2 * A_LORA, G_LORA]))
D_RW_IN = RW_OFFSETS[-1]
D_IN = 2 * D_LRU + D_RW_IN

LANES = 128
SUBLANES = 8
CHUNK = 64
PAIR = 2 * RWKV_HEAD
N_PAIRS = D_RWKV // PAIR
TOK_ROWS = D_MODEL // LANES


def _cparams(sem, vmem_mb=48):
    return pltpu.CompilerParams(dimension_semantics=sem, vmem_limit_bytes=vmem_mb * 1024 * 1024)


def _dot(a, b):
    return jnp.dot(a, b, preferred_element_type=F32)


def _dot_nt(a, b):
    return lax.dot_general(a, b, (((1,), (1,)), ((), ())), preferred_element_type=F32)


def _split(x):
    hi = x.astype(BF16)
    lo = (x - hi.astype(F32)).astype(BF16)
    return hi, lo


def _dot_hilo(x, w_bf16):
    hi, lo = _split(x)
    return _dot(hi, w_bf16) + _dot(lo, w_bf16)


def _dot3(a, b):
    ah, al = _split(a)
    bh, bl = _split(b)
    return _dot(ah, bh) + (_dot(ah, bl) + _dot(al, bh))


def _dot3_nt(a, b):
    ah, al = _split(a)
    bh, bl = _split(b)
    return _dot_nt(ah, bh) + (_dot_nt(ah, bl) + _dot_nt(al, bh))


def _sigmoid(x):
    return 1.0 / (1.0 + jnp.exp(-x))


def _rows_before(x, halo8, k):
    rolled = pltpu.roll(x, k, 0)
    head = rolled[0:SUBLANES]
    r8 = _rows(head.shape)
    for j in range(k):
        head = jnp.where(r8 == j, halo8[SUBLANES - k + j: SUBLANES - k + j + 1], head)
    return jnp.concatenate([head, rolled[SUBLANES:]], axis=0)


def _rows_after(x, halo8, k):
    n = x.shape[0]
    rolled = pltpu.roll(x, n - k, 0)
    tail = rolled[n - SUBLANES:]
    r8 = _rows(tail.shape)
    for j in range(k):
        tail = jnp.where(r8 == SUBLANES - k + j, halo8[j: j + 1], tail)
    return jnp.concatenate([rolled[: n - SUBLANES], tail], axis=0)


def _rows(shape):
    return lax.broadcasted_iota(jnp.int32, shape, 0)


def _cols(shape):
    return lax.broadcasted_iota(jnp.int32, shape, 1)


def _mod_kernel(c_ref, w_ref, b_ref, o_ref):
    c = c_ref[...]
    cs = c * _sigmoid(c)
    o_ref[0] = _dot3(cs, w_ref[0]) + b_ref[0]


def _mod_call(c, w_mod, b_mod):
    nl, d, n6 = w_mod.shape
    bn = c.shape[0]
    tn = 1536
    return pl.pallas_call(
        _mod_kernel,
        grid=(nl, n6 // tn),
        in_specs=[
            pl.BlockSpec((bn, d), lambda l, j: (0, 0)),
            pl.BlockSpec((1, d, tn), lambda l, j: (l, 0, j)),
            pl.BlockSpec((1, 1, tn), lambda l, j: (l, 0, j)),
        ],
        out_specs=pl.BlockSpec((1, bn, tn), lambda l, j: (l, 0, j)),
        out_shape=jax.ShapeDtypeStruct((nl, bn, n6), F32),
        compiler_params=_cparams(("arbitrary", "arbitrary")),
        name="adaln_mod",
    )(c, w_mod, b_mod.reshape(nl, 1, n6))


def _norm_mod(x, nw, sc, sh):
    ms = jnp.mean(x * x, axis=-1, keepdims=True)
    return (x * lax.rsqrt(ms + NORM_EPS) * nw) * (1.0 + sc) + sh


def _x_specs(tm, d, n_a, nt):
    spec_a = pl.BlockSpec((1, tm, d), lambda b, i: (jnp.minimum(b, n_a - 1), jnp.where(b < n_a, i, nt - 1), 0))
    spec_b = pl.BlockSpec((1, tm, d), lambda b, i: (jnp.maximum(b - n_a, 0), jnp.where(b < n_a, 0, i), 0))
    return spec_a, spec_b


def _x_block(xa_ref, xb_ref, n_a):
    return jnp.where(pl.program_id(0) < n_a, xa_ref[0], xb_ref[0])


def _inproj_kernel(xa_ref, xb_ref, sh_ref, sc_ref, nw_ref, w_ref, lru_ref, rw_ref, *, n_a):
    h = _norm_mod(_x_block(xa_ref, xb_ref, n_a), nw_ref[...], sc_ref[0], sh_ref[0]).astype(BF16)
    lru_ref[0] = _dot(h, w_ref[:, : 2 * D_LRU])
    rw_ref[0] = _dot(h, w_ref[:, 2 * D_LRU:])


def _inproj_call(xa, xb, sh, sc, nw, w_in_bf16, tm):
    n_a, t, d = xa.shape
    bn = sh.shape[0]
    nt = t // tm
    spec_a, spec_b = _x_specs(tm, d, n_a, nt)
    return pl.pallas_call(
        functools.partial(_inproj_kernel, n_a=n_a),
        grid=(bn, nt),
        in_specs=[
            spec_a, spec_b,
            pl.BlockSpec((1, 1, d), lambda b, i: (b, 0, 0)),
            pl.BlockSpec((1, 1, d), lambda b, i: (b, 0, 0)),
            pl.BlockSpec((1, d), lambda b, i: (0, 0)),
            pl.BlockSpec((d, D_IN), lambda b, i: (0, 0)),
        ],
        out_specs=[
            pl.BlockSpec((1, tm, 2 * D_LRU), lambda b, i: (b, i, 0)),
            pl.BlockSpec((1, tm, D_RW_IN), lambda b, i: (b, i, 0)),
        ],
        out_shape=[
            jax.ShapeDtypeStruct((bn, t, 2 * D_LRU), F32),
            jax.ShapeDtypeStruct((bn, t, D_RW_IN), F32),
        ],
        compiler_params=_cparams(("parallel", "parallel")),
        name="norm_inproj",
    )(xa, xb, sh, sc, nw, w_in_bf16)


def _lru_kernel(x_ref, prev_ref, next_ref, cw_ref, cb_ref, wg_ref, bg_ref, sp_ref, o_ref,
                carry_ref, a_scr, b_scr, *, n_tiles):
    d = pl.program_id(1)
    i = pl.program_id(2)
    ti = d * (n_tiles - 1) + (1 - 2 * d) * i
    tm = x_ref.shape[1]

    @pl.when(i == 0)
    def _():
        carry_ref[...] = jnp.zeros_like(carry_ref)

    x = x_ref[0]
    prev8 = jnp.where(ti == 0, 0.0, prev_ref[0])
    next8 = jnp.where(ti == n_tiles - 1, 0.0, next_ref[0])
    xm1 = _rows_before(x, prev8, 1)
    xm2 = _rows_before(x, prev8, 2)
    xp1 = _rows_after(x, next8, 1)
    cw = cw_ref[...]
    u = cw[0:1] * xm2 + cw[1:2] * xm1 + cw[2:3] * x + cw[3:4] * xp1 + cb_ref[...]

    gates = _dot(u.astype(BF16), wg_ref[0]) + bg_ref[0]
    rg = _sigmoid(gates[:, :D_LRU])
    ig = _sigmoid(gates[:, D_LRU:])
    log_a = (-LRU_C) * rg * sp_ref[0]
    a = jnp.exp(log_a)
    bv = jnp.sqrt(1.0 - a * a) * (ig * u)

    n_lt = D_LRU // LANES
    for c in range(n_lt):
        a_scr[c] = a[:, c * LANES:(c + 1) * LANES]
        b_scr[c] = bv[:, c * LANES:(c + 1) * LANES]
    win = SUBLANES * SUBLANES

    def scan(reverse):
        r8 = _rows((SUBLANES, D_LRU))
        carry = carry_ref[...]
        n_win = tm // win
        for w in (range(n_win - 1, -1, -1) if reverse else range(n_win)):
            base = w * win
            hs = [jnp.zeros((SUBLANES, LANES), F32)] * n_lt
            accs = [jnp.ones((SUBLANES, LANES), F32)] * n_lt
            for g in (range(SUBLANES - 1, -1, -1) if reverse else range(SUBLANES)):
                rows = pl.ds(base + g, SUBLANES, stride=SUBLANES)
                for c in range(n_lt):
                    ag = a_scr[c, rows, :]
                    hs[c] = ag * hs[c] + b_scr[c, rows, :]
                    accs[c] = ag * accs[c]
                    b_scr[c, rows, :] = hs[c]
                    a_scr[c, rows, :] = accs[c]
            h = jnp.concatenate(hs, axis=1)
            acc = jnp.concatenate(accs, axis=1)
            s = 1
            while s < SUBLANES:
                keep = (r8 >= SUBLANES - s) if reverse else (r8 < s)
                sh = SUBLANES - s if reverse else s
                h = acc * jnp.where(keep, 0.0, pltpu.roll(h, sh, 0)) + h
                acc = acc * jnp.where(keep, 1.0, pltpu.roll(acc, sh, 0))
                s *= 2
            ends = h + acc * carry
            if reverse:
                enter = jnp.where(r8 == SUBLANES - 1, carry, pltpu.roll(ends, SUBLANES - 1, 0))
                carry = ends[0:1]
            else:
                enter = jnp.where(r8 == 0, carry, pltpu.roll(ends, 1, 0))
                carry = ends[SUBLANES - 1: SUBLANES]
            for r in range(SUBLANES):
                rs = slice(base + r * SUBLANES, base + (r + 1) * SUBLANES)
                for c in range(n_lt):
                    cs = slice(c * LANES, (c + 1) * LANES)
                    o_ref[0, 0, rs, cs] = b_scr[c, rs, :] + a_scr[c, rs, :] * enter[r:r + 1, cs]
        carry_ref[...] = carry

    @pl.when(d == 0)
    def _():
        scan(False)

    @pl.when(d == 1)
    def _():
        scan(True)


def _lru_call(proj_lru, conv_w, conv_b, wg, bg, sp, tm):
    bn, t, _ = proj_lru.shape
    nt = t // tm
    r8 = tm // SUBLANES
    n8 = t // SUBLANES

    def tile(d, i):
        return d * (nt - 1) + (1 - 2 * d) * i

    return pl.pallas_call(
        functools.partial(_lru_kernel, n_tiles=nt),
        grid=(bn, 2, nt),
        in_specs=[
            pl.BlockSpec((1, tm, D_LRU), lambda b, d, i: (b, tile(d, i), 0)),
            pl.BlockSpec((1, SUBLANES, D_LRU), lambda b, d, i: (b, jnp.maximum(tile(d, i) * r8 - 1, 0), 0)),
            pl.BlockSpec((1, SUBLANES, D_LRU), lambda b, d, i: (b, jnp.minimum((tile(d, i) + 1) * r8, n8 - 1), 0)),
            pl.BlockSpec((4, D_LRU), lambda b, d, i: (0, 0)),
            pl.BlockSpec((1, D_LRU), lambda b, d, i: (0, 0)),
            pl.BlockSpec((1, D_LRU, 2 * D_LRU), lambda b, d, i: (d, 0, 0)),
            pl.BlockSpec((1, 1, 2 * D_LRU), lambda b, d, i: (d, 0, 0)),
            pl.BlockSpec((1, 1, D_LRU), lambda b, d, i: (d, 0, 0)),
        ],
        out_specs=pl.BlockSpec((1, 1, tm, D_LRU), lambda b, d, i: (d, b, tile(d, i), 0)),
        out_shape=jax.ShapeDtypeStruct((2, bn, t, D_LRU), F32),
        scratch_shapes=[pltpu.VMEM((1, D_LRU), F32),
                        pltpu.VMEM((D_LRU // LANES, tm, LANES), F32),
                        pltpu.VMEM((D_LRU // LANES, tm, LANES), F32)],
        compiler_params=_cparams(("parallel", "arbitrary", "arbitrary")),
        name="rg_lru",
    )(proj_lru, proj_lru, proj_lru, conv_w, conv_b, wg, bg, sp)


def _rwprep_kernel(z_ref, prev_ref, next_ref, mup_ref, mun_ref, w0_ref, wup_ref, a0_ref, aup_ref, gup_ref,
                   kk_ref, ka_ref, rk_ref, ones_ref,
                   r_out, kkn_out, v_out, kd_out, kka_out, lw_out, bonus_out, g_out, *, n_tiles):
    i = pl.program_id(1)
    tm = z_ref.shape[1]
    zc = z_ref[0]
    prev8 = jnp.where(i == 0, 0.0, prev_ref[0])
    next8 = jnp.where(i == n_tiles - 1, 0.0, next_ref[0])
    zp = _rows_before(zc, prev8, 1)
    zn = _rows_after(zc, next8, 1)
    z = zc + mup_ref[...] * (zp - zc) + mun_ref[...] * (zn - zc)

    r, k, v, wd, ad, gd = [z[:, lo:hi] for lo, hi in zip(RW_OFFSETS[:-1], RW_OFFSETS[1:])]
    ones_bd = ones_ref[...]

    kkr = k * kk_ref[...]
    ss = _dot_hilo(kkr * kkr, ones_bd)
    kkn = kkr * jnp.minimum(lax.rsqrt(ss), 1e12)

    wlin = w0_ref[...] + _dot(jnp.tanh(wd).astype(BF16), wup_ref[...])
    lw = (-np.exp(-0.5).astype(np.float32)) * _sigmoid(wlin)
    a = _sigmoid(a0_ref[...] + _dot(ad.astype(BF16), aup_ref[...]))

    ka = ka_ref[...]
    k_sum = jnp.zeros_like(k)
    for d in range(2):
        a_d = a[:, d * D_RWKV:(d + 1) * D_RWKV]
        kd = k * (1.0 + (a_d - 1.0) * ka)
        kd_out[d, 0] = kd.astype(BF16)
        kka_out[d, 0] = (kkn * a_d).astype(BF16)
        lw_out[d, 0] = lw[:, d * D_RWKV:(d + 1) * D_RWKV]
        k_sum = k_sum + kd
    r_out[0] = r.astype(BF16)
    kkn_out[0] = kkn.astype(BF16)
    v_out[0] = v.astype(BF16)
    bonus_out[0] = (_dot_hilo(r * k_sum * rk_ref[...], ones_bd) * v).astype(BF16)
    g_out[0] = _dot(_sigmoid(gd).astype(BF16), gup_ref[...]).astype(BF16)


def _rwprep_call(proj_rw, mup, mun, w0, wup_bd, a0, aup_bd, gup, k_k, k_a, r_k, ones_bd, tm):
    bn, t, _ = proj_rw.shape
    nt = t // tm
    r8 = tm // SUBLANES
    n8 = t // SUBLANES
    c2 = lambda b, i: (0, 0)
    tok = pl.BlockSpec((1, tm, D_RWKV), lambda b, i: (b, i, 0))
    tok2 = pl.BlockSpec((2, 1, tm, D_RWKV), lambda b, i: (0, b, i, 0))
    s1 = jax.ShapeDtypeStruct((bn, t, D_RWKV), BF16)
    s2 = jax.ShapeDtypeStruct((2, bn, t, D_RWKV), BF16)
    s2f = jax.ShapeDtypeStruct((2, bn, t, D_RWKV), F32)
    return pl.pallas_call(
        functools.partial(_rwprep_kernel, n_tiles=nt),
        grid=(bn, nt),
        in_specs=[
            pl.BlockSpec((1, tm, D_RW_IN), lambda b, i: (b, i, 0)),
            pl.BlockSpec((1, SUBLANES, D_RW_IN), lambda b, i: (b, jnp.maximum(i * r8 - 1, 0), 0)),
            pl.BlockSpec((1, SUBLANES, D_RW_IN), lambda b, i: (b, jnp.minimum((i + 1) * r8, n8 - 1), 0)),
            pl.BlockSpec((1, D_RW_IN), c2),
            pl.BlockSpec((1, D_RW_IN), c2),
            pl.BlockSpec((1, 2 * D_RWKV), c2),
            pl.BlockSpec((LANES, 2 * D_RWKV), c2),
            pl.BlockSpec((1, 2 * D_RWKV), c2),
            pl.BlockSpec((LANES, 2 * D_RWKV), c2),
            pl.BlockSpec((LANES, D_RWKV), c2),
            pl.BlockSpec((1, D_RWKV), c2),
            pl.BlockSpec((1, D_RWKV), c2),
            pl.BlockSpec((1, D_RWKV), c2),
            pl.BlockSpec((D_RWKV, D_RWKV), c2),
        ],
        out_specs=[tok, tok, tok, tok2, tok2, tok2, tok, tok],
        out_shape=[s1, s1, s1, s2, s2, s2f, s1, s1],
        compiler_params=_cparams(("parallel", "parallel")),
        name="rwkv_prep",
    )(proj_rw, proj_rw, proj_rw, mup, mun, w0, wup_bd, a0, aup_bd, gup, k_k, k_a, r_k, ones_bd)


def _stack2(y):
    yb = y.astype(BF16)
    lo = _cols(yb.shape) < RWKV_HEAD
    zero = jnp.zeros_like(yb)
    return jnp.concatenate([jnp.where(lo, yb, zero), jnp.where(lo, zero, yb)], axis=0)


def _pair_mm(x, y):
    return _dot(x.astype(BF16), _stack2(y))


class _Masks:
    def __init__(self, reverse):
        shp = (CHUNK, PAIR)
        t = _rows(shp)
        s = _cols(shp) % CHUNK
        self.strict = (s > t) if reverse else (s < t)
        self.incl = (s >= t) if reverse else (s <= t)
        self.eye = s == t
        self.blk16 = (t // 16) == (s // 16)
        self.lvl32 = ((t // 32) == (s // 32)) & ((t // 16) != (s // 16))
        self.lvl64 = (t // 32) != (s // 32)
        sq = (PAIR, PAIR)
        self.bd = (_rows(sq) // RWKV_HEAD) == (_cols(sq) // RWKV_HEAD)
        self.eye_sq = _rows(sq) == _cols(sq)
        tt = _rows((CHUNK, CHUNK))
        ss = _cols((CHUNK, CHUNK))
        self.tri = jnp.where((ss >= tt) if reverse else (ss <= tt), 1.0, 0.0).astype(BF16)
        self.reverse = reverse


def _dot_hilo_l(w_bf16, x):
    hi, lo = _split(x)
    return _dot(w_bf16, hi) + _dot(w_bf16, lo)


def _col(x, p):
    return x[:, p * PAIR:(p + 1) * PAIR]


def _wkv_kernel(rf_ref, kkf_ref, vf_ref, kdf_ref, kkaf_ref, lwf_ref,
                rb_ref, kkb_ref, vb_ref, kdb_ref, kkab_ref, lwb_ref,
                yf_ref, yb_ref, h_ref, *, n_sub):
    @pl.when(pl.program_id(1) == 0)
    def _():
        h_ref[...] = jnp.zeros_like(h_ref)

    dirs = (
        (False, rf_ref, kkf_ref, vf_ref, kdf_ref, kkaf_ref, lwf_ref, yf_ref),
        (True, rb_ref, kkb_ref, vb_ref, kdb_ref, kkab_ref, lwb_ref, yb_ref),
    )
    masks = (_Masks(False), _Masks(True))
    chunks = [(di, j) for di in range(2) for j in range(n_sub)]
    units = [(di, j, p) for (di, j) in chunks for p in range(N_PAIRS)]

    rb, ab, bt, kt, bh, kh, vv, etot = {}, {}, {}, {}, {}, {}, {}, {}
    for c in chunks:
        di, j = c
        reverse, r_ref, kk_ref, v_ref, kd_ref, kka_ref, lw_ref, _ = dirs[di]
        rs = slice(j * CHUNK, (j + 1) * CHUNK)
        lw = lw_ref[0, 0, rs, :]
        kd = kd_ref[0, 0, rs, :].astype(F32)
        kka = kka_ref[0, 0, rs, :].astype(F32)
        cum = _dot_hilo_l(masks[di].tri, lw)
        tot = cum[0:1] if reverse else cum[CHUNK - 1: CHUNK]
        rb[c] = r_ref[0, rs, :].astype(F32) * jnp.exp(cum)
        ab[c] = -kk_ref[0, rs, :].astype(F32) * jnp.exp(cum - lw)
        ip = jnp.exp(-cum)
        bt[c] = kka * ip
        kt[c] = kd * ip
        ph = jnp.exp(tot - cum)
        bh[c] = kka * ph
        kh[c] = kd * ph
        vv[c] = v_ref[0, rs, :]
        etot[c] = jnp.exp(tot)

    def per_unit(fn):
        return {u: fn(u, (u[0], u[1]), u[2], masks[u[0]]) for u in units}

    s_all = per_unit(lambda u, c, p, m: _dot_nt(
        jnp.concatenate([_col(ab[c], p), _col(rb[c], p)], axis=0).astype(BF16),
        jnp.concatenate([_stack2(_col(bt[c], p)), _stack2(_col(kt[c], p))], axis=0)))
    n_ab = per_unit(lambda u, c, p, m: jnp.where(m.strict, s_all[u][:CHUNK, :PAIR], 0.0))
    a_ak = per_unit(lambda u, c, p, m: jnp.where(m.strict, s_all[u][:CHUNK, PAIR:], 0.0).astype(BF16))
    m_rb = per_unit(lambda u, c, p, m: jnp.where(m.incl, s_all[u][CHUNK:, :PAIR], 0.0).astype(BF16))
    m_rk = per_unit(lambda u, c, p, m: jnp.where(m.incl, s_all[u][CHUNK:, PAIR:], 0.0).astype(BF16))

    nd = per_unit(lambda u, c, p, m: jnp.where(m.blk16, n_ab[u], 0.0))
    t_inv = per_unit(lambda u, c, p, m: jnp.where(m.eye, 1.0, nd[u]))
    pw = per_unit(lambda u, c, p, m: _pair_mm(nd[u], nd[u]))
    for step in range(2):
        tp = per_unit(lambda u, c, p, m: _pair_mm(jnp.concatenate([t_inv[u], pw[u]], axis=0), pw[u]))
        t_inv = per_unit(lambda u, c, p, m: t_inv[u] + tp[u][:CHUNK])
        pw = per_unit(lambda u, c, p, m: tp[u][CHUNK:])
    t_inv = per_unit(lambda u, c, p, m: t_inv[u] + _pair_mm(t_inv[u], pw[u]))
    for lvl in ("lvl32", "lvl64"):
        tc = per_unit(lambda u, c, p, m: _pair_mm(t_inv[u], jnp.where(getattr(m, lvl), n_ab[u], 0.0)))
        t_inv = per_unit(lambda u, c, p, m: t_inv[u] + _pair_mm(tc[u], t_inv[u]))

    v2 = per_unit(lambda u, c, p, m: _stack2(_col(vv[c], p)))
    av = per_unit(lambda u, c, p, m: _dot(jnp.concatenate([a_ak[u], m_rk[u]], axis=0), v2[u]))
    akv = per_unit(lambda u, c, p, m: av[u][:CHUNK])
    wu = per_unit(lambda u, c, p, m: _dot(
        t_inv[u].astype(BF16), jnp.concatenate([_stack2(_col(ab[c], p)), _stack2(akv[u])], axis=1)))
    qy = per_unit(lambda u, c, p, m: _dot(
        m_rb[u], jnp.concatenate([_stack2(wu[u][:, :PAIR]), _stack2(wu[u][:, PAIR:])], axis=1)))
    q_hat = per_unit(lambda u, c, p, m: (_col(rb[c], p) + qy[u][:, :PAIR]).astype(BF16))
    y_loc = per_unit(lambda u, c, p, m: qy[u][:, PAIR:] + av[u][CHUNK:])
    gd = per_unit(lambda u, c, p, m: _dot(_col(bh[c], p).T.astype(BF16), wu[u].astype(BF16)))
    kv = per_unit(lambda u, c, p, m: _dot(_col(kh[c], p).T.astype(BF16), _col(vv[c], p).astype(BF16)))
    g_m = per_unit(lambda u, c, p, m: (jnp.where(m.bd, gd[u][:, :PAIR], 0.0)
                                       + jnp.where(m.eye_sq, _col(etot[c], p), 0.0)).astype(BF16))
    d_m = per_unit(lambda u, c, p, m: jnp.where(m.bd, gd[u][:, PAIR:] + kv[u], 0.0))

    h = {(di, p): h_ref[di, p] for di in range(2) for p in range(N_PAIRS)}
    for step in range(n_sub):
        for di in range(2):
            j = n_sub - 1 - step if dirs[di][0] else step
            y_ref = dirs[di][7]
            rs = slice(j * CHUNK, (j + 1) * CHUNK)
            for p in range(N_PAIRS):
                u = (di, j, p)
                hb = h[(di, p)].astype(BF16)
                qh = _dot(jnp.concatenate([q_hat[u], g_m[u]], axis=0), hb)
                y_ref[0, rs, p * PAIR:(p + 1) * PAIR] = (qh[:CHUNK] + y_loc[u]).astype(BF16)
                h[(di, p)] = qh[CHUNK:] + d_m[u]
    for di in range(2):
        for p in range(N_PAIRS):
            h_ref[di, p] = h[(di, p)]


def _wkv_call(r, kkn, v, kd, kka, lw, tm):
    bn, t, _ = r.shape
    nt = t // tm
    fwd = pl.BlockSpec((1, tm, D_RWKV), lambda b, i: (b, i, 0))
    bwd = pl.BlockSpec((1, tm, D_RWKV), lambda b, i: (b, nt - 1 - i, 0))
    fwd2 = pl.BlockSpec((1, 1, tm, D_RWKV), lambda b, i: (0, b, i, 0))
    bwd2 = pl.BlockSpec((1, 1, tm, D_RWKV), lambda b, i: (1, b, nt - 1 - i, 0))
    s1 = jax.ShapeDtypeStruct((bn, t, D_RWKV), BF16)
    return pl.pallas_call(
        functools.partial(_wkv_kernel, n_sub=tm // CHUNK),
        grid=(bn, nt),
        in_specs=[fwd, fwd, fwd, fwd2, fwd2, fwd2, bwd, bwd, bwd, bwd2, bwd2, bwd2],
        out_specs=[fwd, bwd],
        out_shape=[s1, s1],
        scratch_shapes=[pltpu.VMEM((2, N_PAIRS, PAIR, PAIR), F32)],
        compiler_params=_cparams(("parallel", "arbitrary")),
        name="wkv7_chunked",
    )(r, kkn, v, kd, kka, lw, r, kkn, v, kd, kka, lw)


def _gelu_tanh(x):
    return 0.5 * x * (1.0 + jnp.tanh(0.7978845608028654 * (x + 0.044715 * (x * x * x))))


def _route(logits_t, b_col):
    rows = [logits_t[e:e + 1] for e in range(N_EXPERTS)]
    mx = functools.reduce(jnp.maximum, rows)
    ex = [jnp.exp(x - mx) for x in rows]
    den = functools.reduce(lambda a, b: a + b, ex)
    inv_den = 1.0 / den
    probs = [e * inv_den for e in ex]
    sel = [probs[e] + b_col[e:e + 1] for e in range(N_EXPERTS)]
    scores = []
    for g in range(N_GROUPS):
        a, b, c, d = sel[4 * g: 4 * g + 4]
        hi1, lo1 = jnp.maximum(a, b), jnp.minimum(a, b)
        hi2, lo2 = jnp.maximum(c, d), jnp.minimum(c, d)
        scores.append(jnp.maximum(hi1, hi2) + jnp.maximum(jnp.minimum(hi1, hi2), jnp.maximum(lo1, lo2)))
    best = scores[0]
    bg = jnp.zeros_like(best)
    for g in range(1, N_GROUPS):
        upd = scores[g] > best
        best = jnp.where(upd, scores[g], best)
        bg = jnp.where(upd, float(g), bg)

    def pick(vals):
        out = []
        for j in range(EXP_PER_GROUP):
            x = vals[j]
            for g in range(1, N_GROUPS):
                x = jnp.where(bg == float(g), vals[4 * g + j], x)
            out.append(x)
        return out

    sg = pick(sel)
    v1, i1 = sg[0], jnp.zeros_like(best)
    for j in range(1, EXP_PER_GROUP):
        upd = sg[j] > v1
        v1 = jnp.where(upd, sg[j], v1)
        i1 = jnp.where(upd, float(j), i1)
    neg = jnp.full_like(best, -jnp.inf)
    v2, i2 = neg, jnp.zeros_like(best)
    for j in range(EXP_PER_GROUP):
        cand = jnp.where(i1 == float(j), neg, sg[j])
        upd = cand > v2
        v2 = jnp.where(upd, cand, v2)
        i2 = jnp.where(upd, float(j), i2)
    return bg * float(EXP_PER_GROUP) + i1, bg * float(EXP_PER_GROUP) + i2


def _outproj_kernel(hl_ref, gate_ref, yf_ref, yb_ref, bonus_ref, g_ref, xa_ref, xb_ref, lnw_ref, lnb_ref, g1_ref,
                    sh2_ref, sc2_ref, n2_ref, wout_ref, wr_ref, br_ref, ones_ref,
                    xn_ref, h2_ref, route_ref, *, n_a):
    lru_out = (hl_ref[0, 0] + hl_ref[1, 0]) * _gelu_tanh(gate_ref[0])
    ones_bd = ones_ref[...]
    y = yf_ref[0].astype(F32) + yb_ref[0].astype(F32)
    mu = _dot_hilo(y, ones_bd) * (1.0 / RWKV_HEAD)
    yc = y - mu
    var = _dot_hilo(yc * yc, ones_bd) * (1.0 / RWKV_HEAD)
    gn = yc * lax.rsqrt(var + GN_EPS) * lnw_ref[...] + lnb_ref[...]
    rw_out = (gn + bonus_ref[0].astype(F32)) * g_ref[0].astype(F32)
    o = _dot(lru_out.astype(BF16), wout_ref[:D_LRU]) + _dot(rw_out.astype(BF16), wout_ref[D_LRU:])
    xn = _x_block(xa_ref, xb_ref, n_a) + g1_ref[0] * o
    xn_ref[0] = xn
    h2 = _norm_mod(xn, n2_ref[...], sc2_ref[0], sh2_ref[0])
    for j in range(TOK_ROWS):
        h2_ref[0, pl.ds(j, xn.shape[0], stride=TOK_ROWS), :] = h2[:, j * LANES:(j + 1) * LANES]
    logits_t = _dot3_nt(wr_ref[...], h2)
    e0, e1 = _route(logits_t, br_ref[...])
    zero = jnp.zeros_like(e0)
    route_ref[0, 0] = jnp.concatenate([e0, e1, zero, zero, zero, zero, zero, zero], axis=0)


def _outproj_call(hl, proj_lru, yf, yb, bonus, g, xa, xb, lnw, lnb, g1, sh2, sc2, n2, w_out_bf16, wr_t, br, ones_bd,
                  tm):
    n_a, t, d = xa.shape
    bn = g1.shape[0]
    nt = t // tm
    spec_a, spec_b = _x_specs(tm, d, n_a, nt)
    c2 = lambda b, i: (0, 0)
    tok = pl.BlockSpec((1, tm, D_RWKV), lambda b, i: (b, i, 0))
    tokd = pl.BlockSpec((1, tm, d), lambda b, i: (b, i, 0))
    per_b = pl.BlockSpec((1, 1, d), lambda b, i: (b, 0, 0))
    return pl.pallas_call(
        functools.partial(_outproj_kernel, n_a=n_a),
        grid=(bn, nt),
        in_specs=[
            pl.BlockSpec((2, 1, tm, D_LRU), lambda b, i: (0, b, i, 0)),
            pl.BlockSpec((1, tm, D_LRU), lambda b, i: (b, i, 1)),
            tok, tok, tok, tok, spec_a, spec_b,
            pl.BlockSpec((1, D_RWKV), c2),
            pl.BlockSpec((1, D_RWKV), c2),
            per_b, per_b, per_b,
            pl.BlockSpec((1, d), c2),
            pl.BlockSpec((d, d), c2),
            pl.BlockSpec((N_EXPERTS, d), c2),
            pl.BlockSpec((N_EXPERTS, 1), c2),
            pl.BlockSpec((D_RWKV, D_RWKV), c2),
        ],
        out_specs=[tokd, pl.BlockSpec((1, tm * TOK_ROWS, LANES), lambda b, i: (b, i, 0)),
                   pl.BlockSpec((1, 1, SUBLANES, tm), lambda b, i: (b, i, 0, 0))],
        out_shape=[
            jax.ShapeDtypeStruct((bn, t, d), F32),
            jax.ShapeDtypeStruct((bn, t * TOK_ROWS, LANES), F32),
            jax.ShapeDtypeStruct((bn, nt, SUBLANES, tm), F32),
        ],
        compiler_params=_cparams(("parallel", "parallel")),
        name="outproj_router",
    )(hl, proj_lru, yf, yb, bonus, g, xa, xb, lnw, lnb, g1, sh2, sc2, n2, w_out_bf16, wr_t, br, ones_bd)


def _row_copy(src_ref, src_row, dst_ref, dst_row, sem):
    src = src_ref.at[pl.ds(pl.multiple_of(src_row * TOK_ROWS, TOK_ROWS), TOK_ROWS)]
    dst = dst_ref.at[pl.ds(pl.multiple_of(dst_row * TOK_ROWS, TOK_ROWS), TOK_ROWS)]
    return pltpu.make_async_copy(src, dst, sem)


def _dispatch_kernel(pos_ref, h_ref, xb_in_ref, xb_ref, sem):
    del xb_in_ref
    tm = h_ref.shape[0] // TOK_ROWS

    def issue(r, carry):
        _row_copy(h_ref, r, xb_ref, pos_ref[0, 0, r], sem).start()
        return carry

    lax.fori_loop(0, tm, issue, 0, unroll=8)

    def drain(r, carry):
        _row_copy(h_ref, 0, xb_ref, 0, sem).wait()
        return carry

    lax.fori_loop(0, tm, drain, 0, unroll=True)


def _dispatch_call(h2, pos_tiles, xb_init, tm):
    n = h2.shape[0] // TOK_ROWS
    nt = n // tm
    return pl.pallas_call(
        _dispatch_kernel,
        grid=(nt,),
        in_specs=[
            pl.BlockSpec((1, 1, tm), lambda i: (i, 0, 0), memory_space=pltpu.SMEM),
            pl.BlockSpec((tm * TOK_ROWS, LANES), lambda i: (i, 0)),
            pl.BlockSpec(memory_space=pl.ANY),
        ],
        out_specs=pl.BlockSpec(memory_space=pl.ANY),
        out_shape=jax.ShapeDtypeStruct(xb_init.shape, F32),
        scratch_shapes=[pltpu.SemaphoreType.DMA(())],
        input_output_aliases={2: 0},
        compiler_params=_cparams(("arbitrary",)),
        name="moe_dispatch",
    )(pos_tiles, h2, xb_init)


def _expert_kernel(blk_a_ref, blk_b_ref, n_used_ref, x_ref, wr_ref,
                   wga_ref, wua_ref, wda_ref, wgb_ref, wub_ref, wdb_ref, o_ref):
    i = pl.program_id(0)

    @pl.when(i < n_used_ref[0])
    def _():
        x32 = jnp.concatenate([x_ref[pl.ds(j, MOE_BLOCK, stride=TOK_ROWS), :] for j in range(TOK_ROWS)], axis=1)
        x = x32.astype(BF16)
        w_diff = wr_ref[pl.ds(blk_a_ref[i], 1), :] - wr_ref[pl.ds(blk_b_ref[i], 1), :]
        l_diff = jnp.sum(x32 * w_diff, axis=-1, keepdims=True)
        g_a = 1.0 / (1.0 + jnp.exp(-l_diff))
        g_b = 1.0 / (1.0 + jnp.exp(l_diff))

        def ffn(wg_ref, wu_ref, wd_ref):
            gate = _dot(x, wg_ref[0])
            hid = gate * _sigmoid(gate) * _dot(x, wu_ref[0])
            return _dot(hid.astype(BF16), wd_ref[0])

        y = g_a * ffn(wga_ref, wua_ref, wda_ref) + g_b * ffn(wgb_ref, wub_ref, wdb_ref)
        for j in range(TOK_ROWS):
            o_ref[pl.ds(j, MOE_BLOCK, stride=TOK_ROWS), :] = y[:, j * LANES:(j + 1) * LANES]

    @pl.when(i >= n_used_ref[0])
    def _():
        o_ref[...] = jnp.zeros_like(o_ref)


def _expert_call(xb, blk_a, blk_b, n_used, wr_t, wg, wu, wd):
    n_blk = xb.shape[0] // (MOE_BLOCK * TOK_ROWS)
    d = D_MODEL
    w_in_a = pl.BlockSpec((1, d, D_EXPERT), lambda i, ba, bb, nu: (ba[i], 0, 0))
    w_in_b = pl.BlockSpec((1, d, D_EXPERT), lambda i, ba, bb, nu: (bb[i], 0, 0))
    grid_spec = pltpu.PrefetchScalarGridSpec(
        num_scalar_prefetch=3,
        grid=(n_blk,),
        in_specs=[
            pl.BlockSpec((MOE_BLOCK * TOK_ROWS, LANES), lambda i, ba, bb, nu: (i, 0)),
            pl.BlockSpec((N_EXPERTS, d), lambda i, ba, bb, nu: (0, 0)),
            w_in_a, w_in_a,
            pl.BlockSpec((1, D_EXPERT, d), lambda i, ba, bb, nu: (ba[i], 0, 0)),
            w_in_b, w_in_b,
            pl.BlockSpec((1, D_EXPERT, d), lambda i, ba, bb, nu: (bb[i], 0, 0)),
        ],
        out_specs=pl.BlockSpec((MOE_BLOCK * TOK_ROWS, LANES), lambda i, ba, bb, nu: (i, 0)),
    )
    return pl.pallas_call(
        _expert_kernel,
        grid_spec=grid_spec,
        out_shape=jax.ShapeDtypeStruct(xb.shape, F32),
        compiler_params=_cparams(("arbitrary",)),
        name="moe_experts",
    )(blk_a, blk_b, n_used, xb, wr_t, wg, wu, wd, wg, wu, wd)


def _combine_kernel(pos_ref, posn_ref, x_ref, g2_ref, nf_ref, yb_ref, *rest, n_first):
    *o_refs, buf_ref, sem = rest
    i = pl.program_id(0)
    n = pl.num_programs(0)
    tm = x_ref.shape[0]
    slot = i % 2

    def gather(p_ref, s):
        def issue(r, carry):
            _row_copy(yb_ref, p_ref[0, 0, r], buf_ref.at[s], r, sem.at[s]).start()
            return carry

        lax.fori_loop(0, tm, issue, 0, unroll=8)

    @pl.when(i == 0)
    def _():
        gather(pos_ref, 0)

    @pl.when(i + 1 < n)
    def _():
        gather(posn_ref, 1 - slot)

    def drain(r, carry):
        _row_copy(yb_ref, 0, buf_ref.at[slot], 0, sem.at[slot]).wait()
        return carry

    lax.fori_loop(0, tm, drain, 0, unroll=True)
    for j in range(TOK_ROWS):
        cs = slice(j * LANES, (j + 1) * LANES)
        y = buf_ref[slot, pl.ds(j, tm, stride=TOK_ROWS), :]
        o_refs[-1][:, cs] = x_ref[:, cs] + g2_ref[0, :, cs] * y
    if n_first is not None:
        o_first, o_second = o_refs
        xo = o_second[...]
        ms = jnp.mean(xo * xo, axis=-1, keepdims=True)
        res = xo * lax.rsqrt(ms + NORM_EPS) * nf_ref[...]
        o_second[...] = res

        @pl.when(i < n_first)
        def _():
            o_first[...] = res


def _combine_call(xn, yb, pos_tiles, g2, nf, t, tm, n_first=None):
    n, d = xn.shape
    nt = n // tm
    per_b = t // tm
    if n_first is None:
        out_specs = pl.BlockSpec((tm, d), lambda i: (i, 0))
        out_shape = jax.ShapeDtypeStruct((n, d), F32)
    else:
        out_specs = [pl.BlockSpec((tm, d), lambda i: (jnp.minimum(i, n_first - 1), 0)),
                     pl.BlockSpec((tm, d), lambda i: (jnp.maximum(i - n_first, 0), 0))]
        out_shape = [jax.ShapeDtypeStruct((n_first * tm, d), F32), jax.ShapeDtypeStruct((n - n_first * tm, d), F32)]
    return pl.pallas_call(
        functools.partial(_combine_kernel, n_first=n_first),
        grid=(nt,),
        in_specs=[
            pl.BlockSpec((1, 1, tm), lambda i: (i, 0, 0), memory_space=pltpu.SMEM),
            pl.BlockSpec((1, 1, tm), lambda i: (jnp.minimum(i + 1, nt - 1), 0, 0), memory_space=pltpu.SMEM),
            pl.BlockSpec((tm, d), lambda i: (i, 0)),
            pl.BlockSpec((1, 1, d), lambda i: (i // per_b, 0, 0)),
            pl.BlockSpec((1, d), lambda i: (0, 0)),
            pl.BlockSpec(memory_space=pl.ANY),
        ],
        out_specs=out_specs,
        out_shape=out_shape,
        scratch_shapes=[pltpu.VMEM((2, tm * TOK_ROWS, LANES), F32), pltpu.SemaphoreType.DMA((2,))],
        compiler_params=_cparams(("arbitrary",)),
        name="moe_combine",
    )(pos_tiles, pos_tiles, xn, g2, nf, yb)


def _block_diag(w):
    h, a, b = w.shape
    eye = jnp.eye(h, dtype=w.dtype)
    return jnp.einsum("hab,hg->hagb", w, eye).reshape(h * a, h * b)


def _head_ones():
    idx = np.arange(D_RWKV) // RWKV_HEAD
    return jnp.asarray((idx[:, None] == idx[None, :]).astype(np.float32), dtype=BF16)


_PAIRS = [(i, j) for i in range(EXP_PER_GROUP) for j in range(i + 1, EXP_PER_GROUP)]
N_CLASSES = N_GROUPS * len(_PAIRS)


def _routing_tables(route, n_tok, tm):
    flat = jnp.transpose(route, (2, 0, 1, 3)).reshape(SUBLANES, n_tok)
    e0 = flat[0].astype(jnp.int32)
    e1 = flat[1].astype(jnp.int32)
    ea = jnp.minimum(e0, e1)
    eb = jnp.maximum(e0, e1)
    i = ea % EXP_PER_GROUP
    j = eb % EXP_PER_GROUP
    cls = (ea // EXP_PER_GROUP) * len(_PAIRS) + (i * (2 * EXP_PER_GROUP - 1 - i)) // 2 + (j - i - 1)
    ar = jnp.arange(N_CLASSES, dtype=jnp.int32)
    oh = (cls[:, None] == ar).astype(jnp.int32)
    cs = jnp.cumsum(oh, axis=0)
    counts = cs[-1]
    padded = (counts + MOE_BLOCK - 1) // MOE_BLOCK * MOE_BLOCK
    pad_end = jnp.cumsum(padded)
    base = (cs - oh) + (pad_end - padded)[None, :]
    pos = jnp.take_along_axis(base, cls[:, None], axis=1)[:, 0]
    n_blk = (n_tok + N_CLASSES * (MOE_BLOCK - 1) + MOE_BLOCK - 1) // MOE_BLOCK
    blk_start = jnp.arange(n_blk, dtype=jnp.int32) * MOE_BLOCK
    blk_cls = jnp.minimum(jnp.sum(pad_end[None, :] <= blk_start[:, None], axis=-1), N_CLASSES - 1)
    cls_a = np.array([g * EXP_PER_GROUP + p[0] for g in range(N_GROUPS) for p in _PAIRS], np.int32)
    cls_b = np.array([g * EXP_PER_GROUP + p[1] for g in range(N_GROUPS) for p in _PAIRS], np.int32)
    blk_a = jnp.asarray(cls_a)[blk_cls]
    blk_b = jnp.asarray(cls_b)[blk_cls]
    n_used = (pad_end[-1] // MOE_BLOCK).astype(jnp.int32).reshape(1)
    pos_tiles = pos.reshape(n_tok // tm, 1, tm).astype(jnp.int32)
    return pos_tiles, blk_a, blk_b, n_used, n_blk * MOE_BLOCK


def _tiles(t):
    want = dict(inproj=512, lru=512, prep=512, wkv=512, moe=512)
    return {k: min(t, v) for k, v in want.items()}


def _trunk(x_a, x_b, c, w_mod, b_mod, norm1, norm2, w_in, w_out, conv_w, conv_b, lru_wa, lru_ba, lru_wx, lru_bx, lru_lam,
           mu_prev, mu_next, rw_w0, rw_wup, rw_a0, rw_aup, rw_gup, rw_kk, rw_ka, rw_rk, ln_x_w, ln_x_b,
           w_router, b_router, exp_gate, exp_up, exp_down, norm_f):
    n_first, t, d = x_a.shape
    bn = n_first + x_b.shape[0]
    n_tok = bn * t
    depth = w_mod.shape[0]
    ones_bd = _head_ones()
    mod = _mod_call(c, w_mod, b_mod)
    wr_t = jnp.transpose(w_router)
    br = b_router.reshape(N_EXPERTS, 1)
    tiles = _tiles(t)
    tm_moe = tiles["moe"]
    xb = None

    for l in range(depth):
        sh1, sc1, g1, sh2, sc2, g2 = [m.reshape(bn, 1, d) for m in jnp.split(mod[l], 6, axis=-1)]
        proj_lru, proj_rw = _inproj_call(x_a, x_b, sh1, sc1, norm1[l].reshape(1, d), w_in[l].astype(BF16),
                                         tiles["inproj"])

        wg = jnp.stack([jnp.concatenate([_block_diag(lru_wa[l, dd]), _block_diag(lru_wx[l, dd])], axis=1)
                        for dd in range(2)]).astype(BF16)
        bg = jnp.concatenate([lru_ba[l], lru_bx[l]], axis=1).reshape(2, 1, 2 * D_LRU)
        sp = jax.nn.softplus(-lru_lam[l]).reshape(2, 1, D_LRU)
        hl = _lru_call(proj_lru, conv_w[l], conv_b[l].reshape(1, D_LRU), wg, bg, sp, tiles["lru"])

        zeros = jnp.zeros((64, D_RWKV), F32)
        wup_bd = jnp.concatenate([jnp.concatenate([rw_wup[l, 0], zeros], axis=1),
                                  jnp.concatenate([zeros, rw_wup[l, 1]], axis=1)], axis=0).astype(BF16)
        aup_bd = jnp.concatenate([jnp.concatenate([rw_aup[l, 0], zeros], axis=1),
                                  jnp.concatenate([zeros, rw_aup[l, 1]], axis=1)], axis=0).astype(BF16)
        r, kkn, v, kd, kka, lw, bonus, g = _rwprep_call(
            proj_rw, mu_prev[l].reshape(1, -1), mu_next[l].reshape(1, -1),
            rw_w0[l].reshape(1, -1), wup_bd, rw_a0[l].reshape(1, -1), aup_bd, rw_gup[l].astype(BF16),
            rw_kk[l].reshape(1, -1), rw_ka[l].reshape(1, -1), rw_rk[l].reshape(1, -1), ones_bd, tiles["prep"])
        yf, yb = _wkv_call(r, kkn, v, kd, kka, lw, tiles["wkv"])

        xn, h2, route = _outproj_call(
            hl, proj_lru, yf, yb, bonus, g, x_a, x_b, ln_x_w[l].reshape(1, -1), ln_x_b[l].reshape(1, -1),
            g1, sh2, sc2, norm2[l].reshape(1, d), w_out[l].astype(BF16), wr_t, br, ones_bd, tm_moe)

        pos_tiles, blk_a, blk_b, n_used, n_rows = _routing_tables(route, n_tok, tm_moe)
        xb_init = jnp.zeros((n_rows * TOK_ROWS, LANES), F32) if xb is None else xb
        xb = _dispatch_call(h2.reshape(n_tok * TOK_ROWS, LANES), pos_tiles, xb_init, tm_moe)
        ybuf = _expert_call(xb, blk_a, blk_b, n_used, wr_t, exp_gate[l].astype(BF16), exp_up[l].astype(BF16),
                            exp_down[l].astype(BF16))
        last = l == depth - 1
        out = _combine_call(xn.reshape(n_tok, d), ybuf, pos_tiles, g2, norm_f.reshape(1, d), t, tm_moe,
                            n_first=n_first * (t // tm_moe) if last else None)
        if not last:
            x_a = x_b = out.reshape(bn, t, d)
    return out[0].reshape(n_first, t, d), out[1].reshape(bn - n_first, t, d)


def kernel(x_prompt, x_sample, c_prompt, c_sample, w_mod, b_mod, norm1, norm2, w_in, w_out, conv_w, conv_b, lru_wa, lru_ba, lru_wx, lru_bx, lru_lam, mu_prev, mu_next, rw_w0, rw_wup, rw_a0, rw_aup, rw_gup, rw_kk, rw_ka, rw_rk, ln_x_w, ln_x_b, w_router, b_router, exp_gate, exp_up, exp_down, norm_f):
    c = jnp.concatenate([c_prompt, c_sample], axis=0).astype(F32)
    return _trunk(x_prompt, x_sample, c, w_mod, b_mod, norm1, norm2, w_in, w_out, conv_w, conv_b, lru_wa, lru_ba,
                  lru_wx, lru_bx, lru_lam, mu_prev, mu_next, rw_w0, rw_wup, rw_a0, rw_aup, rw_gup, rw_kk, rw_ka,
                  rw_rk, ln_x_w, ln_x_b, w_router, b_router, exp_gate, exp_up, exp_down, norm_f)
```
